```python
import jax, jax.numpy as jnp
from jax import lax
import numpy as np

D_MODEL = 1024
BATCH = 8
SEQ = 2048
DEPTH = 2
DEC_BATCH = 128
DEC_SEQ = 1
PAST_LEN = 16384
PAGE_SIZE = 128

N_META = 16
N_A_LAYERS = (DEPTH + 1) // 2
N_B_LAYERS = DEPTH // 2
SC_WIDTH = 3
GD_HK = 8
GD_HV = 16
GD_DH = 128
GD_K_DIM = GD_HK * GD_DH
GD_V_DIM = GD_HV * GD_DH
GD_CONV = 4
GD_CONV_CH = 2 * GD_K_DIM + GD_V_DIM
GD_PROJ = GD_CONV_CH + GD_V_DIM + 2 * GD_HV
GD_CHUNK = 64
D_FF = 3584
N_EXPERTS = 8
TOP_K = 2
MOE_BLOCK = 128
LN_EPS = 1e-5
RMS_EPS = 1e-6
DEEPNORM_ALPHA = (2 * DEPTH) ** 0.25
DEEPNORM_BETA = (8 * DEPTH) ** -0.25

kernel_name = 'hybrid_shortconv_gdn_moe_step'


def layer_norm(x, g, b):
    xf = x.astype(jnp.float32)
    mu = jnp.mean(xf, axis=-1, keepdims=True)
    var = jnp.mean(jnp.square(xf - mu), axis=-1, keepdims=True)
    return ((xf - mu) * lax.rsqrt(var + LN_EPS) * g.astype(jnp.float32) + b.astype(jnp.float32)).astype(x.dtype)


def causal_dwconv(xp, w):
    c = xp.shape[-1]
    return lax.conv_general_dilated(xp, w[:, None, :].astype(xp.dtype), window_strides=(1,), padding='VALID',
                                    dimension_numbers=('NWC', 'WIO', 'NWC'), feature_group_count=c)


def l2norm(x):
    return x * lax.rsqrt(jnp.sum(x * x, axis=-1, keepdims=True) + RMS_EPS)


def swiglu(x, w1, w3, w2):
    return (jax.nn.silu(x @ w1) * (x @ w3)) @ w2


def short_conv_mixer(x, conv_buf, w_in, conv_w, w_out):
    b_g, c_g, h = jnp.split(x @ w_in, 3, axis=-1)
    xp = jnp.concatenate([conv_buf.astype(x.dtype), c_g * h], axis=1)
    y = causal_dwconv(xp, conv_w)
    return (b_g * y) @ w_out, xp[:, -(SC_WIDTH - 1):]


def to_chunks(t, l):
    t = t.reshape((t.shape[0], t.shape[1] // l, l) + t.shape[2:])
    return jnp.swapaxes(t, 2, 3)


def from_chunks(o):
    b, n, h, l, d = o.shape
    return jnp.swapaxes(o, 2, 3).reshape(b, n * l, h, d)


def delta_rule_chunks(q, k, v, beta, g, s0):
    l = q.shape[-2]
    cum = jnp.cumsum(g, axis=-1)
    incl = jnp.tril(jnp.ones((l, l), bool))
    strict = jnp.tril(jnp.ones((l, l), bool), -1)
    decay = jnp.exp(jnp.where(incl, cum[..., :, None] - cum[..., None, :], -jnp.inf))
    kk = jnp.einsum('bnhtd,bnhsd->bnhts', k, k)
    a_mat = jnp.where(strict, decay * kk, 0.0) * beta[..., :, None]
    lhs = a_mat + jnp.eye(l, dtype=jnp.float32)
    rhs = jnp.concatenate([beta[..., None] * v, (beta * jnp.exp(cum))[..., None] * k], axis=-1)
    sol = lax.linalg.triangular_solve(lhs, rhs, left_side=True, lower=True, unit_diagonal=True)
    dv = v.shape[-1]
    u, wk = sol[..., :dv], sol[..., dv:]
    attn = jnp.einsum('bnhtd,bnhsd->bnhts', q, k) * decay
    q_dec = q * jnp.exp(cum)[..., None]
    k_dec = k * jnp.exp(cum[..., -1:] - cum)[..., None]
    g_last = jnp.exp(cum[..., -1])

    def step(s, xs):
        u_c, wk_c, qd_c, attn_c, kd_c, gl_c = xs
        w = u_c - jnp.einsum('bhld,bhdv->bhlv', wk_c, s)
        o = jnp.einsum('bhld,bhdv->bhlv', qd_c, s) + jnp.einsum('bhts,bhsv->bhtv', attn_c, w)
        s = gl_c[..., None, None] * s + jnp.einsum('bhld,bhlv->bhdv', kd_c, w)
        return s, o

    xs = tuple(jnp.moveaxis(t, 1, 0) for t in (u, wk, q_dec, attn, k_dec, g_last))
    s, o = lax.scan(step, s0, xs)
    return jnp.moveaxis(o, 0, 1), s


def delta_rule_prompt(q, k, v, beta, g, s0):
    lead = [to_chunks(t[:, :N_META], N_META) for t in (q, k, v, beta, g)]
    o_lead, s = delta_rule_chunks(*lead, s0)
    rest = [to_chunks(t[:, N_META:], GD_CHUNK) for t in (q, k, v, beta, g)]
    o_rest, s = delta_rule_chunks(*rest, s)
    return jnp.concatenate([from_chunks(o_lead), from_chunks(o_rest)], axis=1), s


def delta_rule_recurrent(q, k, v, beta, g, s0):
    def step(s, xs):
        q_t, k_t, v_t, b_t, g_t = xs
        s = jnp.exp(g_t)[..., None, None] * s
        w = b_t[..., None] * (v_t - jnp.einsum('bhd,bhdv->bhv', k_t, s))
        s = s + jnp.einsum('bhd,bhv->bhdv', k_t, w)
        return s, jnp.einsum('bhd,bhdv->bhv', q_t, s)

    xs = tuple(jnp.moveaxis(t, 1, 0) for t in (q, k, v, beta, g))
    s, o = lax.scan(step, s0, xs)
    return jnp.moveaxis(o, 0, 1), s


def gated_deltanet_mixer(x, conv_buf, s0, w_in, conv_w, a_log, dt_bias, norm_g, w_out, chunked):
    bs, t, _ = x.shape
    f32 = jnp.float32
    proj = x @ w_in
    qkv_in = proj[..., :GD_CONV_CH]
    z = proj[..., GD_CONV_CH:GD_CONV_CH + GD_V_DIM].reshape(bs, t, GD_HV, GD_DH)
    b = proj[..., GD_CONV_CH + GD_V_DIM:GD_CONV_CH + GD_V_DIM + GD_HV]
    a = proj[..., GD_CONV_CH + GD_V_DIM + GD_HV:]
    xp = jnp.concatenate([conv_buf.astype(x.dtype), qkv_in], axis=1)
    qkv = jax.nn.silu(causal_dwconv(xp, conv_w)).astype(f32)
    q = l2norm(qkv[..., :GD_K_DIM].reshape(bs, t, GD_HK, GD_DH)) * (GD_DH ** -0.5)
    k = l2norm(qkv[..., GD_K_DIM:2 * GD_K_DIM].reshape(bs, t, GD_HK, GD_DH))
    v = qkv[..., 2 * GD_K_DIM:].reshape(bs, t, GD_HV, GD_DH)
    q = jnp.repeat(q, GD_HV // GD_HK, axis=2)
    k = jnp.repeat(k, GD_HV // GD_HK, axis=2)
    beta = jax.nn.sigmoid(b.astype(f32))
    g = -jnp.exp(a_log.astype(f32)) * jax.nn.softplus(a.astype(f32) + dt_bias.astype(f32))
    s0 = s0.astype(f32)
    if chunked:
        o, s = delta_rule_prompt(q, k, v, beta, g, s0)
    else:
        o, s = delta_rule_recurrent(q, k, v, beta, g, s0)
    o = o * lax.rsqrt(jnp.mean(o * o, axis=-1, keepdims=True) + RMS_EPS) * norm_g.astype(f32) * jax.nn.silu(z.astype(f32))
    y = o.reshape(bs, t, GD_V_DIM).astype(x.dtype) @ w_out
    return y, xp[:, -(GD_CONV - 1):], s


def moe_swiglu(x, w_router, w1, w3, w2):
    bs, t, d = x.shape
    xt = x.reshape(-1, d)
    n = xt.shape[0]
    probs = jax.nn.softmax((xt @ w_router).astype(jnp.float32), axis=-1)
    topv, topi = lax.top_k(probs, TOP_K)
    topv = topv / jnp.sum(topv, axis=-1, keepdims=True)
    eid = topi.reshape(-1)
    tok = jnp.repeat(jnp.arange(n), TOP_K)
    order = jnp.argsort(eid)
    counts = jnp.bincount(eid, length=N_EXPERTS)
    starts = jnp.cumsum(counts) - counts
    padded = (counts + MOE_BLOCK - 1) // MOE_BLOCK * MOE_BLOCK
    pends = jnp.cumsum(padded)
    pstarts = pends - padded
    sorted_e = eid[order]
    dest_sorted = pstarts[sorted_e] + (jnp.arange(n * TOP_K) - starts[sorted_e])
    dest = jnp.zeros_like(eid).at[order].set(dest_sorted)
    n_rows = ((n * TOP_K + MOE_BLOCK - 1) // MOE_BLOCK + N_EXPERTS) * MOE_BLOCK
    buf = jnp.zeros((n_rows, d), x.dtype).at[dest].set(xt[tok])
    block_start = jnp.arange(n_rows // MOE_BLOCK) * MOE_BLOCK
    block_e = jnp.minimum(jnp.searchsorted(pends, block_start, side='right'), N_EXPERTS - 1)

    def expert_block(args):
        xb, e = args
        return swiglu(xb, w1[e], w3[e], w2[e])

    yb = lax.map(expert_block, (buf.reshape(-1, MOE_BLOCK, d), block_e)).reshape(n_rows, d)
    y = jnp.sum(yb[dest].reshape(n, TOP_K, d) * topv[..., None].astype(x.dtype), axis=1)
    return y.reshape(bs, t, d)


def setup_inputs(seed: int = 0) -> dict:
    key = jax.random.key(seed)
    ks = jax.random.split(key, 32)
    f32 = jnp.float32

    def nrm(k, shape, scale):
        return jax.random.normal(k, shape, f32) * scale

    d = D_MODEL
    dt = jnp.exp(jax.random.uniform(ks[15], (N_B_LAYERS, GD_HV), f32, np.log(1e-3), np.log(1e-1)))
    return {
        'x_prompt': nrm(ks[0], (BATCH, SEQ, d), 1.0),
        'x_sample': nrm(ks[1], (DEC_BATCH, DEC_SEQ, d), 1.0),
        'state_conv_a': nrm(ks[2], (N_A_LAYERS, DEC_BATCH, SC_WIDTH - 1, d), 1.0),
        'state_conv_b': nrm(ks[3], (N_B_LAYERS, DEC_BATCH, GD_CONV - 1, GD_CONV_CH), 1.0),
        'state_delta': nrm(ks[4], (N_B_LAYERS, DEC_BATCH, GD_HV, GD_DH, GD_DH), 0.1),
        'meta_tokens': nrm(ks[5], (N_META, d), 1.0),
        'ln_g': 1.0 + nrm(ks[6], (DEPTH, 2, d), 0.02),
        'ln_b': nrm(ks[7], (DEPTH, 2, d), 0.02),
        'sc_w_in': nrm(ks[8], (N_A_LAYERS, d, 3 * d), d ** -0.5),
        'sc_conv': nrm(ks[9], (N_A_LAYERS, SC_WIDTH, d), SC_WIDTH ** -0.5),
        'sc_w_out': nrm(ks[10], (N_A_LAYERS, d, d), d ** -0.5 * DEEPNORM_BETA),
        'ffn_w1': nrm(ks[11], (N_A_LAYERS, d, D_FF), d ** -0.5),
        'ffn_w3': nrm(ks[12], (N_A_LAYERS, d, D_FF), d ** -0.5),
        'ffn_w2': nrm(ks[13], (N_A_LAYERS, D_FF, d), D_FF ** -0.5 * DEEPNORM_BETA),
        'gd_w_in': nrm(ks[14], (N_B_LAYERS, d, GD_PROJ), d ** -0.5),
        'gd_conv': nrm(ks[16], (N_B_LAYERS, GD_CONV, GD_CONV_CH), GD_CONV ** -0.5),
        'gd_a_log': jnp.log(jax.random.uniform(ks[17], (N_B_LAYERS, GD_HV), f32, 1.0, 16.0)),
        'gd_dt_bias': dt + jnp.log(-jnp.expm1(-dt)),
        'gd_norm_g': 1.0 + nrm(ks[18], (N_B_LAYERS, GD_DH), 0.02),
        'gd_w_out': nrm(ks[19], (N_B_LAYERS, GD_V_DIM, d), GD_V_DIM ** -0.5 * DEEPNORM_BETA),
        'moe_router': nrm(ks[20], (N_B_LAYERS, d, N_EXPERTS), d ** -0.5),
        'moe_w1': nrm(ks[21], (N_B_LAYERS, N_EXPERTS, d, D_FF), d ** -0.5),
        'moe_w3': nrm(ks[22], (N_B_LAYERS, N_EXPERTS, d, D_FF), d ** -0.5),
        'moe_w2': nrm(ks[23], (N_B_LAYERS, N_EXPERTS, D_FF, d), D_FF ** -0.5 * DEEPNORM_BETA),
    }


def reference(x_prompt, x_sample, state_conv_a, state_conv_b, state_delta, meta_tokens, ln_g, ln_b,
              sc_w_in, sc_conv, sc_w_out, ffn_w1, ffn_w3, ffn_w2,
              gd_w_in, gd_conv, gd_a_log, gd_dt_bias, gd_norm_g, gd_w_out,
              moe_router, moe_w1, moe_w3, moe_w2):
    bp = x_prompt.shape[0]
    meta = jnp.broadcast_to(meta_tokens[None].astype(x_prompt.dtype), (bp, N_META, D_MODEL))
    xp = jnp.concatenate([meta, x_prompt], axis=1)
    xs = x_sample
    conv_a_p, conv_a_s, conv_b_p, conv_b_s, delta_p, delta_s = [], [], [], [], [], []
    for i in range(DEPTH):
        j = i // 2
        if i % 2 == 0:
            zero_buf = jnp.zeros((bp, SC_WIDTH - 1, D_MODEL), xp.dtype)
            mp, cp = short_conv_mixer(xp, zero_buf, sc_w_in[j], sc_conv[j], sc_w_out[j])
            ms, cs = short_conv_mixer(xs, state_conv_a[j], sc_w_in[j], sc_conv[j], sc_w_out[j])
            conv_a_p.append(cp)
            conv_a_s.append(cs)
            xp = layer_norm(DEEPNORM_ALPHA * xp + mp, ln_g[i, 0], ln_b[i, 0])
            xs = layer_norm(DEEPNORM_ALPHA * xs + ms, ln_g[i, 0], ln_b[i, 0])
            xp = layer_norm(DEEPNORM_ALPHA * xp + swiglu(xp, ffn_w1[j], ffn_w3[j], ffn_w2[j]), ln_g[i, 1], ln_b[i, 1])
            xs = layer_norm(DEEPNORM_ALPHA * xs + swiglu(xs, ffn_w1[j], ffn_w3[j], ffn_w2[j]), ln_g[i, 1], ln_b[i, 1])
        else:
            zero_buf = jnp.zeros((bp, GD_CONV - 1, GD_CONV_CH), xp.dtype)
            zero_s = jnp.zeros((bp, GD_HV, GD_DH, GD_DH), jnp.float32)
            mp, cp, sp = gated_deltanet_mixer(xp, zero_buf, zero_s, gd_w_in[j], gd_conv[j], gd_a_log[j],
                                              gd_dt_bias[j], gd_norm_g[j], gd_w_out[j], True)
            ms, cs, ss = gated_deltanet_mixer(xs, state_conv_b[j], state_delta[j], gd_w_in[j], gd_conv[j], gd_a_log[j],
                                              gd_dt_bias[j], gd_norm_g[j], gd_w_out[j], False)
            conv_b_p.append(cp)
            conv_b_s.append(cs)
            delta_p.append(sp)
            delta_s.append(ss)
            xp = layer_norm(DEEPNORM_ALPHA * xp + mp, ln_g[i, 0], ln_b[i, 0])
            xs = layer_norm(DEEPNORM_ALPHA * xs + ms, ln_g[i, 0], ln_b[i, 0])
            xp = layer_norm(DEEPNORM_ALPHA * xp + moe_swiglu(xp, moe_router[j], moe_w1[j], moe_w3[j], moe_w2[j]),
                            ln_g[i, 1], ln_b[i, 1])
            xs = layer_norm(DEEPNORM_ALPHA * xs + moe_swiglu(xs, moe_router[j], moe_w1[j], moe_w3[j], moe_w2[j]),
                            ln_g[i, 1], ln_b[i, 1])
    y_prompt = xp[:, N_META:]
    y_sample = xs
    new_conv_a_prompt = jnp.stack(conv_a_p)
    new_conv_b_prompt = jnp.stack(conv_b_p)
    new_delta_prompt = jnp.stack(delta_p)
    new_conv_a_sample = jnp.stack(conv_a_s)
    new_conv_b_sample = jnp.stack(conv_b_s)
    new_delta_sample = jnp.stack(delta_s)
    return (y_prompt, y_sample, new_conv_a_prompt, new_conv_b_prompt, new_delta_prompt,
            new_conv_a_sample, new_conv_b_sample, new_delta_sample)
```

```python
import functools

import jax
import jax.numpy as jnp
from jax import lax
from jax.experimental import pallas as pl
from jax.experimental.pallas import tpu as pltpu

F32 = jnp.float32
BF16 = jnp.bfloat16

D_MODEL = 1024
N_META = 16
SC_WIDTH = 3
GD_HK = 8
GD_HV = 16
GD_DH = 128
GD_K_DIM = GD_HK * GD_DH
GD_V_DIM = GD_HV * GD_DH
GD_CONV = 4
GD_CONV_CH = 2 * GD_K_DIM + GD_V_DIM
D_FF = 3584
N_EXPERTS = 8
TOP_K = 2
LN_EPS = 1e-5
RMS_EPS = 1e-6
DEPTH = 2
DEEPNORM_ALPHA = (2 * DEPTH) ** 0.25

CHUNK = 64
HALO = 8
V7X_VMEM_LIMIT = 56 * 1024 * 1024


def _cparams(sem, vmem=V7X_VMEM_LIMIT):
    return pltpu.CompilerParams(dimension_semantics=sem, vmem_limit_bytes=vmem)


def _bdot(a, b):
    return jnp.dot(a.astype(BF16), b.astype(BF16), preferred_element_type=F32)


def _bdot_nt(a, b):
    return lax.dot_general(a.astype(BF16), b.astype(BF16), (((1,), (1,)), ((), ())),
                           preferred_element_type=F32)


def _bdot_tn(a, b):
    return lax.dot_general(a.astype(BF16), b.astype(BF16), (((0,), (0,)), ((), ())),
                           preferred_element_type=F32)


def _layer_norm(r, g, b):
    mu = jnp.mean(r, axis=-1, keepdims=True)
    d = r - mu
    var = jnp.mean(d * d, axis=-1, keepdims=True)
    return d * lax.rsqrt(var + LN_EPS) * g + b


def _silu(x):
    return x * jax.nn.sigmoid(x)


def _l0_inproj(xb, w_in_ref, ch_ref, bg_ref, row0, rows):
    col_chunk = 512
    for j in range(D_MODEL // col_chunk):
        lo, hi = j * col_chunk, (j + 1) * col_chunk
        bg = jnp.dot(xb, w_in_ref[:, lo:hi], preferred_element_type=F32)
        c = jnp.dot(xb, w_in_ref[:, D_MODEL + lo:D_MODEL + hi], preferred_element_type=F32)
        h = jnp.dot(xb, w_in_ref[:, 2 * D_MODEL + lo:2 * D_MODEL + hi], preferred_element_type=F32)
        ch_ref[row0:row0 + rows, lo:hi] = c * h
        bg_ref[:, lo:hi] = bg


def _l0_prompt_kernel(x_ref, w_in_ref, w_out_ref, cw_ref, g_ref, b_ref, carry_ref,
                      o_ref, tail_ref, buf_ref, bg_ref):
    tm = x_ref.shape[0]
    t = pl.program_id(1)

    @pl.when(t == 0)
    def _():
        buf_ref[0:HALO, :] = carry_ref[...]

    x = x_ref[...]
    _l0_inproj(x.astype(BF16), w_in_ref, buf_ref, bg_ref, HALO, tm)
    y = (cw_ref[0:1, :] * buf_ref[HALO - 2:HALO - 2 + tm, :]
         + cw_ref[1:2, :] * buf_ref[HALO - 1:HALO - 1 + tm, :]
         + cw_ref[2:3, :] * buf_ref[HALO:HALO + tm, :])
    u = (bg_ref[...] * y).astype(BF16)
    m = jnp.dot(u, w_out_ref[...], preferred_element_type=F32)
    o_ref[...] = _layer_norm(DEEPNORM_ALPHA * x + m, g_ref[...], b_ref[...])
    tail = buf_ref[tm:tm + HALO, :]
    tail_ref[...] = tail
    buf_ref[0:HALO, :] = tail


def _l0_small_kernel(x_ref, st0_ref, st1_ref, w_in_ref, w_out_ref, cw_ref, g_ref, b_ref,
                     o_ref, ch_out_ref, buf_ref, bg_ref):
    n = x_ref.shape[0]
    x = x_ref[...]
    buf_ref[0:HALO, :] = jnp.zeros((HALO, D_MODEL), F32)
    _l0_inproj(x.astype(BF16), w_in_ref, buf_ref, bg_ref, HALO, n)
    y_meta = (cw_ref[0:1, :] * buf_ref[HALO - 2:HALO - 2 + N_META, :]
              + cw_ref[1:2, :] * buf_ref[HALO - 1:HALO - 1 + N_META, :]
              + cw_ref[2:3, :] * buf_ref[HALO:HALO + N_META, :])
    ch = buf_ref[HALO:HALO + n, :]
    y_s = (cw_ref[0:1, :] * st0_ref[...] + cw_ref[1:2, :] * st1_ref[...]
           + cw_ref[2:3, :] * ch[N_META:, :])
    y = jnp.concatenate([y_meta, y_s], axis=0)
    u = (bg_ref[...] * y).astype(BF16)
    m = jnp.dot(u, w_out_ref[...], preferred_element_type=F32)
    o_ref[...] = _layer_norm(DEEPNORM_ALPHA * x + m, g_ref[...], b_ref[...])
    ch_out_ref[...] = ch


def _full(shape):
    nd = len(shape)
    return pl.BlockSpec(shape, lambda *_: (0,) * nd)


def l0_mix_prompt(x, w_in, w_out, cw, g, b, carry, tm=512):
    bsz, seq, d = x.shape
    return pl.pallas_call(
        _l0_prompt_kernel,
        grid=(bsz, seq // tm),
        in_specs=[pl.BlockSpec((None, tm, d), lambda i, t: (i, t, 0)),
                  _full(w_in.shape), _full(w_out.shape), _full(cw.shape), _full(g.shape), _full(b.shape),
                  _full(carry.shape)],
        out_specs=[pl.BlockSpec((None, tm, d), lambda i, t: (i, t, 0)),
                   pl.BlockSpec((None, HALO, d), lambda i, t: (i, 0, 0))],
        out_shape=[jax.ShapeDtypeStruct((bsz, seq, d), F32),
                   jax.ShapeDtypeStruct((bsz, HALO, d), F32)],
        scratch_shapes=[pltpu.VMEM((tm + HALO, d), F32), pltpu.VMEM((tm, d), F32)],
        compiler_params=_cparams(("arbitrary", "arbitrary")),
        name="l0_mix_prompt",
    )(x, w_in, w_out, cw, g, b, carry)


def l0_mix_small(x, st0, st1, w_in, w_out, cw, g, b):
    n, d = x.shape
    return pl.pallas_call(
        _l0_small_kernel,
        grid=(1,),
        in_specs=[_full(a.shape) for a in (x, st0, st1, w_in, w_out, cw, g, b)],
        out_specs=[_full((n, d)), _full((n, d))],
        out_shape=[jax.ShapeDtypeStruct((n, d), F32), jax.ShapeDtypeStruct((n, d), F32)],
        scratch_shapes=[pltpu.VMEM((n + HALO, d), F32), pltpu.VMEM((n, d), F32)],
        compiler_params=_cparams(("arbitrary",)),
        name="l0_mix_small",
    )(x, st0, st1, w_in, w_out, cw, g, b)


def _swiglu_kernel(be_ref, nu_ref, x_ref, w1_ref, w3_ref, w2_ref, *rest, epilogue):
    o_ref = rest[-1]
    i = pl.program_id(0)
    f = pl.program_id(1)
    nf = pl.num_programs(1)

    @pl.when(i < nu_ref[0])
    def _():
        xb = x_ref[...].astype(BF16)
        a = jnp.dot(xb, w1_ref[...], preferred_element_type=F32)
        b = jnp.dot(xb, w3_ref[...], preferred_element_type=F32)
        h = (_silu(a) * b).astype(BF16)
        part = jnp.dot(h, w2_ref[...], preferred_element_type=F32)

        @pl.when(f == 0)
        def _():
            o_ref[...] = part

        @pl.when(f > 0)
        def _():
            o_ref[...] += part

        @pl.when(f == nf - 1)
        def _():
            if epilogue == "ln":
                g_ref, b_ref = rest[0], rest[1]
                o_ref[...] = _layer_norm(DEEPNORM_ALPHA * x_ref[...] + o_ref[...], g_ref[...], b_ref[...])
            else:
                o_ref[...] = o_ref[...] * rest[0][...]

    @pl.when(i >= nu_ref[0])
    def _():
        o_ref[...] = jnp.zeros(o_ref.shape, o_ref.dtype)


def swiglu_grouped(block_e, n_used, x, w1, w3, w2, extras, epilogue, tm, tf=1792):
    n_rows, d = x.shape
    nb = n_rows // tm
    nf = D_FF // tf

    def f_idx(i, f, nu):
        return jnp.where(i < nu[0], f, nf - 1)

    extra_specs = []
    for e in extras:
        if e.shape[0] == n_rows:
            extra_specs.append(pl.BlockSpec((tm, e.shape[1]), lambda i, f, be, nu: (i, 0)))
        else:
            extra_specs.append(pl.BlockSpec(e.shape, lambda i, f, be, nu: (0, 0)))
    grid_spec = pltpu.PrefetchScalarGridSpec(
        num_scalar_prefetch=2,
        grid=(nb, nf),
        in_specs=[pl.BlockSpec((tm, d), lambda i, f, be, nu: (i, 0)),
                  pl.BlockSpec((None, d, tf), lambda i, f, be, nu: (be[i], 0, f_idx(i, f, nu))),
                  pl.BlockSpec((None, d, tf), lambda i, f, be, nu: (be[i], 0, f_idx(i, f, nu))),
                  pl.BlockSpec((None, tf, d), lambda i, f, be, nu: (be[i], f_idx(i, f, nu), 0)),
                  ] + extra_specs,
        out_specs=pl.BlockSpec((tm, d), lambda i, f, be, nu: (i, 0)),
    )
    return pl.pallas_call(
        functools.partial(_swiglu_kernel, epilogue=epilogue),
        grid_spec=grid_spec,
        out_shape=jax.ShapeDtypeStruct((n_rows, d), F32),
        compiler_params=_cparams(("arbitrary", "arbitrary")),
        name="swiglu_" + epilogue,
    )(block_e, n_used, x, w1, w3, w2, *extras)


def _mm_kernel(x_ref, w_ref, o_ref):
    o_ref[...] = jnp.dot(x_ref[...].astype(BF16), w_ref[...], preferred_element_type=F32).astype(o_ref.dtype)


def matmul(x, w, out_dtype, tm, tn):
    n, k = x.shape
    _, m = w.shape
    return pl.pallas_call(
        _mm_kernel,
        grid=(n // tm, m // tn),
        in_specs=[pl.BlockSpec((tm, k), lambda i, j: (i, 0)),
                  pl.BlockSpec((k, tn), lambda i, j: (0, j))],
        out_specs=pl.BlockSpec((tm, tn), lambda i, j: (i, j)),
        out_shape=jax.ShapeDtypeStruct((n, m), out_dtype),
        compiler_params=_cparams(("arbitrary", "arbitrary")),
        name="matmul",
    )(x, w)


def _mm_ln_kernel(a_ref, w_ref, x_ref, g_ref, b_ref, o_ref):
    m = jnp.dot(a_ref[...], w_ref[...], preferred_element_type=F32)
    o_ref[...] = _layer_norm(DEEPNORM_ALPHA * x_ref[...] + m, g_ref[...], b_ref[...])


def matmul_ln(a, w, x, g, b, tm):
    n, k = a.shape
    d = w.shape[1]
    return pl.pallas_call(
        _mm_ln_kernel,
        grid=(n // tm,),
        in_specs=[pl.BlockSpec((tm, k), lambda i: (i, 0)), _full(w.shape),
                  pl.BlockSpec((tm, d), lambda i: (i, 0)), _full(g.shape), _full(b.shape)],
        out_specs=pl.BlockSpec((tm, d), lambda i: (i, 0)),
        out_shape=jax.ShapeDtypeStruct((n, d), F32),
        compiler_params=_cparams(("arbitrary",)),
        name="matmul_ln",
    )(a, w, x, g, b)


def _l2norm_heads(x, n_heads, scale):
    outs = []
    for h in range(n_heads):
        xh = x[:, h * GD_DH:(h + 1) * GD_DH]
        ss = jnp.sum(xh * xh, axis=-1, keepdims=True)
        outs.append(xh * (lax.rsqrt(ss + RMS_EPS) * scale))
    return outs


def _softplus(x):
    return jnp.maximum(x, 0.0) + jnp.log1p(jnp.exp(-jnp.abs(x)))


def _gated_rmsnorm(o, z, norm_g):
    ms = jnp.mean(o * o, axis=-1, keepdims=True)
    return o * lax.rsqrt(ms + RMS_EPS) * norm_g * _silu(z)


def _tri_inverse(a_strict, row, col):
    eye = jnp.where(row == col, 1.0, 0.0).astype(F32)
    s = 1
    inv = eye
    while s < CHUNK:
        pair = (row // (2 * s)) == (col // (2 * s))
        sub = pair & ((row // s) % 2 == 1) & ((col // s) % 2 == 0)
        e = jnp.where(sub, a_strict, 0.0)
        if s == 1:
            inv = eye - e
        else:
            inv = inv - _bdot(inv, _bdot(e, inv))
        s *= 2
    return inv


def _gdn_chunk_kernel(qkv_ref, z_ref, ba_ref, bat_ref, cw_ref, alog_ref, dtb_ref, alogt_ref, dtbt_ref,
                      ng_ref, carry_ref, s0_ref, o_ref, tail_ref, sout_ref,
                      buf_ref, s_ref, q_ref, k_ref, v_ref, *, valid_rows):
    tc = qkv_ref.shape[0]
    t = pl.program_id(1)

    @pl.when(t == 0)
    def _():
        buf_ref[0:HALO, :] = carry_ref[...]
        s_ref[...] = s0_ref[...]

    buf_ref[HALO:HALO + tc, :] = qkv_ref[...]
    col_chunk = 512
    for j in range(GD_CONV_CH // col_chunk):
        lo, hi = j * col_chunk, (j + 1) * col_chunk
        y = cw_ref[GD_CONV - 1:GD_CONV, lo:hi] * buf_ref[HALO:HALO + tc, lo:hi]
        for i in range(GD_CONV - 1):
            off = HALO - (GD_CONV - 1) + i
            y = y + cw_ref[i:i + 1, lo:hi] * buf_ref[off:off + tc, lo:hi]
        act = _silu(y)
        heads_per = col_chunk // GD_DH
        if lo < GD_K_DIM:
            parts = _l2norm_heads(act, heads_per, GD_DH ** -0.5)
            for p, val in enumerate(parts):
                q_ref[:, lo + p * GD_DH:lo + (p + 1) * GD_DH] = val
        elif lo < 2 * GD_K_DIM:
            parts = _l2norm_heads(act, heads_per, 1.0)
            for p, val in enumerate(parts):
                c0 = lo - GD_K_DIM + p * GD_DH
                k_ref[:, c0:c0 + GD_DH] = val
        else:
            v_ref[:, lo - 2 * GD_K_DIM:hi - 2 * GD_K_DIM] = act
    tail = buf_ref[tc:tc + HALO, :]
    tail_ref[...] = tail
    buf_ref[0:HALO, :] = tail

    row = lax.broadcasted_iota(jnp.int32, (CHUNK, CHUNK), 0)
    col = lax.broadcasted_iota(jnp.int32, (CHUNK, CHUNK), 1)
    incl = row >= col
    strict = row > col
    tril_ones = jnp.where(incl, 1.0, 0.0).astype(F32)
    triu_ones = jnp.where(row <= col, 1.0, 0.0).astype(F32)
    hp = lax.Precision.HIGHEST

    def chunk_body(c, carry):
        r0 = pl.multiple_of(c * CHUNK, CHUNK)
        ba = ba_ref[c]
        bat = bat_ref[c]
        beta = jax.nn.sigmoid(ba[:, :GD_HV])
        g = -jnp.exp(alog_ref[...]) * _softplus(ba[:, GD_HV:] + dtb_ref[...])
        gt = -jnp.exp(alogt_ref[...]) * _softplus(bat[GD_HV:, :] + dtbt_ref[...])
        if valid_rows is not None:
            rid = lax.broadcasted_iota(jnp.int32, (CHUNK, GD_HV), 0) + r0 + t * tc
            beta = jnp.where(rid < valid_rows, beta, 0.0)
            g = jnp.where(rid < valid_rows, g, 0.0)
            cid = lax.broadcasted_iota(jnp.int32, (GD_HV, CHUNK), 1) + r0 + t * tc
            gt = jnp.where(cid < valid_rows, gt, 0.0)
        cum = jnp.dot(tril_ones, g, precision=hp, preferred_element_type=F32)
        cumt = jnp.dot(gt, triu_ones, precision=hp, preferred_element_type=F32)
        ecum = jnp.exp(cum)
        elast = jnp.exp(cum[CHUNK - 1:CHUNK, :] - cum)
        glast = jnp.exp(cum[CHUNK - 1:CHUNK, :])

        for kh in range(GD_HK):
            q = q_ref[pl.ds(r0, CHUNK), kh * GD_DH:(kh + 1) * GD_DH]
            k = k_ref[pl.ds(r0, CHUNK), kh * GD_DH:(kh + 1) * GD_DH]
            kk = _bdot_nt(k, k)
            qk = _bdot_nt(q, k)
            for h in range(kh * (GD_HV // GD_HK), (kh + 1) * (GD_HV // GD_HK)):
                v = v_ref[pl.ds(r0, CHUNK), h * GD_DH:(h + 1) * GD_DH]
                z = z_ref[pl.ds(r0, CHUNK), h * GD_DH:(h + 1) * GD_DH].astype(F32)
                bcol = beta[:, h:h + 1]
                diff = cum[:, h:h + 1] - cumt[h:h + 1, :]
                dec = jnp.where(incl, jnp.exp(jnp.where(incl, diff, 0.0)), 0.0)
                a_strict = jnp.where(strict, dec * kk, 0.0) * bcol
                tinv = _tri_inverse(a_strict, row, col)
                s = s_ref[h]
                ks = _bdot(k, s)
                w = _bdot(tinv, bcol * (v - ecum[:, h:h + 1] * ks))
                o = ecum[:, h:h + 1] * _bdot(q, s) + _bdot(qk * dec, w)
                s_ref[h] = glast[:, h:h + 1] * s + _bdot_tn(k * elast[:, h:h + 1], w)
                o_ref[pl.ds(r0, CHUNK), h * GD_DH:(h + 1) * GD_DH] = _gated_rmsnorm(o, z, ng_ref[...]).astype(o_ref.dtype)
        return carry

    lax.fori_loop(0, tc // CHUNK, chunk_body, 0)
    sout_ref[...] = s_ref[...]


def gdn_chunked(qkv, z, ba, bat, cw, a_log, dt_bias, norm_g, carry, s0, tc, valid_rows=None):
    bsz, seq, _ = qkv.shape
    nchunk = tc // CHUNK
    ba4 = ba.reshape(bsz, seq // CHUNK, CHUNK, 2 * GD_HV)
    bat4 = bat.reshape(bsz, 2 * GD_HV, seq // CHUNK, CHUNK).transpose(0, 2, 1, 3)
    alog = a_log.reshape(1, GD_HV)
    dtb = dt_bias.reshape(1, GD_HV)
    alogt = a_log.reshape(GD_HV, 1)
    dtbt = dt_bias.reshape(GD_HV, 1)
    ng = norm_g.reshape(1, GD_DH)
    return pl.pallas_call(
        functools.partial(_gdn_chunk_kernel, valid_rows=valid_rows),
        grid=(bsz, seq // tc),
        in_specs=[pl.BlockSpec((None, tc, GD_CONV_CH), lambda i, t: (i, t, 0)),
                  pl.BlockSpec((None, tc, GD_V_DIM), lambda i, t: (i, t, 0)),
                  pl.BlockSpec((None, nchunk, CHUNK, 2 * GD_HV), lambda i, t: (i, t, 0, 0)),
                  pl.BlockSpec((None, nchunk, 2 * GD_HV, CHUNK), lambda i, t: (i, t, 0, 0)),
                  _full(cw.shape), _full(alog.shape), _full(dtb.shape), _full(alogt.shape), _full(dtbt.shape),
                  _full(ng.shape), _full(carry.shape), _full(s0.shape)],
        out_specs=[pl.BlockSpec((None, tc, GD_V_DIM), lambda i, t: (i, t, 0)),
                   pl.BlockSpec((None, HALO, GD_CONV_CH), lambda i, t: (i, 0, 0)),
                   pl.BlockSpec((None, GD_HV, GD_DH, GD_DH), lambda i, t: (i, 0, 0, 0))],
        out_shape=[jax.ShapeDtypeStruct((bsz, seq, GD_V_DIM), BF16),
                   jax.ShapeDtypeStruct((bsz, HALO, GD_CONV_CH), F32),
                   jax.ShapeDtypeStruct((bsz, GD_HV, GD_DH, GD_DH), F32)],
        scratch_shapes=[pltpu.VMEM((tc + HALO, GD_CONV_CH), F32),
                        pltpu.VMEM((GD_HV, GD_DH, GD_DH), F32),
                        pltpu.VMEM((tc, GD_K_DIM), F32), pltpu.VMEM((tc, GD_K_DIM), F32),
                        pltpu.VMEM((tc, GD_V_DIM), F32)],
        compiler_params=_cparams(("arbitrary", "arbitrary")),
        name="gdn_chunked",
    )(qkv, z, ba4, bat4, cw, alog, dtb, alogt, dtbt, ng, carry, s0)


def _gdn_step_kernel(cur_ref, st_ref, z_ref, ba_ref, cw_ref, alog_ref, dtb_ref, ng_ref, s_ref,
                     o_ref, sout_ref, oacc_ref):
    ns = cur_ref.shape[0]
    y = cw_ref[GD_CONV - 1:GD_CONV, :] * cur_ref[...]
    for i in range(GD_CONV - 1):
        y = y + cw_ref[i:i + 1, :] * st_ref[:, i, :]
    act = _silu(y)
    qh = _l2norm_heads(act[:, :GD_K_DIM], GD_HK, GD_DH ** -0.5)
    kh = _l2norm_heads(act[:, GD_K_DIM:2 * GD_K_DIM], GD_HK, 1.0)
    qk_t = jnp.concatenate(qh + kh, axis=0).T
    ba = ba_ref[...]
    beta = jax.nn.sigmoid(ba[:, :GD_HV])
    eg = jnp.exp(-jnp.exp(alog_ref[...]) * _softplus(ba[:, GD_HV:] + dtb_ref[...]))
    for s in range(ns):
        for h in range(GD_HV):
            g = h // (GD_HV // GD_HK)
            qcol = qk_t[:, g * ns + s:g * ns + s + 1]
            kcol = qk_t[:, (GD_HK + g) * ns + s:(GD_HK + g) * ns + s + 1]
            sd = s_ref[s, h] * eg[s:s + 1, h:h + 1]
            ks = jnp.sum(sd * kcol, axis=0, keepdims=True)
            v = act[s:s + 1, 2 * GD_K_DIM + h * GD_DH:2 * GD_K_DIM + (h + 1) * GD_DH]
            w = beta[s:s + 1, h:h + 1] * (v - ks)
            sn = sd + kcol * w
            sout_ref[s, h] = sn
            oacc_ref[s:s + 1, h * GD_DH:(h + 1) * GD_DH] = jnp.sum(sn * qcol, axis=0, keepdims=True)
    for h in range(GD_HV):
        lo, hi = h * GD_DH, (h + 1) * GD_DH
        o_ref[:, lo:hi] = _gated_rmsnorm(oacc_ref[:, lo:hi], z_ref[:, lo:hi].astype(F32),
                                         ng_ref[...]).astype(o_ref.dtype)


def gdn_step(cur, st, z, ba, cw, a_log, dt_bias, norm_g, s0, ns=8):
    n = cur.shape[0]
    alog = a_log.reshape(1, GD_HV)
    dtb = dt_bias.reshape(1, GD_HV)
    ng = norm_g.reshape(1, GD_DH)
    return pl.pallas_call(
        _gdn_step_kernel,
        grid=(n // ns,),
        in_specs=[pl.BlockSpec((ns, GD_CONV_CH), lambda i: (i, 0)),
                  pl.BlockSpec((ns, GD_CONV - 1, GD_CONV_CH), lambda i: (i, 0, 0)),
                  pl.BlockSpec((ns, GD_V_DIM), lambda i: (i, 0)),
                  pl.BlockSpec((ns, 2 * GD_HV), lambda i: (i, 0)),
                  _full(cw.shape), _full(alog.shape), _full(dtb.shape), _full(ng.shape),
                  pl.BlockSpec((ns, GD_HV, GD_DH, GD_DH), lambda i: (i, 0, 0, 0))],
        out_specs=[pl.BlockSpec((ns, GD_V_DIM), lambda i: (i, 0)),
                   pl.BlockSpec((ns, GD_HV, GD_DH, GD_DH), lambda i: (i, 0, 0, 0))],
        out_shape=[jax.ShapeDtypeStruct((n, GD_V_DIM), BF16),
                   jax.ShapeDtypeStruct((n, GD_HV, GD_DH, GD_DH), F32)],
        scratch_shapes=[pltpu.VMEM((ns, GD_V_DIM), F32)],
        compiler_params=_cparams(("arbitrary",)),
        name="gdn_step",
    )(cur, st, z, ba, cw, alog, dtb, ng, s0)


def _router_kernel(x_ref, wr_ref, idx_ref, val_ref):
    logits = jnp.dot(x_ref[...], wr_ref[...], precision=lax.Precision.HIGHEST, preferred_element_type=F32)
    mx = jnp.max(logits, axis=-1, keepdims=True)
    ex = jnp.exp(logits - mx)
    p = ex / jnp.sum(ex, axis=-1, keepdims=True)
    lane = lax.broadcasted_iota(jnp.int32, p.shape, 1)
    p1 = jnp.max(p, axis=-1, keepdims=True)
    i1 = jnp.min(jnp.where(p == p1, lane, N_EXPERTS), axis=-1, keepdims=True)
    rest = jnp.where(lane == i1, -1.0, p)
    p2 = jnp.max(rest, axis=-1, keepdims=True)
    i2 = jnp.min(jnp.where(rest == p2, lane, N_EXPERTS), axis=-1, keepdims=True)
    tot = p1 + p2
    idx_ref[...] = jnp.concatenate([i1, i2], axis=-1)
    val_ref[...] = jnp.concatenate([p1 / tot, p2 / tot], axis=-1)


def router(x, wr, tm):
    n, d = x.shape
    return pl.pallas_call(
        _router_kernel,
        grid=(n // tm,),
        in_specs=[pl.BlockSpec((tm, d), lambda i: (i, 0)), _full(wr.shape)],
        out_specs=[pl.BlockSpec((tm, TOP_K), lambda i: (i, 0)), pl.BlockSpec((tm, TOP_K), lambda i: (i, 0))],
        out_shape=[jax.ShapeDtypeStruct((n, TOP_K), jnp.int32), jax.ShapeDtypeStruct((n, TOP_K), F32)],
        compiler_params=_cparams(("arbitrary",)),
        name="router",
    )(x, wr)


def _row_copy(src_hbm, row, dst, slot, sem):
    return pltpu.make_async_copy(src_hbm.at[pl.ds(row, 1)], dst.at[pl.ds(slot, 1)], sem)


def _dispatch_kernel(src_ref, nu_ref, x_hbm, o_ref, buf_ref, sem):
    tm = o_ref.shape[0]
    i = pl.program_id(0)

    @pl.when(i < nu_ref[0])
    def _():
        base = i * tm

        def issue(r, c):
            _row_copy(x_hbm, src_ref[base + r], buf_ref, r, sem).start()
            return c

        lax.fori_loop(0, tm, issue, 0)

        def drain(r, c):
            _row_copy(x_hbm, 0, buf_ref, r, sem).wait()
            return c

        lax.fori_loop(0, tm, drain, 0)
        o_ref[...] = buf_ref[...].astype(o_ref.dtype)

    @pl.when(i >= nu_ref[0])
    def _():
        o_ref[...] = jnp.zeros(o_ref.shape, o_ref.dtype)


def dispatch(src, n_used, x, n_rows, tm):
    d = x.shape[1]
    grid_spec = pltpu.PrefetchScalarGridSpec(
        num_scalar_prefetch=2,
        grid=(n_rows // tm,),
        in_specs=[pl.BlockSpec(memory_space=pl.ANY)],
        out_specs=pl.BlockSpec((tm, d), lambda i, s, nu: (i, 0)),
        scratch_shapes=[pltpu.VMEM((tm, d), F32), pltpu.SemaphoreType.DMA(())],
    )
    return pl.pallas_call(
        _dispatch_kernel,
        grid_spec=grid_spec,
        out_shape=jax.ShapeDtypeStruct((n_rows, d), BF16),
        compiler_params=_cparams(("arbitrary",)),
        name="moe_dispatch",
    )(src, n_used, x)


def _combine_kernel(dest_ref, x_ref, g_ref, b_ref, y_hbm, o_ref, buf_ref, sem):
    tt = o_ref.shape[0]
    base = pl.program_id(0) * tt * TOP_K

    def issue(r, c):
        for k in range(TOP_K):
            _row_copy(y_hbm, dest_ref[base + r * TOP_K + k], buf_ref.at[k], r, sem).start()
        return c

    lax.fori_loop(0, tt, issue, 0)

    def drain(r, c):
        for k in range(TOP_K):
            _row_copy(y_hbm, 0, buf_ref.at[k], r, sem).wait()
        return c

    lax.fori_loop(0, tt, drain, 0)
    y = buf_ref[0] + buf_ref[1]
    o_ref[...] = _layer_norm(DEEPNORM_ALPHA * x_ref[...] + y, g_ref[...], b_ref[...])


def combine(dest, x, g, b, yb, tt):
    n, d = x.shape
    grid_spec = pltpu.PrefetchScalarGridSpec(
        num_scalar_prefetch=1,
        grid=(n // tt,),
        in_specs=[pl.BlockSpec((tt, d), lambda i, dst: (i, 0)),
                  pl.BlockSpec(g.shape, lambda i, dst: (0, 0)),
                  pl.BlockSpec(b.shape, lambda i, dst: (0, 0)),
                  pl.BlockSpec(memory_space=pl.ANY)],
        out_specs=pl.BlockSpec((tt, d), lambda i, dst: (i, 0)),
        scratch_shapes=[pltpu.VMEM((TOP_K, tt, d), F32), pltpu.SemaphoreType.DMA(())],
    )
    return pl.pallas_call(
        _combine_kernel,
        grid_spec=grid_spec,
        out_shape=jax.ShapeDtypeStruct((n, d), F32),
        compiler_params=_cparams(("arbitrary",)),
        name="moe_combine",
    )(dest, x, g, b, yb)


def moe_layer(x, wr, w1, w3, w2, g, b, tm, tt):
    n = x.shape[0]
    topi, topv = router(x, wr, tt)
    eid = topi.reshape(-1)
    onehot = (eid[:, None] == jnp.arange(N_EXPERTS, dtype=jnp.int32)[None, :]).astype(jnp.int32)
    csum = jnp.cumsum(onehot, axis=0)
    rank = jnp.sum((csum - onehot) * onehot, axis=1)
    counts = csum[-1]
    padded = (counts + tm - 1) // tm * tm
    pends = jnp.cumsum(padded)
    pstarts = pends - padded
    dest = (pstarts[eid] + rank).astype(jnp.int32)
    nb = (n * TOP_K + tm - 1) // tm + N_EXPERTS
    n_rows = nb * tm
    tok = jnp.arange(n * TOP_K, dtype=jnp.int32) // TOP_K
    src = jnp.zeros((n_rows,), jnp.int32).at[dest].set(tok, unique_indices=True)
    scale = jnp.zeros((n_rows,), F32).at[dest].set(topv.reshape(-1), unique_indices=True).reshape(n_rows, 1)
    block_start = jnp.arange(nb, dtype=jnp.int32) * tm
    block_e = jnp.minimum(jnp.searchsorted(pends, block_start, side="right"), N_EXPERTS - 1).astype(jnp.int32)
    n_used = (pends[-1] // tm).astype(jnp.int32).reshape(1)

    xs = dispatch(src, n_used, x, n_rows, tm)
    yb = swiglu_grouped(block_e, n_used, xs, w1, w3, w2, [scale], "scale", tm)
    return combine(dest, x, g, b, yb, tt)


def kernel(x_prompt, x_sample, state_conv_a, state_conv_b, state_delta, meta_tokens, ln_g, ln_b, sc_w_in, sc_conv, sc_w_out, ffn_w1, ffn_w3, ffn_w2, gd_w_in, gd_conv, gd_a_log, gd_dt_bias, gd_norm_g, gd_w_out, moe_router, moe_w1, moe_w3, moe_w2):
    bsz, seq, d = x_prompt.shape
    n_s = x_sample.shape[0]
    n_small = N_META + n_s
    n_p = bsz * seq

    def row(v):
        return v.reshape(1, -1)

    sc_w_in_b = sc_w_in[0].astype(BF16)
    sc_w_out_b = sc_w_out[0].astype(BF16)
    ffn_w1_b, ffn_w3_b, ffn_w2_b = ffn_w1.astype(BF16), ffn_w3.astype(BF16), ffn_w2.astype(BF16)
    gd_w_qkv_b = gd_w_in[0][:, :GD_CONV_CH].astype(BF16)
    gd_w_z_b = gd_w_in[0][:, GD_CONV_CH:GD_CONV_CH + GD_V_DIM].astype(BF16)
    gd_w_ba_b = gd_w_in[0][:, GD_CONV_CH + GD_V_DIM:].astype(BF16)
    gd_w_out_b = gd_w_out[0].astype(BF16)
    moe_w1_b, moe_w3_b, moe_w2_b = moe_w1[0].astype(BF16), moe_w3[0].astype(BF16), moe_w2[0].astype(BF16)

    x_small = jnp.concatenate([meta_tokens.astype(F32), x_sample.reshape(n_s, d)], axis=0)

    xa_s, ch_s = l0_mix_small(x_small, state_conv_a[0, :, 0], state_conv_a[0, :, 1], sc_w_in_b, sc_w_out_b,
                              sc_conv[0], row(ln_g[0, 0]), row(ln_b[0, 0]))
    xa_p, tail_a = l0_mix_prompt(x_prompt, sc_w_in_b, sc_w_out_b, sc_conv[0], row(ln_g[0, 0]), row(ln_b[0, 0]),
                                 ch_s[N_META - HALO:N_META])
    one = jnp.ones((1,), jnp.int32)

    def dense_ffn(x, tm):
        nb = x.shape[0] // tm
        return swiglu_grouped(jnp.zeros((nb,), jnp.int32), one * nb, x, ffn_w1_b, ffn_w3_b, ffn_w2_b,
                              [row(ln_g[0, 1]), row(ln_b[0, 1])], "ln", tm)

    xb_s = dense_ffn(xa_s, n_small)
    xb_p = dense_ffn(xa_p.reshape(n_p, d), 512)

    def gdn_inproj(x, tm):
        qkv = matmul(x, gd_w_qkv_b, F32, tm, 1024)
        z = matmul(x, gd_w_z_b, BF16, tm, 1024)
        ba = matmul(x, gd_w_ba_b, F32, tm, 2 * GD_HV)
        return qkv, z, ba

    qkv_s, z_s, ba_s = gdn_inproj(xb_s, n_small)
    qkv_p, z_p, ba_p = gdn_inproj(xb_p, 1024)

    pad = CHUNK - N_META

    def meta_pad(a):
        return jnp.pad(a[:N_META], ((0, pad), (0, 0)))[None]

    ba_m = meta_pad(ba_s)
    o_m, _, s_meta = gdn_chunked(meta_pad(qkv_s), meta_pad(z_s), ba_m, jnp.swapaxes(ba_m, 1, 2), gd_conv[0],
                                 gd_a_log[0], gd_dt_bias[0], gd_norm_g[0],
                                 jnp.zeros((HALO, GD_CONV_CH), F32), jnp.zeros((GD_HV, GD_DH, GD_DH), F32),
                                 CHUNK, valid_rows=N_META)
    o_smp, s_smp = gdn_step(qkv_s[N_META:], state_conv_b[0], z_s[N_META:], ba_s[N_META:], gd_conv[0],
                            gd_a_log[0], gd_dt_bias[0], gd_norm_g[0], state_delta[0])
    ba_p3 = ba_p.reshape(bsz, seq, 2 * GD_HV)
    o_p, tail_b, s_p = gdn_chunked(qkv_p.reshape(bsz, seq, GD_CONV_CH), z_p.reshape(bsz, seq, GD_V_DIM), ba_p3,
                                   jnp.swapaxes(ba_p3, 1, 2), gd_conv[0], gd_a_log[0], gd_dt_bias[0],
                                   gd_norm_g[0], qkv_s[N_META - HALO:N_META], s_meta[0], 256)
    o_s = jnp.concatenate([o_m[0, :N_META], o_smp], axis=0)
    xc_s = matmul_ln(o_s, gd_w_out_b, xb_s, row(ln_g[1, 0]), row(ln_b[1, 0]), n_small)
    xc_p = matmul_ln(o_p.reshape(n_p, GD_V_DIM), gd_w_out_b, xb_p, row(ln_g[1, 0]), row(ln_b[1, 0]), 512)

    xd_s = moe_layer(xc_s, moe_router[0], moe_w1_b, moe_w3_b, moe_w2_b, row(ln_g[1, 1]), row(ln_b[1, 1]),
                     128, n_small)
    xd_p = moe_layer(xc_p, moe_router[0], moe_w1_b, moe_w3_b, moe_w2_b, row(ln_g[1, 1]), row(ln_b[1, 1]),
                     512, 256)

    y_prompt = xd_p.reshape(bsz, seq, d)
    y_sample = xd_s[N_META:].reshape(n_s, 1, d)
    new_conv_a_prompt = tail_a[None, :, HALO - (SC_WIDTH - 1):]
    new_conv_b_prompt = tail_b[None, :, HALO - (GD_CONV - 1):]
    new_delta_prompt = s_p[None]
    new_conv_a_sample = jnp.stack([state_conv_a[0, :, 1], ch_s[N_META:]], axis=1)[None]
    new_conv_b_sample = jnp.concatenate([state_conv_b[0, :, 1:], qkv_s[N_META:, None]], axis=1)[None]
    new_delta_sample = s_smp[None]
    return (y_prompt, y_sample, new_conv_a_prompt, new_conv_b_prompt, new_delta_prompt,
            new_conv_a_sample, new_conv_b_sample, new_delta_sample)
```

```python
import functools

import jax
import jax.numpy as jnp
from jax import lax
from jax.experimental import pallas as pl
from jax.experimental.pallas import tpu as pltpu

F32 = jnp.float32
BF16 = jnp.bfloat16

D_MODEL = 1024
N_META = 16
SC_WIDTH = 3
GD_HK = 8
GD_HV = 16
GD_DH = 128
GD_K_DIM = GD_HK * GD_DH
GD_V_DIM = GD_HV * GD_DH
GD_CONV = 4
GD_CONV_CH = 2 * GD_K_DIM + GD_V_DIM
D_FF = 3584
N_EXPERTS = 8
TOP_K = 2
LN_EPS = 1e-5
RMS_EPS = 1e-6
DEPTH = 2
DEEPNORM_ALPHA = (2 * DEPTH) ** 0.25

LANES = 128
ROW_SLABS = D_MODEL // LANES
DMA_UNROLL = 8
TOKEN_BLOCK = 256
EXPERT_BLOCK = 512
CHUNK = 64
HALO = 8
V7X_VMEM_LIMIT = 56 * 1024 * 1024


def _cparams(sem, vmem=V7X_VMEM_LIMIT):
    return pltpu.CompilerParams(dimension_semantics=sem, vmem_limit_bytes=vmem)


def _bdot(a, b):
    return jnp.dot(a.astype(BF16), b.astype(BF16), preferred_element_type=F32)


def _bdot_nt(a, b):
    return lax.dot_general(a.astype(BF16), b.astype(BF16), (((1,), (1,)), ((), ())),
                           preferred_element_type=F32)


def _bdot_tn(a, b):
    return lax.dot_general(a.astype(BF16), b.astype(BF16), (((0,), (0,)), ((), ())),
                           preferred_element_type=F32)


def _layer_norm(r, g, b):
    mu = jnp.mean(r, axis=-1, keepdims=True)
    d = r - mu
    var = jnp.mean(d * d, axis=-1, keepdims=True)
    return d * lax.rsqrt(var + LN_EPS) * g + b


def _silu(x):
    return x * jax.nn.sigmoid(x)


def _l0_inproj(xb, w_in_ref, ch_ref, bg_ref, row0, rows):
    col_chunk = 512
    for j in range(D_MODEL // col_chunk):
        lo, hi = j * col_chunk, (j + 1) * col_chunk
        bg = jnp.dot(xb, w_in_ref[:, lo:hi], preferred_element_type=F32)
        c = jnp.dot(xb, w_in_ref[:, D_MODEL + lo:D_MODEL + hi], preferred_element_type=F32)
        h = jnp.dot(xb, w_in_ref[:, 2 * D_MODEL + lo:2 * D_MODEL + hi], preferred_element_type=F32)
        ch_ref[row0:row0 + rows, lo:hi] = c * h
        bg_ref[:, lo:hi] = bg


def _l0_prompt_kernel(x_ref, w_in_ref, w_out_ref, cw_ref, g_ref, b_ref, carry_ref,
                      o_ref, tail_ref, buf_ref, bg_ref):
    tm = x_ref.shape[0]
    t = pl.program_id(1)

    @pl.when(t == 0)
    def _():
        buf_ref[0:HALO, :] = carry_ref[...]

    x = x_ref[...]
    _l0_inproj(x.astype(BF16), w_in_ref, buf_ref, bg_ref, HALO, tm)
    y = (cw_ref[0:1, :] * buf_ref[HALO - 2:HALO - 2 + tm, :]
         + cw_ref[1:2, :] * buf_ref[HALO - 1:HALO - 1 + tm, :]
         + cw_ref[2:3, :] * buf_ref[HALO:HALO + tm, :])
    u = (bg_ref[...] * y).astype(BF16)
    m = jnp.dot(u, w_out_ref[...], preferred_element_type=F32)
    o_ref[...] = _layer_norm(DEEPNORM_ALPHA * x + m, g_ref[...], b_ref[...])
    tail = buf_ref[tm:tm + HALO, :]
    tail_ref[...] = tail
    buf_ref[0:HALO, :] = tail


def _l0_small_kernel(x_ref, st0_ref, st1_ref, w_in_ref, w_out_ref, cw_ref, g_ref, b_ref,
                     o_ref, ch_out_ref, buf_ref, bg_ref):
    n = x_ref.shape[0]
    x = x_ref[...]
    buf_ref[0:HALO, :] = jnp.zeros((HALO, D_MODEL), F32)
    _l0_inproj(x.astype(BF16), w_in_ref, buf_ref, bg_ref, HALO, n)
    y_meta = (cw_ref[0:1, :] * buf_ref[HALO - 2:HALO - 2 + N_META, :]
              + cw_ref[1:2, :] * buf_ref[HALO - 1:HALO - 1 + N_META, :]
              + cw_ref[2:3, :] * buf_ref[HALO:HALO + N_META, :])
    ch = buf_ref[HALO:HALO + n, :]
    y_s = (cw_ref[0:1, :] * st0_ref[...] + cw_ref[1:2, :] * st1_ref[...]
           + cw_ref[2:3, :] * ch[N_META:, :])
    y = jnp.concatenate([y_meta, y_s], axis=0)
    u = (bg_ref[...] * y).astype(BF16)
    m = jnp.dot(u, w_out_ref[...], preferred_element_type=F32)
    o_ref[...] = _layer_norm(DEEPNORM_ALPHA * x + m, g_ref[...], b_ref[...])
    ch_out_ref[...] = ch


def _full(shape):
    nd = len(shape)
    return pl.BlockSpec(shape, lambda *_: (0,) * nd)


def l0_mix_prompt(x, w_in, w_out, cw, g, b, carry, tm=512):
    bsz, seq, d = x.shape
    return pl.pallas_call(
        _l0_prompt_kernel,
        grid=(bsz, seq // tm),
        in_specs=[pl.BlockSpec((None, tm, d), lambda i, t: (i, t, 0)),
                  _full(w_in.shape), _full(w_out.shape), _full(cw.shape), _full(g.shape), _full(b.shape),
                  _full(carry.shape)],
        out_specs=[pl.BlockSpec((None, tm, d), lambda i, t: (i, t, 0)),
                   pl.BlockSpec((None, HALO, d), lambda i, t: (i, 0, 0))],
        out_shape=[jax.ShapeDtypeStruct((bsz, seq, d), F32),
                   jax.ShapeDtypeStruct((bsz, HALO, d), F32)],
        scratch_shapes=[pltpu.VMEM((tm + HALO, d), F32), pltpu.VMEM((tm, d), F32)],
        compiler_params=_cparams(("arbitrary", "arbitrary")),
        name="l0_mix_prompt",
    )(x, w_in, w_out, cw, g, b, carry)


def l0_mix_small(x, st0, st1, w_in, w_out, cw, g, b):
    n, d = x.shape
    return pl.pallas_call(
        _l0_small_kernel,
        grid=(1,),
        in_specs=[_full(a.shape) for a in (x, st0, st1, w_in, w_out, cw, g, b)],
        out_specs=[_full((n, d)), _full((n, d))],
        out_shape=[jax.ShapeDtypeStruct((n, d), F32), jax.ShapeDtypeStruct((n, d), F32)],
        scratch_shapes=[pltpu.VMEM((n + HALO, d), F32), pltpu.VMEM((n, d), F32)],
        compiler_params=_cparams(("arbitrary",)),
        name="l0_mix_small",
    )(x, st0, st1, w_in, w_out, cw, g, b)


def _slab(ref, s, rows):
    return ref[pl.ds(s, rows, stride=ROW_SLABS), :]


def _to_slabs(o_ref, val):
    rows = val.shape[0]
    for s in range(ROW_SLABS):
        o_ref[pl.ds(s, rows, stride=ROW_SLABS), :] = val[:, s * LANES:(s + 1) * LANES]


def _swiglu_kernel(be_ref, nu_ref, x_ref, w1_ref, w3_ref, w2_ref, *rest, epilogue):
    o_ref, acc_ref = rest[-2], rest[-1]
    i = pl.program_id(0)
    f = pl.program_id(1)
    nf = pl.num_programs(1)

    @pl.when(i < nu_ref[0])
    def _():
        xb = x_ref[...].astype(BF16)
        a = jnp.dot(xb, w1_ref[...], preferred_element_type=F32)
        b = jnp.dot(xb, w3_ref[...], preferred_element_type=F32)
        h = (_silu(a) * b).astype(BF16)
        part = jnp.dot(h, w2_ref[...], preferred_element_type=F32)

        @pl.when(f == 0)
        def _():
            acc_ref[...] = part

        @pl.when(f > 0)
        def _():
            acc_ref[...] += part

        @pl.when(f == nf - 1)
        def _():
            if epilogue == "ln":
                g_ref, b_ref = rest[0], rest[1]
                o_ref[...] = _layer_norm(DEEPNORM_ALPHA * x_ref[...] + acc_ref[...], g_ref[...], b_ref[...])
            else:
                _to_slabs(o_ref, acc_ref[...])

    @pl.when(i >= nu_ref[0])
    def _():
        o_ref[...] = jnp.zeros(o_ref.shape, o_ref.dtype)


def swiglu_grouped(block_e, n_used, x, w1, w3, w2, extras, epilogue, tm, tf=1792):
    n_rows, d = x.shape
    nb = n_rows // tm
    nf = D_FF // tf

    def f_idx(i, f, nu):
        return jnp.where(i < nu[0], f, nf - 1)

    extra_specs = [pl.BlockSpec(e.shape, lambda i, f, be, nu: (0, 0)) for e in extras]
    if epilogue == "ln":
        out_spec = pl.BlockSpec((tm, d), lambda i, f, be, nu: (i, 0))
        out_shape = jax.ShapeDtypeStruct((n_rows, d), F32)
    else:
        out_spec = pl.BlockSpec((tm * ROW_SLABS, LANES), lambda i, f, be, nu: (i, 0))
        out_shape = jax.ShapeDtypeStruct((n_rows * ROW_SLABS, LANES), F32)
    grid_spec = pltpu.PrefetchScalarGridSpec(
        num_scalar_prefetch=2,
        grid=(nb, nf),
        in_specs=[pl.BlockSpec((tm, d), lambda i, f, be, nu: (i, 0)),
                  pl.BlockSpec((None, d, tf), lambda i, f, be, nu: (be[i], 0, f_idx(i, f, nu))),
                  pl.BlockSpec((None, d, tf), lambda i, f, be, nu: (be[i], 0, f_idx(i, f, nu))),
                  pl.BlockSpec((None, tf, d), lambda i, f, be, nu: (be[i], f_idx(i, f, nu), 0)),
                  ] + extra_specs,
        out_specs=out_spec,
        scratch_shapes=[pltpu.VMEM((tm, d), F32)],
    )
    return pl.pallas_call(
        functools.partial(_swiglu_kernel, epilogue=epilogue),
        grid_spec=grid_spec,
        out_shape=out_shape,
        compiler_params=_cparams(("arbitrary", "arbitrary")),
        name="swiglu_" + epilogue,
    )(block_e, n_used, x, w1, w3, w2, *extras)


def _mm_kernel(x_ref, w_ref, o_ref):
    o_ref[...] = jnp.dot(x_ref[...].astype(BF16), w_ref[...], preferred_element_type=F32).astype(o_ref.dtype)


def matmul(x, w, out_dtype, tm, tn):
    n, k = x.shape
    _, m = w.shape
    return pl.pallas_call(
        _mm_kernel,
        grid=(n // tm, m // tn),
        in_specs=[pl.BlockSpec((tm, k), lambda i, j: (i, 0)),
                  pl.BlockSpec((k, tn), lambda i, j: (0, j))],
        out_specs=pl.BlockSpec((tm, tn), lambda i, j: (i, j)),
        out_shape=jax.ShapeDtypeStruct((n, m), out_dtype),
        compiler_params=_cparams(("arbitrary", "arbitrary")),
        name="matmul",
    )(x, w)


def _top2(logits):
    mx = jnp.max(logits, axis=-1, keepdims=True)
    ex = jnp.exp(logits - mx)
    p = ex / jnp.sum(ex, axis=-1, keepdims=True)
    lane = lax.broadcasted_iota(jnp.int32, p.shape, 1)
    p1 = jnp.max(p, axis=-1, keepdims=True)
    i1 = jnp.min(jnp.where(p == p1, lane, N_EXPERTS), axis=-1, keepdims=True)
    rest = jnp.where(lane == i1, -1.0, p)
    p2 = jnp.max(rest, axis=-1, keepdims=True)
    i2 = jnp.min(jnp.where(rest == p2, lane, N_EXPERTS), axis=-1, keepdims=True)
    tot = p1 + p2
    return jnp.concatenate([i1, i2], axis=-1), jnp.concatenate([p1 / tot, p2 / tot], axis=-1)


def _mm_ln_router_kernel(a_ref, x_ref, a_tail_ref, x_tail_ref, w_ref, g_ref, b_ref, wr_ref,
                         o3_ref, idx_ref, val_ref, *, n_main):
    def block(a_blk, x_blk):
        m = jnp.dot(a_blk[...], w_ref[...], preferred_element_type=F32)
        y = _layer_norm(DEEPNORM_ALPHA * x_blk[...] + m, g_ref[...], b_ref[...])
        _to_slabs(o3_ref, y)
        logits = jnp.dot(y, wr_ref[...], precision=lax.Precision.HIGHEST, preferred_element_type=F32)
        idx_ref[...], val_ref[...] = _top2(logits)

    @pl.when(pl.program_id(0) < n_main)
    def _():
        block(a_ref, x_ref)

    @pl.when(pl.program_id(0) >= n_main)
    def _():
        block(a_tail_ref, x_tail_ref)


def matmul_ln_router(a, x, a_tail, x_tail, w, g, b, wr):
    tm, k = a_tail.shape
    n = a.shape[0]
    d = w.shape[1]
    n_main = n // tm
    n_all = n + tm

    def main(i):
        return (jnp.minimum(i, n_main - 1), 0)

    return pl.pallas_call(
        functools.partial(_mm_ln_router_kernel, n_main=n_main),
        grid=(n_main + 1,),
        in_specs=[pl.BlockSpec((tm, k), main), pl.BlockSpec((tm, d), main),
                  _full(a_tail.shape), _full(x_tail.shape),
                  _full(w.shape), _full(g.shape), _full(b.shape), _full(wr.shape)],
        out_specs=[pl.BlockSpec((tm * ROW_SLABS, LANES), lambda i: (i, 0)),
                   pl.BlockSpec((tm, TOP_K), lambda i: (i, 0)), pl.BlockSpec((tm, TOP_K), lambda i: (i, 0))],
        out_shape=[jax.ShapeDtypeStruct((n_all * ROW_SLABS, LANES), F32),
                   jax.ShapeDtypeStruct((n_all, TOP_K), jnp.int32), jax.ShapeDtypeStruct((n_all, TOP_K), F32)],
        compiler_params=_cparams(("arbitrary",)),
        name="matmul_ln_router",
    )(a, x, a_tail, x_tail, w, g, b, wr)


def _l2norm_heads(x, n_heads, scale):
    outs = []
    for h in range(n_heads):
        xh = x[:, h * GD_DH:(h + 1) * GD_DH]
        ss = jnp.sum(xh * xh, axis=-1, keepdims=True)
        outs.append(xh * (lax.rsqrt(ss + RMS_EPS) * scale))
    return outs


def _softplus(x):
    return jnp.maximum(x, 0.0) + jnp.log1p(jnp.exp(-jnp.abs(x)))


def _gated_rmsnorm(o, z, norm_g):
    ms = jnp.mean(o * o, axis=-1, keepdims=True)
    return o * lax.rsqrt(ms + RMS_EPS) * norm_g * _silu(z)


PAIR = GD_HV // GD_HK
PAIR_SQ = PAIR * CHUNK
PAIR_DV = PAIR * GD_DH


def _sq_cols(x, p):
    lane = lax.broadcasted_iota(jnp.int32, (CHUNK, PAIR_SQ), 1)
    a = jnp.broadcast_to(x[:, PAIR * p:PAIR * p + 1], (CHUNK, PAIR_SQ))
    b = jnp.broadcast_to(x[:, PAIR * p + 1:PAIR * p + 2], (CHUNK, PAIR_SQ))
    return jnp.where(lane < CHUNK, a, b)


def _dv_cols(x, p):
    r = x.shape[0]
    return jnp.concatenate([jnp.broadcast_to(x[:, PAIR * p + j:PAIR * p + j + 1], (r, GD_DH))
                            for j in range(PAIR)], axis=1)


def _block_diag_sq(m):
    lane = lax.broadcasted_iota(jnp.int32, m.shape, 1)
    top = jnp.where(lane < CHUNK, m, 0.0).astype(BF16)
    bot = jnp.where(lane < CHUNK, 0.0, m).astype(BF16)
    return jnp.concatenate([top, bot], axis=0)


def _block_diag_dv(m):
    mb = m.astype(BF16)
    zero = jnp.zeros((CHUNK, GD_DH), BF16)
    top = jnp.concatenate([mb[:, :GD_DH], zero], axis=1)
    bot = jnp.concatenate([zero, mb[:, GD_DH:]], axis=1)
    return jnp.concatenate([top, bot], axis=0)


def _tri_inverse_pairs(a_list):
    row = lax.broadcasted_iota(jnp.int32, (CHUNK, PAIR_SQ), 0)
    col = lax.broadcasted_iota(jnp.int32, (CHUNK, PAIR_SQ), 1) % CHUNK
    eye = jnp.where(row == col, 1.0, 0.0).astype(F32)
    s = 1
    inv = None
    while s < CHUNK:
        sub = ((row // (2 * s)) == (col // (2 * s))) & ((row // s) % 2 == 1) & ((col // s) % 2 == 0)
        e_list = [jnp.where(sub, a, 0.0) for a in a_list]
        if s == 1:
            inv = [eye - e for e in e_list]
        else:
            x_list = [jnp.dot(e.astype(BF16), _block_diag_sq(d), preferred_element_type=F32)
                      for e, d in zip(e_list, inv)]
            y_list = [jnp.dot(d.astype(BF16), _block_diag_sq(x), preferred_element_type=F32)
                      for d, x in zip(inv, x_list)]
            inv = [d - y for d, y in zip(inv, y_list)]
        s *= 2
    return inv


def _gdn_chunk_kernel(qkv_ref, z_ref, ba_ref, bat_ref, cw_ref, alog_ref, dtb_ref, alogt_ref, dtbt_ref,
                      ng_ref, carry_ref, s0_ref, o_ref, tail_ref, sout_ref,
                      buf_ref, s_ref, q_ref, k_ref, v_ref, tinv_ref, attn_ref, cum_ref, beta_ref,
                      *, valid_rows):
    tc = qkv_ref.shape[0]
    t = pl.program_id(1)

    @pl.when(t == 0)
    def _():
        buf_ref[0:HALO, :] = carry_ref[...]
        for h in range(GD_HV):
            j = h % PAIR
            s_ref[h // PAIR, :, j * GD_DH:(j + 1) * GD_DH] = s0_ref[h]

    buf_ref[HALO:HALO + tc, :] = qkv_ref[...]
    col_chunk = 512
    for j in range(GD_CONV_CH // col_chunk):
        lo, hi = j * col_chunk, (j + 1) * col_chunk
        y = cw_ref[GD_CONV - 1:GD_CONV, lo:hi] * buf_ref[HALO:HALO + tc, lo:hi]
        for i in range(GD_CONV - 1):
            off = HALO - (GD_CONV - 1) + i
            y = y + cw_ref[i:i + 1, lo:hi] * buf_ref[off:off + tc, lo:hi]
        act = _silu(y)
        heads_per = col_chunk // GD_DH
        if lo < GD_K_DIM:
            parts = _l2norm_heads(act, heads_per, GD_DH ** -0.5)
            for p, val in enumerate(parts):
                q_ref[:, lo + p * GD_DH:lo + (p + 1) * GD_DH] = val.astype(BF16)
        elif lo < 2 * GD_K_DIM:
            parts = _l2norm_heads(act, heads_per, 1.0)
            for p, val in enumerate(parts):
                c0 = lo - GD_K_DIM + p * GD_DH
                k_ref[:, c0:c0 + GD_DH] = val.astype(BF16)
        else:
            v_ref[:, lo - 2 * GD_K_DIM:hi - 2 * GD_K_DIM] = act
    tail = buf_ref[tc:tc + HALO, :]
    tail_ref[...] = tail
    buf_ref[0:HALO, :] = tail

    row = lax.broadcasted_iota(jnp.int32, (CHUNK, CHUNK), 0)
    col = lax.broadcasted_iota(jnp.int32, (CHUNK, CHUNK), 1)
    tril_ones = jnp.where(row >= col, 1.0, 0.0).astype(F32)
    triu_ones = jnp.where(row <= col, 1.0, 0.0).astype(F32)
    hp = lax.Precision.HIGHEST
    row2 = lax.broadcasted_iota(jnp.int32, (CHUNK, PAIR_SQ), 0)
    col2 = lax.broadcasted_iota(jnp.int32, (CHUNK, PAIR_SQ), 1) % CHUNK
    incl2 = row2 >= col2
    strict2 = row2 > col2
    n_pairs = GD_HK

    def local_body(c, carry):
        r0 = pl.multiple_of(c * CHUNK, CHUNK)
        ba = ba_ref[c]
        bat = bat_ref[c]
        beta = jax.nn.sigmoid(ba[:, :GD_HV])
        g = -jnp.exp(alog_ref[...]) * _softplus(ba[:, GD_HV:] + dtb_ref[...])
        gt = -jnp.exp(alogt_ref[...]) * _softplus(bat[GD_HV:, :] + dtbt_ref[...])
        if valid_rows is not None:
            rid = lax.broadcasted_iota(jnp.int32, (CHUNK, GD_HV), 0) + r0 + t * tc
            beta = jnp.where(rid < valid_rows, beta, 0.0)
            g = jnp.where(rid < valid_rows, g, 0.0)
            cid = lax.broadcasted_iota(jnp.int32, (GD_HV, CHUNK), 1) + r0 + t * tc
            gt = jnp.where(cid < valid_rows, gt, 0.0)
        cum = jnp.dot(tril_ones, g, precision=hp, preferred_element_type=F32)
        cumt = jnp.dot(gt, triu_ones, precision=hp, preferred_element_type=F32)
        cum_ref[c] = cum
        beta_ref[c] = beta

        grams = []
        for p in range(n_pairs):
            kb = k_ref[pl.ds(r0, CHUNK), p * GD_DH:(p + 1) * GD_DH]
            qb = q_ref[pl.ds(r0, CHUNK), p * GD_DH:(p + 1) * GD_DH]
            grams.append(lax.dot_general(jnp.concatenate([kb, qb], axis=0), jnp.concatenate([kb, kb], axis=0),
                                         (((1,), (1,)), ((), ())), preferred_element_type=F32))
        a_list = []
        for p in range(n_pairs):
            cum_row = jnp.concatenate([cumt[PAIR * p + j:PAIR * p + j + 1, :] for j in range(PAIR)], axis=1)
            diff = _sq_cols(cum, p) - cum_row
            dec = jnp.where(incl2, jnp.exp(jnp.where(incl2, diff, 0.0)), 0.0)
            attn_ref[c, p] = (grams[p][CHUNK:] * dec).astype(BF16)
            a_list.append(jnp.where(strict2, dec * grams[p][:CHUNK], 0.0) * _sq_cols(beta, p))
        for p, inv in enumerate(_tri_inverse_pairs(a_list)):
            tinv_ref[c, p] = inv.astype(BF16)
        return carry

    lax.fori_loop(0, tc // CHUNK, local_body, 0)

    def state_body(c, carry):
        r0 = pl.multiple_of(c * CHUNK, CHUNK)
        cum = cum_ref[c]
        beta = beta_ref[c]
        last = cum[CHUNK - 1:CHUNK, :]
        ecum = jnp.exp(cum)
        elast = jnp.exp(last - cum)
        glast = jnp.exp(last)
        kbs, s_old, kqs = [], [], []
        for p in range(n_pairs):
            kb = k_ref[pl.ds(r0, CHUNK), p * GD_DH:(p + 1) * GD_DH]
            qb = q_ref[pl.ds(r0, CHUNK), p * GD_DH:(p + 1) * GD_DH]
            s2 = s_ref[p]
            kbs.append(kb)
            s_old.append(s2)
            kqs.append(jnp.dot(jnp.concatenate([kb, qb], axis=0), s2.astype(BF16), preferred_element_type=F32))
        ws = []
        for p in range(n_pairs):
            v2 = v_ref[pl.ds(r0, CHUNK), p * PAIR_DV:(p + 1) * PAIR_DV]
            rhs = _dv_cols(beta, p) * (v2 - _dv_cols(ecum, p) * kqs[p][:CHUNK])
            ws.append(jnp.dot(tinv_ref[c, p], _block_diag_dv(rhs), preferred_element_type=F32))
        for p in range(n_pairs):
            o2 = _dv_cols(ecum, p) * kqs[p][CHUNK:] + jnp.dot(attn_ref[c, p], _block_diag_dv(ws[p]),
                                                              preferred_element_type=F32)
            upd = lax.dot_general(kbs[p], (_dv_cols(elast, p) * ws[p]).astype(BF16), (((0,), (0,)), ((), ())),
                                  preferred_element_type=F32)
            s_ref[p] = _dv_cols(glast, p) * s_old[p] + upd
            for j in range(PAIR):
                h = PAIR * p + j
                z = z_ref[pl.ds(r0, CHUNK), h * GD_DH:(h + 1) * GD_DH].astype(F32)
                o_ref[pl.ds(r0, CHUNK), h * GD_DH:(h + 1) * GD_DH] = _gated_rmsnorm(
                    o2[:, j * GD_DH:(j + 1) * GD_DH], z, ng_ref[...]).astype(o_ref.dtype)
        return carry

    lax.fori_loop(0, tc // CHUNK, state_body, 0)
    for h in range(GD_HV):
        j = h % PAIR
        sout_ref[h] = s_ref[h // PAIR, :, j * GD_DH:(j + 1) * GD_DH]


def gdn_chunked(qkv, z, ba, bat, cw, a_log, dt_bias, norm_g, carry, s0, tc, valid_rows=None):
    bsz, seq, _ = qkv.shape
    nchunk = tc // CHUNK
    ba4 = ba.reshape(bsz, seq // CHUNK, CHUNK, 2 * GD_HV)
    bat4 = bat.reshape(bsz, 2 * GD_HV, seq // CHUNK, CHUNK).transpose(0, 2, 1, 3)
    alog = a_log.reshape(1, GD_HV)
    dtb = dt_bias.reshape(1, GD_HV)
    alogt = a_log.reshape(GD_HV, 1)
    dtbt = dt_bias.reshape(GD_HV, 1)
    ng = norm_g.reshape(1, GD_DH)
    return pl.pallas_call(
        functools.partial(_gdn_chunk_kernel, valid_rows=valid_rows),
        grid=(bsz, seq // tc),
        in_specs=[pl.BlockSpec((None, tc, GD_CONV_CH), lambda i, t: (i, t, 0)),
                  pl.BlockSpec((None, tc, GD_V_DIM), lambda i, t: (i, t, 0)),
                  pl.BlockSpec((None, nchunk, CHUNK, 2 * GD_HV), lambda i, t: (i, t, 0, 0)),
                  pl.BlockSpec((None, nchunk, 2 * GD_HV, CHUNK), lambda i, t: (i, t, 0, 0)),
                  _full(cw.shape), _full(alog.shape), _full(dtb.shape), _full(alogt.shape), _full(dtbt.shape),
                  _full(ng.shape), _full(carry.shape), _full(s0.shape)],
        out_specs=[pl.BlockSpec((None, tc, GD_V_DIM), lambda i, t: (i, t, 0)),
                   pl.BlockSpec((None, HALO, GD_CONV_CH), lambda i, t: (i, 0, 0)),
                   pl.BlockSpec((None, GD_HV, GD_DH, GD_DH), lambda i, t: (i, 0, 0, 0))],
        out_shape=[jax.ShapeDtypeStruct((bsz, seq, GD_V_DIM), BF16),
                   jax.ShapeDtypeStruct((bsz, HALO, GD_CONV_CH), F32),
                   jax.ShapeDtypeStruct((bsz, GD_HV, GD_DH, GD_DH), F32)],
        scratch_shapes=[pltpu.VMEM((tc + HALO, GD_CONV_CH), F32),
                        pltpu.VMEM((GD_HK, GD_DH, PAIR_DV), F32),
                        pltpu.VMEM((tc, GD_K_DIM), BF16), pltpu.VMEM((tc, GD_K_DIM), BF16),
                        pltpu.VMEM((tc, GD_V_DIM), F32),
                        pltpu.VMEM((nchunk, GD_HK, CHUNK, PAIR_SQ), BF16),
                        pltpu.VMEM((nchunk, GD_HK, CHUNK, PAIR_SQ), BF16),
                        pltpu.VMEM((nchunk, CHUNK, GD_HV), F32),
                        pltpu.VMEM((nchunk, CHUNK, GD_HV), F32)],
        compiler_params=_cparams(("arbitrary", "arbitrary")),
        name="gdn_chunked",
    )(qkv, z, ba4, bat4, cw, alog, dtb, alogt, dtbt, ng, carry, s0)


def _gdn_step_kernel(cur_ref, st_ref, z_ref, ba_ref, cw_ref, alog_ref, dtb_ref, ng_ref, s_ref,
                     o_ref, sout_ref, oacc_ref):
    ns = cur_ref.shape[0]
    y = cw_ref[GD_CONV - 1:GD_CONV, :] * cur_ref[...]
    for i in range(GD_CONV - 1):
        y = y + cw_ref[i:i + 1, :] * st_ref[:, i, :]
    act = _silu(y)
    qh = _l2norm_heads(act[:, :GD_K_DIM], GD_HK, GD_DH ** -0.5)
    kh = _l2norm_heads(act[:, GD_K_DIM:2 * GD_K_DIM], GD_HK, 1.0)
    qk_t = jnp.concatenate(qh + kh, axis=0).T
    ba = ba_ref[...]
    beta = jax.nn.sigmoid(ba[:, :GD_HV])
    eg = jnp.exp(-jnp.exp(alog_ref[...]) * _softplus(ba[:, GD_HV:] + dtb_ref[...]))
    for s in range(ns):
        for h in range(GD_HV):
            g = h // (GD_HV // GD_HK)
            qcol = qk_t[:, g * ns + s:g * ns + s + 1]
            kcol = qk_t[:, (GD_HK + g) * ns + s:(GD_HK + g) * ns + s + 1]
            sd = s_ref[s, h] * eg[s:s + 1, h:h + 1]
            ks = jnp.sum(sd * kcol, axis=0, keepdims=True)
            v = act[s:s + 1, 2 * GD_K_DIM + h * GD_DH:2 * GD_K_DIM + (h + 1) * GD_DH]
            w = beta[s:s + 1, h:h + 1] * (v - ks)
            sn = sd + kcol * w
            sout_ref[s, h] = sn
            oacc_ref[s:s + 1, h * GD_DH:(h + 1) * GD_DH] = jnp.sum(sn * qcol, axis=0, keepdims=True)
    for h in range(GD_HV):
        lo, hi = h * GD_DH, (h + 1) * GD_DH
        o_ref[:, lo:hi] = _gated_rmsnorm(oacc_ref[:, lo:hi], z_ref[:, lo:hi].astype(F32),
                                         ng_ref[...]).astype(o_ref.dtype)


def gdn_step(cur, st, z, ba, cw, a_log, dt_bias, norm_g, s0, ns=8):
    n = cur.shape[0]
    alog = a_log.reshape(1, GD_HV)
    dtb = dt_bias.reshape(1, GD_HV)
    ng = norm_g.reshape(1, GD_DH)
    return pl.pallas_call(
        _gdn_step_kernel,
        grid=(n // ns,),
        in_specs=[pl.BlockSpec((ns, GD_CONV_CH), lambda i: (i, 0)),
                  pl.BlockSpec((ns, GD_CONV - 1, GD_CONV_CH), lambda i: (i, 0, 0)),
                  pl.BlockSpec((ns, GD_V_DIM), lambda i: (i, 0)),
                  pl.BlockSpec((ns, 2 * GD_HV), lambda i: (i, 0)),
                  _full(cw.shape), _full(alog.shape), _full(dtb.shape), _full(ng.shape),
                  pl.BlockSpec((ns, GD_HV, GD_DH, GD_DH), lambda i: (i, 0, 0, 0))],
        out_specs=[pl.BlockSpec((ns, GD_V_DIM), lambda i: (i, 0)),
                   pl.BlockSpec((ns, GD_HV, GD_DH, GD_DH), lambda i: (i, 0, 0, 0))],
        out_shape=[jax.ShapeDtypeStruct((n, GD_V_DIM), BF16),
                   jax.ShapeDtypeStruct((n, GD_HV, GD_DH, GD_DH), F32)],
        scratch_shapes=[pltpu.VMEM((ns, GD_V_DIM), F32)],
        compiler_params=_cparams(("arbitrary",)),
        name="gdn_step",
    )(cur, st, z, ba, cw, alog, dtb, ng, s0)


def _row_copy(src_hbm, row, dst, slot, sem):
    return pltpu.make_async_copy(src_hbm.at[pl.ds(pl.multiple_of(row * ROW_SLABS, ROW_SLABS), ROW_SLABS)],
                                 dst.at[pl.ds(pl.multiple_of(slot * ROW_SLABS, ROW_SLABS), ROW_SLABS)], sem)


def _start_row_gather(src_hbm, idx_ref, base, stride, n, dst, sem):
    def issue(blk, c):
        r0 = blk * DMA_UNROLL
        for u in range(DMA_UNROLL):
            _row_copy(src_hbm, idx_ref[base + stride * (r0 + u)], dst, r0 + u, sem).start()
        return c

    lax.fori_loop(0, n // DMA_UNROLL, issue, 0)


def _wait_row_gather(src_hbm, n, dst, sem):
    pltpu.make_async_copy(src_hbm.at[pl.ds(0, n * ROW_SLABS)], dst, sem).wait()


def _dispatch_kernel(src_ref, nu_ref, x_hbm, o_ref, buf_ref, sem):
    tm = o_ref.shape[0]
    i = pl.program_id(0)

    @pl.when(i < nu_ref[0])
    def _():
        _start_row_gather(x_hbm, src_ref, i * tm, 1, tm, buf_ref, sem)
        _wait_row_gather(x_hbm, tm, buf_ref, sem)
        for s in range(ROW_SLABS):
            o_ref[:, s * LANES:(s + 1) * LANES] = _slab(buf_ref, s, tm).astype(o_ref.dtype)

    @pl.when(i >= nu_ref[0])
    def _():
        o_ref[...] = jnp.zeros(o_ref.shape, o_ref.dtype)


def dispatch(src, n_used, x3, n_rows, tm):
    grid_spec = pltpu.PrefetchScalarGridSpec(
        num_scalar_prefetch=2,
        grid=(n_rows // tm,),
        in_specs=[pl.BlockSpec(memory_space=pl.ANY)],
        out_specs=pl.BlockSpec((tm, D_MODEL), lambda i, s, nu: (i, 0)),
        scratch_shapes=[pltpu.VMEM((tm * ROW_SLABS, LANES), F32), pltpu.SemaphoreType.DMA(())],
    )
    return pl.pallas_call(
        _dispatch_kernel,
        grid_spec=grid_spec,
        out_shape=jax.ShapeDtypeStruct((n_rows, D_MODEL), BF16),
        compiler_params=_cparams(("arbitrary",)),
        name="moe_dispatch",
    )(src, n_used, x3)


def _combine_kernel(dest_ref, x3_ref, pv_ref, g_ref, b_ref, y_hbm, o_ref, buf_ref, sem, *, first_block):
    tt = o_ref.shape[0]
    base = (pl.program_id(0) + first_block) * tt * TOP_K
    for k in range(TOP_K):
        _start_row_gather(y_hbm, dest_ref, base + k, TOP_K, tt, buf_ref.at[k], sem.at[k])
    for k in range(TOP_K):
        _wait_row_gather(y_hbm, tt, buf_ref.at[k], sem.at[k])
    pv = pv_ref[...]
    slabs = []
    for s in range(ROW_SLABS):
        y = pv[:, 0:1] * _slab(buf_ref.at[0], s, tt) + pv[:, 1:2] * _slab(buf_ref.at[1], s, tt)
        slabs.append(DEEPNORM_ALPHA * _slab(x3_ref, s, tt) + y)
    mu = sum(jnp.sum(r, axis=-1, keepdims=True) for r in slabs) * (1.0 / D_MODEL)
    var = sum(jnp.sum((r - mu) * (r - mu), axis=-1, keepdims=True) for r in slabs) * (1.0 / D_MODEL)
    inv = lax.rsqrt(var + LN_EPS)
    for s in range(ROW_SLABS):
        cs = slice(s * LANES, (s + 1) * LANES)
        o_ref[:, cs] = (slabs[s] - mu) * inv * g_ref[:, cs] + b_ref[:, cs]


def combine(dest, x3, pv, g, b, yb3, tt, first_block, n_blocks):
    grid_spec = pltpu.PrefetchScalarGridSpec(
        num_scalar_prefetch=1,
        grid=(n_blocks,),
        in_specs=[pl.BlockSpec((tt * ROW_SLABS, LANES), lambda i, dst: (i + first_block, 0)),
                  pl.BlockSpec((tt, TOP_K), lambda i, dst: (i + first_block, 0)),
                  pl.BlockSpec(g.shape, lambda i, dst: (0, 0)),
                  pl.BlockSpec(b.shape, lambda i, dst: (0, 0)),
                  pl.BlockSpec(memory_space=pl.ANY)],
        out_specs=pl.BlockSpec((tt, D_MODEL), lambda i, dst: (i, 0)),
        scratch_shapes=[pltpu.VMEM((TOP_K, tt * ROW_SLABS, LANES), F32), pltpu.SemaphoreType.DMA((TOP_K,))],
    )
    return pl.pallas_call(
        functools.partial(_combine_kernel, first_block=first_block),
        grid_spec=grid_spec,
        out_shape=jax.ShapeDtypeStruct((n_blocks * tt, D_MODEL), F32),
        compiler_params=_cparams(("arbitrary",)),
        name="moe_combine",
    )(dest, x3, pv, g, b, yb3)


def moe_experts(x3, topi, w1, w3, w2, tm):
    n = topi.shape[0]
    eid = topi.reshape(-1)
    onehot = (eid[:, None] == jnp.arange(N_EXPERTS, dtype=jnp.int32)[None, :]).astype(jnp.int32)
    csum = jnp.cumsum(onehot, axis=0)
    rank = jnp.sum((csum - onehot) * onehot, axis=1)
    counts = csum[-1]
    padded = (counts + tm - 1) // tm * tm
    pends = jnp.cumsum(padded)
    pstarts = pends - padded
    dest = (pstarts[eid] + rank).astype(jnp.int32)
    nb = (n * TOP_K + tm - 1) // tm + N_EXPERTS
    n_rows = nb * tm
    tok = jnp.arange(n * TOP_K, dtype=jnp.int32) // TOP_K
    src = jnp.zeros((n_rows,), jnp.int32).at[dest].set(tok, unique_indices=True)
    block_start = jnp.arange(nb, dtype=jnp.int32) * tm
    block_e = jnp.minimum(jnp.searchsorted(pends, block_start, side="right"), N_EXPERTS - 1).astype(jnp.int32)
    n_used = (pends[-1] // tm).astype(jnp.int32).reshape(1)

    xs = dispatch(src, n_used, x3, n_rows, tm)
    yb3 = swiglu_grouped(block_e, n_used, xs, w1, w3, w2, [], "rows", tm)
    return yb3, dest


def kernel(x_prompt, x_sample, state_conv_a, state_conv_b, state_delta, meta_tokens, ln_g, ln_b, sc_w_in, sc_conv, sc_w_out, ffn_w1, ffn_w3, ffn_w2, gd_w_in, gd_conv, gd_a_log, gd_dt_bias, gd_norm_g, gd_w_out, moe_router, moe_w1, moe_w3, moe_w2):
    bsz, seq, d = x_prompt.shape
    n_s = x_sample.shape[0]
    n_small = N_META + n_s
    n_p = bsz * seq

    def row(v):
        return v.reshape(1, -1)

    sc_w_in_b = sc_w_in[0].astype(BF16)
    sc_w_out_b = sc_w_out[0].astype(BF16)
    ffn_w1_b, ffn_w3_b, ffn_w2_b = ffn_w1.astype(BF16), ffn_w3.astype(BF16), ffn_w2.astype(BF16)
    gd_w_qkv_b = gd_w_in[0][:, :GD_CONV_CH].astype(BF16)
    gd_w_z_b = gd_w_in[0][:, GD_CONV_CH:GD_CONV_CH + GD_V_DIM].astype(BF16)
    gd_w_ba_b = gd_w_in[0][:, GD_CONV_CH + GD_V_DIM:].astype(BF16)
    gd_w_out_b = gd_w_out[0].astype(BF16)
    moe_w1_b, moe_w3_b, moe_w2_b = moe_w1[0].astype(BF16), moe_w3[0].astype(BF16), moe_w2[0].astype(BF16)

    x_small = jnp.concatenate([meta_tokens.astype(F32), x_sample.reshape(n_s, d)], axis=0)

    xa_s, ch_s = l0_mix_small(x_small, state_conv_a[0, :, 0], state_conv_a[0, :, 1], sc_w_in_b, sc_w_out_b,
                              sc_conv[0], row(ln_g[0, 0]), row(ln_b[0, 0]))
    xa_p, tail_a = l0_mix_prompt(x_prompt, sc_w_in_b, sc_w_out_b, sc_conv[0], row(ln_g[0, 0]), row(ln_b[0, 0]),
                                 ch_s[N_META - HALO:N_META])
    one = jnp.ones((1,), jnp.int32)

    def dense_ffn(x, tm):
        nb = x.shape[0] // tm
        return swiglu_grouped(jnp.zeros((nb,), jnp.int32), one * nb, x, ffn_w1_b, ffn_w3_b, ffn_w2_b,
                              [row(ln_g[0, 1]), row(ln_b[0, 1])], "ln", tm)

    xb_s = dense_ffn(xa_s, n_small)
    xb_p = dense_ffn(xa_p.reshape(n_p, d), 512)

    def gdn_inproj(x, tm):
        qkv = matmul(x, gd_w_qkv_b, F32, tm, 1024)
        z = matmul(x, gd_w_z_b, BF16, tm, 1024)
        ba = matmul(x, gd_w_ba_b, F32, tm, 2 * GD_HV)
        return qkv, z, ba

    qkv_s, z_s, ba_s = gdn_inproj(xb_s, n_small)
    qkv_p, z_p, ba_p = gdn_inproj(xb_p, 1024)

    pad = CHUNK - N_META

    def meta_pad(a):
        return jnp.pad(a[:N_META], ((0, pad), (0, 0)))[None]

    ba_m = meta_pad(ba_s)
    o_m, _, s_meta = gdn_chunked(meta_pad(qkv_s), meta_pad(z_s), ba_m, jnp.swapaxes(ba_m, 1, 2), gd_conv[0],
                                 gd_a_log[0], gd_dt_bias[0], gd_norm_g[0],
                                 jnp.zeros((HALO, GD_CONV_CH), F32), jnp.zeros((GD_HV, GD_DH, GD_DH), F32),
                                 CHUNK, valid_rows=N_META)
    o_smp, s_smp = gdn_step(qkv_s[N_META:], state_conv_b[0], z_s[N_META:], ba_s[N_META:], gd_conv[0],
                            gd_a_log[0], gd_dt_bias[0], gd_norm_g[0], state_delta[0])
    ba_p3 = ba_p.reshape(bsz, seq, 2 * GD_HV)
    o_p, tail_b, s_p = gdn_chunked(qkv_p.reshape(bsz, seq, GD_CONV_CH), z_p.reshape(bsz, seq, GD_V_DIM), ba_p3,
                                   jnp.swapaxes(ba_p3, 1, 2), gd_conv[0], gd_a_log[0], gd_dt_bias[0],
                                   gd_norm_g[0], qkv_s[N_META - HALO:N_META], s_meta[0], 256)
    tt = TOKEN_BLOCK
    small_pad = ((0, tt - n_small), (0, 0))
    o_s = jnp.pad(jnp.concatenate([o_m[0, :N_META], o_smp], axis=0), small_pad)
    xc3, topi, topv = matmul_ln_router(o_p.reshape(n_p, GD_V_DIM), xb_p, o_s, jnp.pad(xb_s, small_pad),
                                       gd_w_out_b, row(ln_g[1, 0]), row(ln_b[1, 0]), moe_router[0])
    yb3, dest = moe_experts(xc3, topi, moe_w1_b, moe_w3_b, moe_w2_b, EXPERT_BLOCK)
    xd_p = combine(dest, xc3, topv, row(ln_g[1, 1]), row(ln_b[1, 1]), yb3, tt, 0, n_p // tt)
    xd_s = combine(dest, xc3, topv, row(ln_g[1, 1]), row(ln_b[1, 1]), yb3, tt, n_p // tt, 1)

    y_prompt = xd_p.reshape(bsz, seq, d)
    y_sample = xd_s[N_META:n_small].reshape(n_s, 1, d)
    new_conv_a_prompt = tail_a[None, :, HALO - (SC_WIDTH - 1):]
    new_conv_b_prompt = tail_b[None, :, HALO - (GD_CONV - 1):]
    new_delta_prompt = s_p[None]
    new_conv_a_sample = jnp.stack([state_conv_a[0, :, 1], ch_s[N_META:]], axis=1)[None]
    new_conv_b_sample = jnp.concatenate([state_conv_b[0, :, 1:], qkv_s[N_META:, None]], axis=1)[None]
    new_delta_sample = s_smp[None]
    return (y_prompt, y_sample, new_conv_a_prompt, new_conv_b_prompt, new_delta_prompt,
            new_conv_a_sample, new_conv_b_sample, new_delta_sample)
```

```python
import functools

import jax
import jax.numpy as jnp
from jax import lax
from jax.experimental import pallas as pl
from jax.experimental.pallas import tpu as pltpu

F32 = jnp.float32
BF16 = jnp.bfloat16

D_MODEL = 1024
N_META = 16
SC_WIDTH = 3
GD_HK = 8
GD_HV = 16
GD_DH = 128
GD_K_DIM = GD_HK * GD_DH
GD_V_DIM = GD_HV * GD_DH
GD_CONV = 4
GD_CONV_CH = 2 * GD_K_DIM + GD_V_DIM
D_FF = 3584
N_EXPERTS = 8
TOP_K = 2
LN_EPS = 1e-5
RMS_EPS = 1e-6
DEPTH = 2
DEEPNORM_ALPHA = (2 * DEPTH) ** 0.25

LANES = 128
ROW_SLABS = D_MODEL // LANES
DMA_UNROLL = 8
TOKEN_BLOCK = 256
EXPERT_BLOCK = 512
CHUNK = 64
HALO = 8
V7X_VMEM_LIMIT = 56 * 1024 * 1024


def _cparams(sem, vmem=V7X_VMEM_LIMIT):
    return pltpu.CompilerParams(dimension_semantics=sem, vmem_limit_bytes=vmem)


def _bdot(a, b):
    return jnp.dot(a.astype(BF16), b.astype(BF16), preferred_element_type=F32)


def _bdot_nt(a, b):
    return lax.dot_general(a.astype(BF16), b.astype(BF16), (((1,), (1,)), ((), ())),
                           preferred_element_type=F32)


def _bdot_tn(a, b):
    return lax.dot_general(a.astype(BF16), b.astype(BF16), (((0,), (0,)), ((), ())),
                           preferred_element_type=F32)


def _bf16_pieces(x):
    p0 = x.astype(BF16)
    r1 = x - p0.astype(F32)
    p1 = r1.astype(BF16)
    p2 = (r1 - p1.astype(F32)).astype(BF16)
    return p0, p1, p2


def _dot_f32x3(a, b):
    a_hi = a.astype(BF16)
    b_hi = b.astype(BF16)
    a_lo = (a - a_hi.astype(F32)).astype(BF16)
    b_lo = (b - b_hi.astype(F32)).astype(BF16)
    return (jnp.dot(a_hi, b_hi, preferred_element_type=F32) + jnp.dot(a_lo, b_hi, preferred_element_type=F32)
            + jnp.dot(a_hi, b_lo, preferred_element_type=F32))


def _layer_norm(r, g, b):
    mu = jnp.mean(r, axis=-1, keepdims=True)
    d = r - mu
    var = jnp.mean(d * d, axis=-1, keepdims=True)
    return d * lax.rsqrt(var + LN_EPS) * g + b


def _silu(x):
    return x * jax.nn.sigmoid(x)


def _l0_inproj(xb, w_in_ref, ch_ref, bg_ref, row0, rows):
    col_chunk = 512
    for j in range(D_MODEL // col_chunk):
        lo, hi = j * col_chunk, (j + 1) * col_chunk
        bg = jnp.dot(xb, w_in_ref[:, lo:hi], preferred_element_type=F32)
        c = jnp.dot(xb, w_in_ref[:, D_MODEL + lo:D_MODEL + hi], preferred_element_type=F32)
        h = jnp.dot(xb, w_in_ref[:, 2 * D_MODEL + lo:2 * D_MODEL + hi], preferred_element_type=F32)
        ch_ref[row0:row0 + rows, lo:hi] = c * h
        bg_ref[:, lo:hi] = bg


def _l0_prompt_kernel(x_ref, w_in_ref, w_out_ref, cw_ref, g_ref, b_ref, carry_ref,
                      o_ref, tail_ref, buf_ref, bg_ref):
    tm = x_ref.shape[0]
    t = pl.program_id(1)

    @pl.when(t == 0)
    def _():
        buf_ref[0:HALO, :] = carry_ref[...]

    x = x_ref[...]
    _l0_inproj(x.astype(BF16), w_in_ref, buf_ref, bg_ref, HALO, tm)
    y = (cw_ref[0:1, :] * buf_ref[HALO - 2:HALO - 2 + tm, :]
         + cw_ref[1:2, :] * buf_ref[HALO - 1:HALO - 1 + tm, :]
         + cw_ref[2:3, :] * buf_ref[HALO:HALO + tm, :])
    u = (bg_ref[...] * y).astype(BF16)
    m = jnp.dot(u, w_out_ref[...], preferred_element_type=F32)
    o_ref[...] = _layer_norm(DEEPNORM_ALPHA * x + m, g_ref[...], b_ref[...])
    tail = buf_ref[tm:tm + HALO, :]
    tail_ref[...] = tail
    buf_ref[0:HALO, :] = tail


def _l0_small_kernel(x_ref, st0_ref, st1_ref, w_in_ref, w_out_ref, cw_ref, g_ref, b_ref,
                     o_ref, ch_out_ref, buf_ref, bg_ref):
    n = x_ref.shape[0]
    x = x_ref[...]
    buf_ref[0:HALO, :] = jnp.zeros((HALO, D_MODEL), F32)
    _l0_inproj(x.astype(BF16), w_in_ref, buf_ref, bg_ref, HALO, n)
    y_meta = (cw_ref[0:1, :] * buf_ref[HALO - 2:HALO - 2 + N_META, :]
              + cw_ref[1:2, :] * buf_ref[HALO - 1:HALO - 1 + N_META, :]
              + cw_ref[2:3, :] * buf_ref[HALO:HALO + N_META, :])
    ch = buf_ref[HALO:HALO + n, :]
    y_s = (cw_ref[0:1, :] * st0_ref[...] + cw_ref[1:2, :] * st1_ref[...]
           + cw_ref[2:3, :] * ch[N_META:, :])
    y = jnp.concatenate([y_meta, y_s], axis=0)
    u = (bg_ref[...] * y).astype(BF16)
    m = jnp.dot(u, w_out_ref[...], preferred_element_type=F32)
    o_ref[...] = _layer_norm(DEEPNORM_ALPHA * x + m, g_ref[...], b_ref[...])
    ch_out_ref[...] = ch


def _full(shape):
    nd = len(shape)
    return pl.BlockSpec(shape, lambda *_: (0,) * nd)


def l0_mix_prompt(x, w_in, w_out, cw, g, b, carry, tm=512):
    bsz, seq, d = x.shape
    return pl.pallas_call(
        _l0_prompt_kernel,
        grid=(bsz, seq // tm),
        in_specs=[pl.BlockSpec((None, tm, d), lambda i, t: (i, t, 0)),
                  _full(w_in.shape), _full(w_out.shape), _full(cw.shape), _full(g.shape), _full(b.shape),
                  _full(carry.shape)],
        out_specs=[pl.BlockSpec((None, tm, d), lambda i, t: (i, t, 0)),
                   pl.BlockSpec((None, HALO, d), lambda i, t: (i, 0, 0))],
        out_shape=[jax.ShapeDtypeStruct((bsz, seq, d), F32),
                   jax.ShapeDtypeStruct((bsz, HALO, d), F32)],
        scratch_shapes=[pltpu.VMEM((tm + HALO, d), F32), pltpu.VMEM((tm, d), F32)],
        compiler_params=_cparams(("arbitrary", "arbitrary")),
        name="l0_mix_prompt",
    )(x, w_in, w_out, cw, g, b, carry)


def l0_mix_small(x, st0, st1, w_in, w_out, cw, g, b):
    n, d = x.shape
    return pl.pallas_call(
        _l0_small_kernel,
        grid=(1,),
        in_specs=[_full(a.shape) for a in (x, st0, st1, w_in, w_out, cw, g, b)],
        out_specs=[_full((n, d)), _full((n, d))],
        out_shape=[jax.ShapeDtypeStruct((n, d), F32), jax.ShapeDtypeStruct((n, d), F32)],
        scratch_shapes=[pltpu.VMEM((n + HALO, d), F32), pltpu.VMEM((n, d), F32)],
        compiler_params=_cparams(("arbitrary",)),
        name="l0_mix_small",
    )(x, st0, st1, w_in, w_out, cw, g, b)


def _slab(ref, s, rows):
    return ref[pl.ds(s, rows, stride=ROW_SLABS), :]


def _to_slabs(o_ref, val):
    rows = val.shape[0]
    for s in range(ROW_SLABS):
        o_ref[pl.ds(s, rows, stride=ROW_SLABS), :] = val[:, s * LANES:(s + 1) * LANES]


def _swiglu_partial(xb, w1_ref, w3_ref, w2_ref, acc_ref):
    a = jnp.dot(xb, w1_ref[...], preferred_element_type=F32)
    b = jnp.dot(xb, w3_ref[...], preferred_element_type=F32)
    h = (_silu(a) * b).astype(BF16)
    part = jnp.dot(h, w2_ref[...], preferred_element_type=F32)

    @pl.when(pl.program_id(1) == 0)
    def _():
        acc_ref[...] = part

    @pl.when(pl.program_id(1) > 0)
    def _():
        acc_ref[...] += part


def _swiglu_ln_kernel(x_ref, w1_ref, w3_ref, w2_ref, g_ref, b_ref, o_ref, acc_ref):
    _swiglu_partial(x_ref[...].astype(BF16), w1_ref, w3_ref, w2_ref, acc_ref)

    @pl.when(pl.program_id(1) == pl.num_programs(1) - 1)
    def _():
        o_ref[...] = _layer_norm(DEEPNORM_ALPHA * x_ref[...] + acc_ref[...], g_ref[...], b_ref[...])


def swiglu_ln(x, w1, w3, w2, g, b, tm, tf=1792):
    n, d = x.shape
    return pl.pallas_call(
        _swiglu_ln_kernel,
        grid=(n // tm, D_FF // tf),
        in_specs=[pl.BlockSpec((tm, d), lambda i, f: (i, 0)),
                  pl.BlockSpec((d, tf), lambda i, f: (0, f)),
                  pl.BlockSpec((d, tf), lambda i, f: (0, f)),
                  pl.BlockSpec((tf, d), lambda i, f: (f, 0)),
                  _full(g.shape), _full(b.shape)],
        out_specs=pl.BlockSpec((tm, d), lambda i, f: (i, 0)),
        out_shape=jax.ShapeDtypeStruct((n, d), F32),
        scratch_shapes=[pltpu.VMEM((tm, d), F32)],
        compiler_params=_cparams(("arbitrary", "arbitrary")),
        name="swiglu_ln",
    )(x, w1, w3, w2, g, b)


def _swiglu_experts_kernel(be_ref, nu_ref, src_ref, x_hbm, w1_ref, w3_ref, w2_ref, o_ref,
                           acc_ref, xbuf_ref, xb_ref, sem):
    tm = xb_ref.shape[0]
    i = pl.program_id(0)
    f = pl.program_id(1)

    @pl.when(i < nu_ref[0])
    def _():
        @pl.when(f == 0)
        def _():
            slot = i % 2

            @pl.when(i == 0)
            def _():
                _start_row_gather(x_hbm, src_ref, 0, 1, tm, xbuf_ref.at[0], sem.at[0])

            _wait_row_gather(x_hbm, tm, xbuf_ref.at[slot], sem.at[slot])
            for s in range(ROW_SLABS):
                xb_ref[:, s * LANES:(s + 1) * LANES] = _slab(xbuf_ref.at[slot], s, tm).astype(BF16)

            @pl.when(i + 1 < nu_ref[0])
            def _():
                _start_row_gather(x_hbm, src_ref, (i + 1) * tm, 1, tm, xbuf_ref.at[1 - slot], sem.at[1 - slot])

        _swiglu_partial(xb_ref[...], w1_ref, w3_ref, w2_ref, acc_ref)

        @pl.when(f == pl.num_programs(1) - 1)
        def _():
            _to_slabs(o_ref, acc_ref[...])

    @pl.when(i >= nu_ref[0])
    def _():
        o_ref[...] = jnp.zeros(o_ref.shape, o_ref.dtype)


def swiglu_experts(block_e, n_used, src, x3, w1, w3, w2, n_rows, tm, tf=1792):
    d = D_MODEL
    nf = D_FF // tf

    def f_idx(i, f, nu):
        return jnp.where(i < nu[0], f, nf - 1)

    grid_spec = pltpu.PrefetchScalarGridSpec(
        num_scalar_prefetch=3,
        grid=(n_rows // tm, nf),
        in_specs=[pl.BlockSpec(memory_space=pl.ANY),
                  pl.BlockSpec((None, d, tf), lambda i, f, be, nu, src: (be[i], 0, f_idx(i, f, nu))),
                  pl.BlockSpec((None, d, tf), lambda i, f, be, nu, src: (be[i], 0, f_idx(i, f, nu))),
                  pl.BlockSpec((None, tf, d), lambda i, f, be, nu, src: (be[i], f_idx(i, f, nu), 0))],
        out_specs=pl.BlockSpec((tm * ROW_SLABS, LANES), lambda i, f, be, nu, src: (i, 0)),
        scratch_shapes=[pltpu.VMEM((tm, d), F32),
                        pltpu.VMEM((2, tm * ROW_SLABS, LANES), F32),
                        pltpu.VMEM((tm, d), BF16),
                        pltpu.SemaphoreType.DMA((2,))],
    )
    return pl.pallas_call(
        _swiglu_experts_kernel,
        grid_spec=grid_spec,
        out_shape=jax.ShapeDtypeStruct((n_rows * ROW_SLABS, LANES), F32),
        compiler_params=_cparams(("arbitrary", "arbitrary")),
        name="swiglu_experts",
    )(block_e, n_used, src, x3, w1, w3, w2)


def _mm_kernel(x_ref, w_ref, o_ref):
    o_ref[...] = jnp.dot(x_ref[...].astype(BF16), w_ref[...], preferred_element_type=F32).astype(o_ref.dtype)


def matmul(x, w, out_dtype, tm, tn):
    n, k = x.shape
    _, m = w.shape
    return pl.pallas_call(
        _mm_kernel,
        grid=(n // tm, m // tn),
        in_specs=[pl.BlockSpec((tm, k), lambda i, j: (i, 0)),
                  pl.BlockSpec((k, tn), lambda i, j: (0, j))],
        out_specs=pl.BlockSpec((tm, tn), lambda i, j: (i, j)),
        out_shape=jax.ShapeDtypeStruct((n, m), out_dtype),
        compiler_params=_cparams(("arbitrary", "arbitrary")),
        name="matmul",
    )(x, w)


def _top2(logits):
    mx = jnp.max(logits, axis=-1, keepdims=True)
    ex = jnp.exp(logits - mx)
    p = ex / jnp.sum(ex, axis=-1, keepdims=True)
    lane = lax.broadcasted_iota(jnp.int32, p.shape, 1)
    p1 = jnp.max(p, axis=-1, keepdims=True)
    i1 = jnp.min(jnp.where(p == p1, lane, N_EXPERTS), axis=-1, keepdims=True)
    rest = jnp.where(lane == i1, -1.0, p)
    p2 = jnp.max(rest, axis=-1, keepdims=True)
    i2 = jnp.min(jnp.where(rest == p2, lane, N_EXPERTS), axis=-1, keepdims=True)
    tot = p1 + p2
    return jnp.concatenate([i1, i2], axis=-1), jnp.concatenate([p1 / tot, p2 / tot], axis=-1)


def _mm_ln_router_kernel(a_ref, x_ref, a_tail_ref, x_tail_ref, w_ref, g_ref, b_ref, wr_ref,
                         o3_ref, idx_ref, val_ref, *, n_main):
    def block(a_blk, x_blk):
        m = jnp.dot(a_blk[...], w_ref[...], preferred_element_type=F32)
        y = _layer_norm(DEEPNORM_ALPHA * x_blk[...] + m, g_ref[...], b_ref[...])
        _to_slabs(o3_ref, y)
        logits = _dot_f32x3(y, wr_ref[...])
        idx_ref[...], val_ref[...] = _top2(logits)

    @pl.when(pl.program_id(0) < n_main)
    def _():
        block(a_ref, x_ref)

    @pl.when(pl.program_id(0) >= n_main)
    def _():
        block(a_tail_ref, x_tail_ref)


def matmul_ln_router(a, x, a_tail, x_tail, w, g, b, wr):
    tm, k = a_tail.shape
    n = a.shape[0]
    d = w.shape[1]
    n_main = n // tm
    n_all = n + tm

    def main(i):
        return (jnp.minimum(i, n_main - 1), 0)

    return pl.pallas_call(
        functools.partial(_mm_ln_router_kernel, n_main=n_main),
        grid=(n_main + 1,),
        in_specs=[pl.BlockSpec((tm, k), main), pl.BlockSpec((tm, d), main),
                  _full(a_tail.shape), _full(x_tail.shape),
                  _full(w.shape), _full(g.shape), _full(b.shape), _full(wr.shape)],
        out_specs=[pl.BlockSpec((tm * ROW_SLABS, LANES), lambda i: (i, 0)),
                   pl.BlockSpec((tm, TOP_K), lambda i: (i, 0)), pl.BlockSpec((tm, TOP_K), lambda i: (i, 0))],
        out_shape=[jax.ShapeDtypeStruct((n_all * ROW_SLABS, LANES), F32),
                   jax.ShapeDtypeStruct((n_all, TOP_K), jnp.int32), jax.ShapeDtypeStruct((n_all, TOP_K), F32)],
        compiler_params=_cparams(("arbitrary",)),
        name="matmul_ln_router",
    )(a, x, a_tail, x_tail, w, g, b, wr)


def _l2norm_heads(x, n_heads, scale):
    outs = []
    for h in range(n_heads):
        xh = x[:, h * GD_DH:(h + 1) * GD_DH]
        ss = jnp.sum(xh * xh, axis=-1, keepdims=True)
        outs.append(xh * (lax.rsqrt(ss + RMS_EPS) * scale))
    return outs


def _softplus(x):
    return jnp.maximum(x, 0.0) + jnp.log1p(jnp.exp(-jnp.abs(x)))


def _gated_rmsnorm(o, z, norm_g):
    ms = jnp.mean(o * o, axis=-1, keepdims=True)
    return o * lax.rsqrt(ms + RMS_EPS) * norm_g * _silu(z)


PAIR = GD_HV // GD_HK
PAIR_SQ = PAIR * CHUNK
PAIR_DV = PAIR * GD_DH


def _sq_cols(x, p):
    lane = lax.broadcasted_iota(jnp.int32, (CHUNK, PAIR_SQ), 1)
    a = jnp.broadcast_to(x[:, PAIR * p:PAIR * p + 1], (CHUNK, PAIR_SQ))
    b = jnp.broadcast_to(x[:, PAIR * p + 1:PAIR * p + 2], (CHUNK, PAIR_SQ))
    return jnp.where(lane < CHUNK, a, b)


def _block_diag_sq(m):
    lane = lax.broadcasted_iota(jnp.int32, m.shape, 1)
    top = jnp.where(lane < CHUNK, m, 0.0).astype(BF16)
    bot = jnp.where(lane < CHUNK, 0.0, m).astype(BF16)
    return jnp.concatenate([top, bot], axis=0)


def _block_diag_dv(m):
    mb = m.astype(BF16)
    zero = jnp.zeros((CHUNK, GD_DH), BF16)
    top = jnp.concatenate([mb[:, :GD_DH], zero], axis=1)
    bot = jnp.concatenate([zero, mb[:, GD_DH:]], axis=1)
    return jnp.concatenate([top, bot], axis=0)


def _tri_inverse_pairs(a_list):
    row = lax.broadcasted_iota(jnp.int32, (CHUNK, PAIR_SQ), 0)
    col = lax.broadcasted_iota(jnp.int32, (CHUNK, PAIR_SQ), 1) % CHUNK
    eye = jnp.where(row == col, 1.0, 0.0).astype(F32)
    s = 1
    inv = None
    while s < CHUNK:
        sub = ((row // (2 * s)) == (col // (2 * s))) & ((row // s) % 2 == 1) & ((col // s) % 2 == 0)
        e_list = [jnp.where(sub, a, 0.0) for a in a_list]
        if s == 1:
            inv = [eye - e for e in e_list]
        else:
            x_list = [jnp.dot(e.astype(BF16), _block_diag_sq(d), preferred_element_type=F32)
                      for e, d in zip(e_list, inv)]
            y_list = [jnp.dot(d.astype(BF16), _block_diag_sq(x), preferred_element_type=F32)
                      for d, x in zip(inv, x_list)]
            inv = [d - y for d, y in zip(inv, y_list)]
        s *= 2
    return inv


def _gdn_chunk_kernel(qkv_ref, z_ref, ba_ref, bat_ref, cw_ref, alog_ref, dtb_ref, alogt_ref, dtbt_ref,
                      ng_ref, carry_ref, s0_ref, o_ref, tail_ref, sout_ref,
                      buf_ref, s_ref, q_ref, k_ref, v_ref, lhs_w_ref, lhs_o_ref, glast_ref,
                      *, valid_rows):
    tc = qkv_ref.shape[0]
    t = pl.program_id(1)

    @pl.when(t == 0)
    def _():
        buf_ref[0:HALO, :] = carry_ref[...]
        for h in range(GD_HV):
            j = h % PAIR
            s_ref[h // PAIR, :, j * GD_DH:(j + 1) * GD_DH] = s0_ref[h]

    buf_ref[HALO:HALO + tc, :] = qkv_ref[...]
    col_chunk = 512
    for j in range(GD_CONV_CH // col_chunk):
        lo, hi = j * col_chunk, (j + 1) * col_chunk
        y = cw_ref[GD_CONV - 1:GD_CONV, lo:hi] * buf_ref[HALO:HALO + tc, lo:hi]
        for i in range(GD_CONV - 1):
            off = HALO - (GD_CONV - 1) + i
            y = y + cw_ref[i:i + 1, lo:hi] * buf_ref[off:off + tc, lo:hi]
        act = _silu(y)
        heads_per = col_chunk // GD_DH
        if lo < GD_K_DIM:
            parts = _l2norm_heads(act, heads_per, GD_DH ** -0.5)
            for p, val in enumerate(parts):
                q_ref[:, lo + p * GD_DH:lo + (p + 1) * GD_DH] = val.astype(BF16)
        elif lo < 2 * GD_K_DIM:
            parts = _l2norm_heads(act, heads_per, 1.0)
            for p, val in enumerate(parts):
                c0 = lo - GD_K_DIM + p * GD_DH
                k_ref[:, c0:c0 + GD_DH] = val.astype(BF16)
        else:
            v_ref[:, lo - 2 * GD_K_DIM:hi - 2 * GD_K_DIM] = act
    tail = buf_ref[tc:tc + HALO, :]
    tail_ref[...] = tail
    buf_ref[0:HALO, :] = tail

    row = lax.broadcasted_iota(jnp.int32, (CHUNK, CHUNK), 0)
    col = lax.broadcasted_iota(jnp.int32, (CHUNK, CHUNK), 1)
    tril_ones = jnp.where(row >= col, 1.0, 0.0).astype(BF16)
    triu_ones = jnp.where(row <= col, 1.0, 0.0).astype(BF16)
    row2 = lax.broadcasted_iota(jnp.int32, (CHUNK, PAIR_SQ), 0)
    col2 = lax.broadcasted_iota(jnp.int32, (CHUNK, PAIR_SQ), 1) % CHUNK
    incl2 = row2 >= col2
    strict2 = row2 > col2
    n_pairs = GD_HK

    eye2 = jnp.where(row2 == col2, 1.0, 0.0).astype(F32)
    n_chunks = tc // CHUNK

    def pair_rows(xt, p):
        return jnp.concatenate([xt[PAIR * p + j:PAIR * p + j + 1, :] for j in range(PAIR)], axis=1)

    items = [(c, p) for c in range(n_chunks) for p in range(n_pairs)]
    cums, cumts, betas, betats = [], [], [], []
    for c in range(n_chunks):
        ba = ba_ref[c]
        bat = bat_ref[c]
        beta = jax.nn.sigmoid(ba[:, :GD_HV])
        betat = jax.nn.sigmoid(bat[:GD_HV, :])
        g = -jnp.exp(alog_ref[...]) * _softplus(ba[:, GD_HV:] + dtb_ref[...])
        gt = -jnp.exp(alogt_ref[...]) * _softplus(bat[GD_HV:, :] + dtbt_ref[...])
        if valid_rows is not None:
            rid = lax.broadcasted_iota(jnp.int32, (CHUNK, GD_HV), 0) + c * CHUNK + t * tc
            beta = jnp.where(rid < valid_rows, beta, 0.0)
            g = jnp.where(rid < valid_rows, g, 0.0)
            cid = lax.broadcasted_iota(jnp.int32, (GD_HV, CHUNK), 1) + c * CHUNK + t * tc
            betat = jnp.where(cid < valid_rows, betat, 0.0)
            gt = jnp.where(cid < valid_rows, gt, 0.0)
        cum = sum(jnp.dot(tril_ones, piece, preferred_element_type=F32) for piece in _bf16_pieces(g))
        cumt = sum(jnp.dot(piece, triu_ones, preferred_element_type=F32) for piece in _bf16_pieces(gt))
        glast_ref[c] = jnp.concatenate(
            [jnp.broadcast_to(jnp.exp(cum[CHUNK - 1:CHUNK, h:h + 1]), (1, GD_DH)) for h in range(GD_HV)], axis=1)
        cums.append(cum)
        cumts.append(cumt)
        betas.append(beta)
        betats.append(betat)

    grams = []
    for c, p in items:
        kb = k_ref[c * CHUNK:(c + 1) * CHUNK, p * GD_DH:(p + 1) * GD_DH]
        qb = q_ref[c * CHUNK:(c + 1) * CHUNK, p * GD_DH:(p + 1) * GD_DH]
        grams.append(lax.dot_general(jnp.concatenate([kb, qb], axis=0), jnp.concatenate([kb, kb], axis=0),
                                     (((1,), (1,)), ((), ())), preferred_element_type=F32))
    a_list, attn_list = [], []
    for (c, p), gram in zip(items, grams):
        diff = _sq_cols(cums[c], p) - pair_rows(cumts[c], p)
        dec = jnp.where(incl2, jnp.exp(jnp.where(incl2, diff, 0.0)), 0.0)
        attn_list.append(gram[CHUNK:] * dec)
        a_list.append(jnp.where(strict2, dec * gram[:CHUNK], 0.0) * _sq_cols(betas[c], p))
    inv_list = _tri_inverse_pairs(a_list)
    for (c, p), inv, attn in zip(items, inv_list, attn_list):
        cum_row = pair_rows(cumts[c], p)
        last = jnp.concatenate([jnp.broadcast_to(cumts[c][PAIR * p + j:PAIR * p + j + 1, CHUNK - 1:CHUNK], (1, CHUNK))
                                for j in range(PAIR)], axis=1)
        ecum_row = jnp.exp(cum_row)
        t_beta = inv * pair_rows(betats[c], p)
        lhs_w_ref[c, p] = jnp.concatenate([t_beta, -(t_beta * ecum_row)], axis=1).astype(BF16)
        top = jnp.concatenate([attn, eye2 * ecum_row], axis=1)
        bot = jnp.concatenate([eye2 * jnp.exp(last - cum_row), jnp.zeros((CHUNK, PAIR_SQ), F32)], axis=1)
        lhs_o_ref[c, p] = jnp.concatenate([top, bot], axis=0).astype(BF16)

    def state_body(c, carry):
        r0 = pl.multiple_of(c * CHUNK, CHUNK)
        kbs, s_old, kqs = [], [], []
        for p in range(n_pairs):
            kb = k_ref[pl.ds(r0, CHUNK), p * GD_DH:(p + 1) * GD_DH]
            qb = q_ref[pl.ds(r0, CHUNK), p * GD_DH:(p + 1) * GD_DH]
            s2 = s_ref[p]
            kbs.append(kb)
            s_old.append(s2)
            kqs.append(jnp.dot(jnp.concatenate([kb, qb], axis=0), s2.astype(BF16), preferred_element_type=F32))
        ws = []
        for p in range(n_pairs):
            v2 = v_ref[pl.ds(r0, CHUNK), p * PAIR_DV:(p + 1) * PAIR_DV]
            rhs = jnp.concatenate([_block_diag_dv(v2), _block_diag_dv(kqs[p][:CHUNK])], axis=0)
            ws.append(jnp.dot(lhs_w_ref[c, p], rhs, preferred_element_type=F32))
        for p in range(n_pairs):
            rhs = jnp.concatenate([_block_diag_dv(ws[p]), _block_diag_dv(kqs[p][CHUNK:])], axis=0)
            ow = jnp.dot(lhs_o_ref[c, p], rhs, preferred_element_type=F32)
            o2 = ow[:CHUNK]
            upd = lax.dot_general(kbs[p], ow[CHUNK:].astype(BF16), (((0,), (0,)), ((), ())),
                                  preferred_element_type=F32)
            s_ref[p] = glast_ref[c, :, p * PAIR_DV:(p + 1) * PAIR_DV] * s_old[p] + upd
            for j in range(PAIR):
                h = PAIR * p + j
                z = z_ref[pl.ds(r0, CHUNK), h * GD_DH:(h + 1) * GD_DH].astype(F32)
                o_ref[pl.ds(r0, CHUNK), h * GD_DH:(h + 1) * GD_DH] = _gated_rmsnorm(
                    o2[:, j * GD_DH:(j + 1) * GD_DH], z, ng_ref[...]).astype(o_ref.dtype)
        return carry

    lax.fori_loop(0, tc // CHUNK, state_body, 0)
    for h in range(GD_HV):
        j = h % PAIR
        sout_ref[h] = s_ref[h // PAIR, :, j * GD_DH:(j + 1) * GD_DH]


def gdn_chunked(qkv, z, ba, bat, cw, a_log, dt_bias, norm_g, carry, s0, tc, valid_rows=None):
    bsz, seq, _ = qkv.shape
    nchunk = tc // CHUNK
    ba4 = ba.reshape(bsz, seq // CHUNK, CHUNK, 2 * GD_HV)
    bat4 = bat.reshape(bsz, 2 * GD_HV, seq // CHUNK, CHUNK).transpose(0, 2, 1, 3)
    alog = a_log.reshape(1, GD_HV)
    dtb = dt_bias.reshape(1, GD_HV)
    alogt = a_log.reshape(GD_HV, 1)
    dtbt = dt_bias.reshape(GD_HV, 1)
    ng = norm_g.reshape(1, GD_DH)
    return pl.pallas_call(
        functools.partial(_gdn_chunk_kernel, valid_rows=valid_rows),
        grid=(bsz, seq // tc),
        in_specs=[pl.BlockSpec((None, tc, GD_CONV_CH), lambda i, t: (i, t, 0)),
                  pl.BlockSpec((None, tc, GD_V_DIM), lambda i, t: (i, t, 0)),
                  pl.BlockSpec((None, nchunk, CHUNK, 2 * GD_HV), lambda i, t: (i, t, 0, 0)),
                  pl.BlockSpec((None, nchunk, 2 * GD_HV, CHUNK), lambda i, t: (i, t, 0, 0)),
                  _full(cw.shape), _full(alog.shape), _full(dtb.shape), _full(alogt.shape), _full(dtbt.shape),
                  _full(ng.shape), _full(carry.shape), _full(s0.shape)],
        out_specs=[pl.BlockSpec((None, tc, GD_V_DIM), lambda i, t: (i, t, 0)),
                   pl.BlockSpec((None, HALO, GD_CONV_CH), lambda i, t: (i, 0, 0)),
                   pl.BlockSpec((None, GD_HV, GD_DH, GD_DH), lambda i, t: (i, 0, 0, 0))],
        out_shape=[jax.ShapeDtypeStruct((bsz, seq, GD_V_DIM), BF16),
                   jax.ShapeDtypeStruct((bsz, HALO, GD_CONV_CH), F32),
                   jax.ShapeDtypeStruct((bsz, GD_HV, GD_DH, GD_DH), F32)],
        scratch_shapes=[pltpu.VMEM((tc + HALO, GD_CONV_CH), F32),
                        pltpu.VMEM((GD_HK, GD_DH, PAIR_DV), F32),
                        pltpu.VMEM((tc, GD_K_DIM), BF16), pltpu.VMEM((tc, GD_K_DIM), BF16),
                        pltpu.VMEM((tc, GD_V_DIM), F32),
                        pltpu.VMEM((nchunk, GD_HK, CHUNK, 2 * PAIR_SQ), BF16),
                        pltpu.VMEM((nchunk, GD_HK, 2 * CHUNK, 2 * PAIR_SQ), BF16),
                        pltpu.VMEM((nchunk, 1, GD_V_DIM), F32)],
        compiler_params=_cparams(("arbitrary", "arbitrary")),
        name="gdn_chunked",
    )(qkv, z, ba4, bat4, cw, alog, dtb, alogt, dtbt, ng, carry, s0)


def _gdn_step_kernel(cur_ref, st_ref, z_ref, ba_ref, cw_ref, alog_ref, dtb_ref, ng_ref, s_ref,
                     o_ref, sout_ref, oacc_ref):
    ns = cur_ref.shape[0]
    y = cw_ref[GD_CONV - 1:GD_CONV, :] * cur_ref[...]
    for i in range(GD_CONV - 1):
        y = y + cw_ref[i:i + 1, :] * st_ref[:, i, :]
    act = _silu(y)
    qh = _l2norm_heads(act[:, :GD_K_DIM], GD_HK, GD_DH ** -0.5)
    kh = _l2norm_heads(act[:, GD_K_DIM:2 * GD_K_DIM], GD_HK, 1.0)
    qk_t = jnp.concatenate(qh + kh, axis=0).T
    ba = ba_ref[...]
    beta = jax.nn.sigmoid(ba[:, :GD_HV])
    eg = jnp.exp(-jnp.exp(alog_ref[...]) * _softplus(ba[:, GD_HV:] + dtb_ref[...]))
    for s in range(ns):
        for h in range(GD_HV):
            g = h // (GD_HV // GD_HK)
            qcol = qk_t[:, g * ns + s:g * ns + s + 1]
            kcol = qk_t[:, (GD_HK + g) * ns + s:(GD_HK + g) * ns + s + 1]
            sd = s_ref[s, h] * eg[s:s + 1, h:h + 1]
            ks = jnp.sum(sd * kcol, axis=0, keepdims=True)
            v = act[s:s + 1, 2 * GD_K_DIM + h * GD_DH:2 * GD_K_DIM + (h + 1) * GD_DH]
            w = beta[s:s + 1, h:h + 1] * (v - ks)
            sn = sd + kcol * w
            sout_ref[s, h] = sn
            oacc_ref[s:s + 1, h * GD_DH:(h + 1) * GD_DH] = jnp.sum(sn * qcol, axis=0, keepdims=True)
    for h in range(GD_HV):
        lo, hi = h * GD_DH, (h + 1) * GD_DH
        o_ref[:, lo:hi] = _gated_rmsnorm(oacc_ref[:, lo:hi], z_ref[:, lo:hi].astype(F32),
                                         ng_ref[...]).astype(o_ref.dtype)


def gdn_step(cur, st, z, ba, cw, a_log, dt_bias, norm_g, s0, ns=8):
    n = cur.shape[0]
    alog = a_log.reshape(1, GD_HV)
    dtb = dt_bias.reshape(1, GD_HV)
    ng = norm_g.reshape(1, GD_DH)
    return pl.pallas_call(
        _gdn_step_kernel,
        grid=(n // ns,),
        in_specs=[pl.BlockSpec((ns, GD_CONV_CH), lambda i: (i, 0)),
                  pl.BlockSpec((ns, GD_CONV - 1, GD_CONV_CH), lambda i: (i, 0, 0)),
                  pl.BlockSpec((ns, GD_V_DIM), lambda i: (i, 0)),
                  pl.BlockSpec((ns, 2 * GD_HV), lambda i: (i, 0)),
                  _full(cw.shape), _full(alog.shape), _full(dtb.shape), _full(ng.shape),
                  pl.BlockSpec((ns, GD_HV, GD_DH, GD_DH), lambda i: (i, 0, 0, 0))],
        out_specs=[pl.BlockSpec((ns, GD_V_DIM), lambda i: (i, 0)),
                   pl.BlockSpec((ns, GD_HV, GD_DH, GD_DH), lambda i: (i, 0, 0, 0))],
        out_shape=[jax.ShapeDtypeStruct((n, GD_V_DIM), BF16),
                   jax.ShapeDtypeStruct((n, GD_HV, GD_DH, GD_DH), F32)],
        scratch_shapes=[pltpu.VMEM((ns, GD_V_DIM), F32)],
        compiler_params=_cparams(("arbitrary",)),
        name="gdn_step",
    )(cur, st, z, ba, cw, alog, dtb, ng, s0)


def _row_copy(src_hbm, row, dst, slot, sem):
    return pltpu.make_async_copy(src_hbm.at[pl.ds(pl.multiple_of(row * ROW_SLABS, ROW_SLABS), ROW_SLABS)],
                                 dst.at[pl.ds(pl.multiple_of(slot * ROW_SLABS, ROW_SLABS), ROW_SLABS)], sem)


def _start_row_gather(src_hbm, idx_ref, base, stride, n, dst, sem, both_queues=False):
    def issue(blk, c):
        r0 = blk * DMA_UNROLL
        for u in range(DMA_UNROLL):
            _row_copy(src_hbm, idx_ref[base + stride * (r0 + u)], dst, r0 + u, sem).start(
                priority=u % 2 if both_queues else 0)
        return c

    lax.fori_loop(0, n // DMA_UNROLL, issue, 0)


def _wait_row_gather(src_hbm, n, dst, sem):
    pltpu.make_async_copy(src_hbm.at[pl.ds(0, n * ROW_SLABS)], dst, sem).wait()


def _combine_kernel(dest_ref, x3_ref, pv_ref, g_ref, b_ref, y_hbm, o_ref, buf_ref, sem, *, first_block):
    tt = o_ref.shape[0]
    i = pl.program_id(0)
    slot = i % 2

    def start(blk, into):
        base = (blk + first_block) * tt * TOP_K
        for k in range(TOP_K):
            _start_row_gather(y_hbm, dest_ref, base + k, TOP_K, tt, buf_ref.at[into, k], sem.at[into, k],
                              both_queues=True)

    @pl.when(i == 0)
    def _():
        start(0, 0)

    @pl.when(i + 1 < pl.num_programs(0))
    def _():
        start(i + 1, 1 - slot)

    for k in range(TOP_K):
        _wait_row_gather(y_hbm, tt, buf_ref.at[slot, k], sem.at[slot, k])
    pv = pv_ref[...]
    slabs = []
    for s in range(ROW_SLABS):
        y = pv[:, 0:1] * _slab(buf_ref.at[slot, 0], s, tt) + pv[:, 1:2] * _slab(buf_ref.at[slot, 1], s, tt)
        slabs.append(DEEPNORM_ALPHA * _slab(x3_ref, s, tt) + y)
    mu = sum(jnp.sum(r, axis=-1, keepdims=True) for r in slabs) * (1.0 / D_MODEL)
    var = sum(jnp.sum((r - mu) * (r - mu), axis=-1, keepdims=True) for r in slabs) * (1.0 / D_MODEL)
    inv = lax.rsqrt(var + LN_EPS)
    for s in range(ROW_SLABS):
        cs = slice(s * LANES, (s + 1) * LANES)
        o_ref[:, cs] = (slabs[s] - mu) * inv * g_ref[:, cs] + b_ref[:, cs]


def combine(dest, x3, pv, g, b, yb3, tt, first_block, n_blocks):
    grid_spec = pltpu.PrefetchScalarGridSpec(
        num_scalar_prefetch=1,
        grid=(n_blocks,),
        in_specs=[pl.BlockSpec((tt * ROW_SLABS, LANES), lambda i, dst: (i + first_block, 0)),
                  pl.BlockSpec((tt, TOP_K), lambda i, dst: (i + first_block, 0)),
                  pl.BlockSpec(g.shape, lambda i, dst: (0, 0)),
                  pl.BlockSpec(b.shape, lambda i, dst: (0, 0)),
                  pl.BlockSpec(memory_space=pl.ANY)],
        out_specs=pl.BlockSpec((tt, D_MODEL), lambda i, dst: (i, 0)),
        scratch_shapes=[pltpu.VMEM((2, TOP_K, tt * ROW_SLABS, LANES), F32), pltpu.SemaphoreType.DMA((2, TOP_K))],
    )
    return pl.pallas_call(
        functools.partial(_combine_kernel, first_block=first_block),
        grid_spec=grid_spec,
        out_shape=jax.ShapeDtypeStruct((n_blocks * tt, D_MODEL), F32),
        compiler_params=_cparams(("arbitrary",)),
        name="moe_combine",
    )(dest, x3, pv, g, b, yb3)


def moe_experts(x3, topi, w1, w3, w2, tm):
    n = topi.shape[0]
    eid = topi.reshape(-1)
    onehot = (eid[:, None] == jnp.arange(N_EXPERTS, dtype=jnp.int32)[None, :]).astype(jnp.int32)
    csum = jnp.cumsum(onehot, axis=0)
    rank = jnp.sum((csum - onehot) * onehot, axis=1)
    counts = csum[-1]
    padded = (counts + tm - 1) // tm * tm
    pends = jnp.cumsum(padded)
    pstarts = pends - padded
    dest = (pstarts[eid] + rank).astype(jnp.int32)
    nb = (n * TOP_K + tm - 1) // tm + N_EXPERTS
    n_rows = nb * tm
    tok = jnp.arange(n * TOP_K, dtype=jnp.int32) // TOP_K
    src = jnp.zeros((n_rows,), jnp.int32).at[dest].set(tok, unique_indices=True)
    block_start = jnp.arange(nb, dtype=jnp.int32) * tm
    block_e = jnp.minimum(jnp.searchsorted(pends, block_start, side="right"), N_EXPERTS - 1).astype(jnp.int32)
    n_used = (pends[-1] // tm).astype(jnp.int32).reshape(1)

    yb3 = swiglu_experts(block_e, n_used, src, x3, w1, w3, w2, n_rows, tm)
    return yb3, dest


def kernel(x_prompt, x_sample, state_conv_a, state_conv_b, state_delta, meta_tokens, ln_g, ln_b, sc_w_in, sc_conv, sc_w_out, ffn_w1, ffn_w3, ffn_w2, gd_w_in, gd_conv, gd_a_log, gd_dt_bias, gd_norm_g, gd_w_out, moe_router, moe_w1, moe_w3, moe_w2):
    bsz, seq, d = x_prompt.shape
    n_s = x_sample.shape[0]
    n_small = N_META + n_s
    n_p = bsz * seq

    def row(v):
        return v.reshape(1, -1)

    sc_w_in_b = sc_w_in[0].astype(BF16)
    sc_w_out_b = sc_w_out[0].astype(BF16)
    ffn_w1_b, ffn_w3_b, ffn_w2_b = ffn_w1.astype(BF16), ffn_w3.astype(BF16), ffn_w2.astype(BF16)
    gd_w_qkv_b = gd_w_in[0][:, :GD_CONV_CH].astype(BF16)
    gd_w_z_b = gd_w_in[0][:, GD_CONV_CH:GD_CONV_CH + GD_V_DIM].astype(BF16)
    gd_w_ba_b = gd_w_in[0][:, GD_CONV_CH + GD_V_DIM:].astype(BF16)
    gd_w_out_b = gd_w_out[0].astype(BF16)
    moe_w1_b, moe_w3_b, moe_w2_b = moe_w1[0].astype(BF16), moe_w3[0].astype(BF16), moe_w2[0].astype(BF16)

    x_small = jnp.concatenate([meta_tokens.astype(F32), x_sample.reshape(n_s, d)], axis=0)

    xa_s, ch_s = l0_mix_small(x_small, state_conv_a[0, :, 0], state_conv_a[0, :, 1], sc_w_in_b, sc_w_out_b,
                              sc_conv[0], row(ln_g[0, 0]), row(ln_b[0, 0]))
    xa_p, tail_a = l0_mix_prompt(x_prompt, sc_w_in_b, sc_w_out_b, sc_conv[0], row(ln_g[0, 0]), row(ln_b[0, 0]),
                                 ch_s[N_META - HALO:N_META])
    def dense_ffn(x, tm):
        return swiglu_ln(x, ffn_w1_b[0], ffn_w3_b[0], ffn_w2_b[0], row(ln_g[0, 1]), row(ln_b[0, 1]), tm)

    xb_s = dense_ffn(xa_s, n_small)
    xb_p = dense_ffn(xa_p.reshape(n_p, d), 512)

    def gdn_inproj(x, tm):
        qkv = matmul(x, gd_w_qkv_b, F32, tm, 1024)
        z = matmul(x, gd_w_z_b, BF16, tm, 1024)
        ba = matmul(x, gd_w_ba_b, F32, tm, 2 * GD_HV)
        return qkv, z, ba

    qkv_s, z_s, ba_s = gdn_inproj(xb_s, n_small)
    qkv_p, z_p, ba_p = gdn_inproj(xb_p, 1024)

    pad = CHUNK - N_META

    def meta_pad(a):
        return jnp.pad(a[:N_META], ((0, pad), (0, 0)))[None]

    ba_m = meta_pad(ba_s)
    o_m, _, s_meta = gdn_chunked(meta_pad(qkv_s), meta_pad(z_s), ba_m, jnp.swapaxes(ba_m, 1, 2), gd_conv[0],
                                 gd_a_log[0], gd_dt_bias[0], gd_norm_g[0],
                                 jnp.zeros((HALO, GD_CONV_CH), F32), jnp.zeros((GD_HV, GD_DH, GD_DH), F32),
                                 CHUNK, valid_rows=N_META)
    o_smp, s_smp = gdn_step(qkv_s[N_META:], state_conv_b[0], z_s[N_META:], ba_s[N_META:], gd_conv[0],
                            gd_a_log[0], gd_dt_bias[0], gd_norm_g[0], state_delta[0])
    ba_p3 = ba_p.reshape(bsz, seq, 2 * GD_HV)
    o_p, tail_b, s_p = gdn_chunked(qkv_p.reshape(bsz, seq, GD_CONV_CH), z_p.reshape(bsz, seq, GD_V_DIM), ba_p3,
                                   jnp.swapaxes(ba_p3, 1, 2), gd_conv[0], gd_a_log[0], gd_dt_bias[0],
                                   gd_norm_g[0], qkv_s[N_META - HALO:N_META], s_meta[0], 256)
    tt = TOKEN_BLOCK
    small_pad = ((0, tt - n_small), (0, 0))
    o_s = jnp.pad(jnp.concatenate([o_m[0, :N_META], o_smp], axis=0), small_pad)
    xc3, topi, topv = matmul_ln_router(o_p.reshape(n_p, GD_V_DIM), xb_p, o_s, jnp.pad(xb_s, small_pad),
                                       gd_w_out_b, row(ln_g[1, 0]), row(ln_b[1, 0]), moe_router[0])
    yb3, dest = moe_experts(xc3, topi, moe_w1_b, moe_w3_b, moe_w2_b, EXPERT_BLOCK)
    xd_p = combine(dest, xc3, topv, row(ln_g[1, 1]), row(ln_b[1, 1]), yb3, tt, 0, n_p // tt)
    xd_s = combine(dest, xc3, topv, row(ln_g[1, 1]), row(ln_b[1, 1]), yb3, tt, n_p // tt, 1)

    y_prompt = xd_p.reshape(bsz, seq, d)
    y_sample = xd_s[N_META:n_small].reshape(n_s, 1, d)
    new_conv_a_prompt = tail_a[None, :, HALO - (SC_WIDTH - 1):]
    new_conv_b_prompt = tail_b[None, :, HALO - (GD_CONV - 1):]
    new_delta_prompt = s_p[None]
    new_conv_a_sample = jnp.stack([state_conv_a[0, :, 1], ch_s[N_META:]], axis=1)[None]
    new_conv_b_sample = jnp.concatenate([state_conv_b[0, :, 1:], qkv_s[N_META:, None]], axis=1)[None]
    new_delta_sample = s_smp[None]
    return (y_prompt, y_sample, new_conv_a_prompt, new_conv_b_prompt, new_delta_prompt,
            new_conv_a_sample, new_conv_b_sample, new_delta_sample)
```

```python
import functools

import jax
import jax.numpy as jnp
from jax import lax
from jax.experimental import pallas as pl
from jax.experimental.pallas import tpu as pltpu

F32 = jnp.float32
BF16 = jnp.bfloat16

D_MODEL = 1024
N_META = 16
SC_WIDTH = 3
GD_HK = 8
GD_HV = 16
GD_DH = 128
GD_K_DIM = GD_HK * GD_DH
GD_V_DIM = GD_HV * GD_DH
GD_CONV = 4
GD_CONV_CH = 2 * GD_K_DIM + GD_V_DIM
D_FF = 3584
N_EXPERTS = 8
TOP_K = 2
LN_EPS = 1e-5
RMS_EPS = 1e-6
DEPTH = 2
DEEPNORM_ALPHA = (2 * DEPTH) ** 0.25

LANES = 128
ROW_SLABS = D_MODEL // LANES
DMA_UNROLL = 8
EXPERT_F_STEPS = 2
TOKEN_BLOCK = 256
EXPERT_BLOCK = 512
CHUNK = 64
HALO = 8
V7X_VMEM_LIMIT = 56 * 1024 * 1024


def _cparams(sem, vmem=V7X_VMEM_LIMIT):
    return pltpu.CompilerParams(dimension_semantics=sem, vmem_limit_bytes=vmem)


def _bdot(a, b):
    return jnp.dot(a.astype(BF16), b.astype(BF16), preferred_element_type=F32)


def _bdot_nt(a, b):
    return lax.dot_general(a.astype(BF16), b.astype(BF16), (((1,), (1,)), ((), ())),
                           preferred_element_type=F32)


def _bdot_tn(a, b):
    return lax.dot_general(a.astype(BF16), b.astype(BF16), (((0,), (0,)), ((), ())),
                           preferred_element_type=F32)


def _bf16_pieces(x):
    p0 = x.astype(BF16)
    r1 = x - p0.astype(F32)
    p1 = r1.astype(BF16)
    p2 = (r1 - p1.astype(F32)).astype(BF16)
    return p0, p1, p2


def _dot_f32x3(a, b):
    a_hi = a.astype(BF16)
    b_hi = b.astype(BF16)
    a_lo = (a - a_hi.astype(F32)).astype(BF16)
    b_lo = (b - b_hi.astype(F32)).astype(BF16)
    return (jnp.dot(a_hi, b_hi, preferred_element_type=F32) + jnp.dot(a_lo, b_hi, preferred_element_type=F32)
            + jnp.dot(a_hi, b_lo, preferred_element_type=F32))


def _layer_norm(r, g, b):
    mu = jnp.mean(r, axis=-1, keepdims=True)
    d = r - mu
    var = jnp.mean(d * d, axis=-1, keepdims=True)
    return d * lax.rsqrt(var + LN_EPS) * g + b


def _silu(x):
    return x * jax.nn.sigmoid(x)


def _l0_inproj(xb, w_in_ref, ch_ref, bg_ref, row0, rows):
    col_chunk = 512
    for j in range(D_MODEL // col_chunk):
        lo, hi = j * col_chunk, (j + 1) * col_chunk
        bg = jnp.dot(xb, w_in_ref[:, lo:hi], preferred_element_type=F32)
        c = jnp.dot(xb, w_in_ref[:, D_MODEL + lo:D_MODEL + hi], preferred_element_type=F32)
        h = jnp.dot(xb, w_in_ref[:, 2 * D_MODEL + lo:2 * D_MODEL + hi], preferred_element_type=F32)
        ch_ref[row0:row0 + rows, lo:hi] = c * h
        bg_ref[:, lo:hi] = bg


def _l0_prompt_kernel(x_ref, w_in_ref, w_out_ref, cw_ref, g_ref, b_ref, carry_ref,
                      o_ref, tail_ref, buf_ref, u_ref):
    tm = x_ref.shape[0]
    t = pl.program_id(1)
    n_slabs = D_MODEL // LANES

    @pl.when(t == 0)
    def _():
        for c in range(n_slabs):
            buf_ref[c, 0:HALO, :] = carry_ref[:, c * LANES:(c + 1) * LANES]

    x = x_ref[...]
    xb = x.astype(BF16)
    col_chunk = 512
    for j in range(D_MODEL // col_chunk):
        lo, hi = j * col_chunk, (j + 1) * col_chunk
        bg = jnp.dot(xb, w_in_ref[:, lo:hi], preferred_element_type=F32)
        cg = jnp.dot(xb, w_in_ref[:, D_MODEL + lo:D_MODEL + hi], preferred_element_type=F32)
        h = jnp.dot(xb, w_in_ref[:, 2 * D_MODEL + lo:2 * D_MODEL + hi], preferred_element_type=F32)
        ch = cg * h
        for s in range(col_chunk // LANES):
            c = j * (col_chunk // LANES) + s
            cs = slice(c * LANES, (c + 1) * LANES)
            cur = ch[:, s * LANES:(s + 1) * LANES]
            buf_ref[c, HALO:HALO + tm, :] = cur
            y = (cw_ref[0:1, cs] * buf_ref[c, HALO - 2:HALO - 2 + tm, :]
                 + cw_ref[1:2, cs] * buf_ref[c, HALO - 1:HALO - 1 + tm, :]
                 + cw_ref[2:3, cs] * cur)
            u_ref[:, cs] = (bg[:, s * LANES:(s + 1) * LANES] * y).astype(BF16)
            tail = buf_ref[c, tm:tm + HALO, :]
            tail_ref[:, cs] = tail
            buf_ref[c, 0:HALO, :] = tail
    m = jnp.dot(u_ref[...], w_out_ref[...], preferred_element_type=F32)
    o_ref[...] = _layer_norm(DEEPNORM_ALPHA * x + m, g_ref[...], b_ref[...])


def _l0_small_kernel(x_ref, st0_ref, st1_ref, w_in_ref, w_out_ref, cw_ref, g_ref, b_ref,
                     o_ref, ch_out_ref, buf_ref, bg_ref):
    n = x_ref.shape[0]
    x = x_ref[...]
    buf_ref[0:HALO, :] = jnp.zeros((HALO, D_MODEL), F32)
    _l0_inproj(x.astype(BF16), w_in_ref, buf_ref, bg_ref, HALO, n)
    y_meta = (cw_ref[0:1, :] * buf_ref[HALO - 2:HALO - 2 + N_META, :]
              + cw_ref[1:2, :] * buf_ref[HALO - 1:HALO - 1 + N_META, :]
              + cw_ref[2:3, :] * buf_ref[HALO:HALO + N_META, :])
    ch = buf_ref[HALO:HALO + n, :]
    y_s = (cw_ref[0:1, :] * st0_ref[...] + cw_ref[1:2, :] * st1_ref[...]
           + cw_ref[2:3, :] * ch[N_META:, :])
    y = jnp.concatenate([y_meta, y_s], axis=0)
    u = (bg_ref[...] * y).astype(BF16)
    m = jnp.dot(u, w_out_ref[...], preferred_element_type=F32)
    o_ref[...] = _layer_norm(DEEPNORM_ALPHA * x + m, g_ref[...], b_ref[...])
    ch_out_ref[...] = ch


def _full(shape):
    nd = len(shape)
    return pl.BlockSpec(shape, lambda *_: (0,) * nd)


def l0_mix_prompt(x, w_in, w_out, cw, g, b, carry, tm=512):
    bsz, seq, d = x.shape
    return pl.pallas_call(
        _l0_prompt_kernel,
        grid=(bsz, seq // tm),
        in_specs=[pl.BlockSpec((None, tm, d), lambda i, t: (i, t, 0)),
                  _full(w_in.shape), _full(w_out.shape), _full(cw.shape), _full(g.shape), _full(b.shape),
                  _full(carry.shape)],
        out_specs=[pl.BlockSpec((None, tm, d), lambda i, t: (i, t, 0)),
                   pl.BlockSpec((None, HALO, d), lambda i, t: (i, 0, 0))],
        out_shape=[jax.ShapeDtypeStruct((bsz, seq, d), F32),
                   jax.ShapeDtypeStruct((bsz, HALO, d), F32)],
        scratch_shapes=[pltpu.VMEM((d // LANES, tm + HALO, LANES), F32), pltpu.VMEM((tm, d), BF16)],
        compiler_params=_cparams(("arbitrary", "arbitrary")),
        name="l0_mix_prompt",
    )(x, w_in, w_out, cw, g, b, carry)


def l0_mix_small(x, st0, st1, w_in, w_out, cw, g, b):
    n, d = x.shape
    return pl.pallas_call(
        _l0_small_kernel,
        grid=(1,),
        in_specs=[_full(a.shape) for a in (x, st0, st1, w_in, w_out, cw, g, b)],
        out_specs=[_full((n, d)), _full((n, d))],
        out_shape=[jax.ShapeDtypeStruct((n, d), F32), jax.ShapeDtypeStruct((n, d), F32)],
        scratch_shapes=[pltpu.VMEM((n + HALO, d), F32), pltpu.VMEM((n, d), F32)],
        compiler_params=_cparams(("arbitrary",)),
        name="l0_mix_small",
    )(x, st0, st1, w_in, w_out, cw, g, b)


def _slab(ref, s, rows):
    return ref[pl.ds(s, rows, stride=ROW_SLABS), :]


def _to_slabs(o_ref, val):
    rows = val.shape[0]
    for s in range(ROW_SLABS):
        o_ref[pl.ds(s, rows, stride=ROW_SLABS), :] = val[:, s * LANES:(s + 1) * LANES]


def _swiglu_partial(xb, w1_ref, w3_ref, w2_ref, acc_ref):
    a = jnp.dot(xb, w1_ref[...], preferred_element_type=F32)
    b = jnp.dot(xb, w3_ref[...], preferred_element_type=F32)
    h = (_silu(a) * b).astype(BF16)
    part = jnp.dot(h, w2_ref[...], preferred_element_type=F32)
    acc_ref[...] = jnp.where(pl.program_id(1) > 0, acc_ref[...], 0.0) + part


def _zero_acc_once(acc_ref):
    @pl.when((pl.program_id(0) == 0) & (pl.program_id(1) == 0))
    def _():
        acc_ref[...] = jnp.zeros(acc_ref.shape, acc_ref.dtype)


def _swiglu_ln_kernel(x_ref, w1_ref, w3_ref, w2_ref, g_ref, b_ref, o_ref, acc_ref):
    _zero_acc_once(acc_ref)
    _swiglu_partial(x_ref[...].astype(BF16), w1_ref, w3_ref, w2_ref, acc_ref)

    @pl.when(pl.program_id(1) == pl.num_programs(1) - 1)
    def _():
        o_ref[...] = _layer_norm(DEEPNORM_ALPHA * x_ref[...] + acc_ref[...], g_ref[...], b_ref[...])


def swiglu_ln(x, w1, w3, w2, g, b, tm, tf=1792):
    n, d = x.shape
    return pl.pallas_call(
        _swiglu_ln_kernel,
        grid=(n // tm, D_FF // tf),
        in_specs=[pl.BlockSpec((tm, d), lambda i, f: (i, 0)),
                  pl.BlockSpec((d, tf), lambda i, f: (0, f)),
                  pl.BlockSpec((d, tf), lambda i, f: (0, f)),
                  pl.BlockSpec((tf, d), lambda i, f: (f, 0)),
                  _full(g.shape), _full(b.shape)],
        out_specs=pl.BlockSpec((tm, d), lambda i, f: (i, 0)),
        out_shape=jax.ShapeDtypeStruct((n, d), F32),
        scratch_shapes=[pltpu.VMEM((tm, d), F32)],
        compiler_params=_cparams(("arbitrary", "arbitrary")),
        name="swiglu_ln",
    )(x, w1, w3, w2, g, b)


def _swiglu_experts_kernel(be_ref, nu_ref, src_ref, x_hbm, w1_ref, w3_ref, w2_ref, o_ref,
                           acc_ref, xbuf_ref, xb_ref, sem):
    tm = xb_ref.shape[0]
    i = pl.program_id(0)
    f = pl.program_id(1)
    _zero_acc_once(acc_ref)

    @pl.when(i < nu_ref[0])
    def _():
        @pl.when(f == 0)
        def _():
            slot = i % 2

            @pl.when(i == 0)
            def _():
                _start_row_gather(x_hbm, src_ref, 0, 1, tm, xbuf_ref.at[0], sem.at[0])

            _wait_row_gather(x_hbm, tm, xbuf_ref.at[slot], sem.at[slot])
            for s in range(ROW_SLABS):
                xb_ref[:, s * LANES:(s + 1) * LANES] = _slab(xbuf_ref.at[slot], s, tm).astype(BF16)

            @pl.when(i + 1 < nu_ref[0])
            def _():
                _start_row_gather(x_hbm, src_ref, (i + 1) * tm, 1, tm, xbuf_ref.at[1 - slot], sem.at[1 - slot])

        _swiglu_partial(xb_ref[...], w1_ref, w3_ref, w2_ref, acc_ref)

        @pl.when(f == pl.num_programs(1) - 1)
        def _():
            _to_slabs(o_ref, acc_ref[...])

    @pl.when(i >= nu_ref[0])
    def _():
        o_ref[...] = jnp.zeros(o_ref.shape, o_ref.dtype)


def swiglu_experts(block_e, n_used, src, x3, w1, w3, w2, n_rows, tm):
    d = D_MODEL
    nf = EXPERT_F_STEPS
    tf = D_FF // nf

    def f_idx(i, f, nu):
        return jnp.where(i < nu[0], f, nf - 1)

    grid_spec = pltpu.PrefetchScalarGridSpec(
        num_scalar_prefetch=3,
        grid=(n_rows // tm, nf),
        in_specs=[pl.BlockSpec(memory_space=pl.ANY),
                  pl.BlockSpec((None, d, tf), lambda i, f, be, nu, src: (be[i], 0, f_idx(i, f, nu))),
                  pl.BlockSpec((None, d, tf), lambda i, f, be, nu, src: (be[i], 0, f_idx(i, f, nu))),
                  pl.BlockSpec((None, tf, d), lambda i, f, be, nu, src: (be[i], f_idx(i, f, nu), 0))],
        out_specs=pl.BlockSpec((tm * ROW_SLABS, LANES), lambda i, f, be, nu, src: (i, 0)),
        scratch_shapes=[pltpu.VMEM((tm, d), F32),
                        pltpu.VMEM((2, tm * ROW_SLABS, LANES), F32),
                        pltpu.VMEM((tm, d), BF16),
                        pltpu.SemaphoreType.DMA((2,))],
    )
    return pl.pallas_call(
        _swiglu_experts_kernel,
        grid_spec=grid_spec,
        out_shape=jax.ShapeDtypeStruct((n_rows * ROW_SLABS, LANES), F32),
        compiler_params=_cparams(("arbitrary", "arbitrary")),
        name="swiglu_experts",
    )(block_e, n_used, src, x3, w1, w3, w2)


def _mm_kernel(x_ref, w_ref, o_ref):
    o_ref[...] = jnp.dot(x_ref[...].astype(BF16), w_ref[...], preferred_element_type=F32).astype(o_ref.dtype)


def matmul(x, w, out_dtype, tm, tn):
    n, k = x.shape
    _, m = w.shape
    return pl.pallas_call(
        _mm_kernel,
        grid=(n // tm, m // tn),
        in_specs=[pl.BlockSpec((tm, k), lambda i, j: (i, 0)),
                  pl.BlockSpec((k, tn), lambda i, j: (0, j))],
        out_specs=pl.BlockSpec((tm, tn), lambda i, j: (i, j)),
        out_shape=jax.ShapeDtypeStruct((n, m), out_dtype),
        compiler_params=_cparams(("arbitrary", "arbitrary")),
        name="matmul",
    )(x, w)


def _top2(logits):
    mx = jnp.max(logits, axis=-1, keepdims=True)
    ex = jnp.exp(logits - mx)
    p = ex / jnp.sum(ex, axis=-1, keepdims=True)
    lane = lax.broadcasted_iota(jnp.int32, p.shape, 1)
    p1 = jnp.max(p, axis=-1, keepdims=True)
    i1 = jnp.min(jnp.where(p == p1, lane, N_EXPERTS), axis=-1, keepdims=True)
    rest = jnp.where(lane == i1, -1.0, p)
    p2 = jnp.max(rest, axis=-1, keepdims=True)
    i2 = jnp.min(jnp.where(rest == p2, lane, N_EXPERTS), axis=-1, keepdims=True)
    tot = p1 + p2
    return jnp.concatenate([i1, i2], axis=-1), jnp.concatenate([p1 / tot, p2 / tot], axis=-1)


def _mm_ln_router_kernel(a_ref, x_ref, a_tail_ref, x_tail_ref, w_ref, g_ref, b_ref, wr_ref,
                         o3_ref, idx_ref, val_ref, *, n_main):
    def block(a_blk, x_blk):
        m = jnp.dot(a_blk[...], w_ref[...], preferred_element_type=F32)
        y = _layer_norm(DEEPNORM_ALPHA * x_blk[...] + m, g_ref[...], b_ref[...])
        _to_slabs(o3_ref, y)
        logits = _dot_f32x3(y, wr_ref[...])
        idx_ref[...], val_ref[...] = _top2(logits)

    @pl.when(pl.program_id(0) < n_main)
    def _():
        block(a_ref, x_ref)

    @pl.when(pl.program_id(0) >= n_main)
    def _():
        block(a_tail_ref, x_tail_ref)


def matmul_ln_router(a, x, a_tail, x_tail, w, g, b, wr):
    tm, k = a_tail.shape
    n = a.shape[0]
    d = w.shape[1]
    n_main = n // tm
    n_all = n + tm

    def main(i):
        return (jnp.minimum(i, n_main - 1), 0)

    return pl.pallas_call(
        functools.partial(_mm_ln_router_kernel, n_main=n_main),
        grid=(n_main + 1,),
        in_specs=[pl.BlockSpec((tm, k), main), pl.BlockSpec((tm, d), main),
                  _full(a_tail.shape), _full(x_tail.shape),
                  _full(w.shape), _full(g.shape), _full(b.shape), _full(wr.shape)],
        out_specs=[pl.BlockSpec((tm * ROW_SLABS, LANES), lambda i: (i, 0)),
                   pl.BlockSpec((tm, TOP_K), lambda i: (i, 0)), pl.BlockSpec((tm, TOP_K), lambda i: (i, 0))],
        out_shape=[jax.ShapeDtypeStruct((n_all * ROW_SLABS, LANES), F32),
                   jax.ShapeDtypeStruct((n_all, TOP_K), jnp.int32), jax.ShapeDtypeStruct((n_all, TOP_K), F32)],
        compiler_params=_cparams(("arbitrary",)),
        name="matmul_ln_router",
    )(a, x, a_tail, x_tail, w, g, b, wr)


def _l2norm_heads(x, n_heads, scale):
    outs = []
    for h in range(n_heads):
        xh = x[:, h * GD_DH:(h + 1) * GD_DH]
        ss = jnp.sum(xh * xh, axis=-1, keepdims=True)
        outs.append(xh * (lax.rsqrt(ss + RMS_EPS) * scale))
    return outs


def _softplus(x):
    return jnp.maximum(x, 0.0) + jnp.log1p(jnp.exp(-jnp.abs(x)))


def _gated_rmsnorm(o, z, norm_g):
    ms = jnp.mean(o * o, axis=-1, keepdims=True)
    return o * lax.rsqrt(ms + RMS_EPS) * norm_g * _silu(z)


PAIR = GD_HV // GD_HK
PAIR_SQ = PAIR * CHUNK
PAIR_DV = PAIR * GD_DH


def _sq_cols(x, p):
    lane = lax.broadcasted_iota(jnp.int32, (CHUNK, PAIR_SQ), 1)
    a = jnp.broadcast_to(x[:, PAIR * p:PAIR * p + 1], (CHUNK, PAIR_SQ))
    b = jnp.broadcast_to(x[:, PAIR * p + 1:PAIR * p + 2], (CHUNK, PAIR_SQ))
    return jnp.where(lane < CHUNK, a, b)


def _block_diag_sq(m):
    lane = lax.broadcasted_iota(jnp.int32, m.shape, 1)
    top = jnp.where(lane < CHUNK, m, 0.0).astype(BF16)
    bot = jnp.where(lane < CHUNK, 0.0, m).astype(BF16)
    return jnp.concatenate([top, bot], axis=0)


def _block_diag_dv(m):
    mb = m.astype(BF16)
    zero = jnp.zeros((CHUNK, GD_DH), BF16)
    top = jnp.concatenate([mb[:, :GD_DH], zero], axis=1)
    bot = jnp.concatenate([zero, mb[:, GD_DH:]], axis=1)
    return jnp.concatenate([top, bot], axis=0)


def _tri_inverse_pairs(a_list):
    row = lax.broadcasted_iota(jnp.int32, (CHUNK, PAIR_SQ), 0)
    col = lax.broadcasted_iota(jnp.int32, (CHUNK, PAIR_SQ), 1) % CHUNK
    eye = jnp.where(row == col, 1.0, 0.0).astype(F32)
    s = 1
    inv = None
    while s < CHUNK:
        sub = ((row // (2 * s)) == (col // (2 * s))) & ((row // s) % 2 == 1) & ((col // s) % 2 == 0)
        e_list = [jnp.where(sub, a, 0.0) for a in a_list]
        if s == 1:
            inv = [eye - e for e in e_list]
        else:
            x_list = [jnp.dot(e.astype(BF16), _block_diag_sq(d), preferred_element_type=F32)
                      for e, d in zip(e_list, inv)]
            y_list = [jnp.dot(d.astype(BF16), _block_diag_sq(x), preferred_element_type=F32)
                      for d, x in zip(inv, x_list)]
            inv = [d - y for d, y in zip(inv, y_list)]
        s *= 2
    return inv


def _gdn_chunk_kernel(qkv_ref, z_ref, ba_ref, bat_ref, cw_ref, alog_ref, dtb_ref, alogt_ref, dtbt_ref,
                      ng_ref, carry_ref, s0_ref, o_ref, tail_ref, sout_ref,
                      buf_ref, s_ref, q_ref, k_ref, v_ref, lhs_w_ref, lhs_o_ref, glast_ref,
                      *, valid_rows):
    tc = qkv_ref.shape[0]
    t = pl.program_id(1)

    @pl.when(t == 0)
    def _():
        for c in range(GD_CONV_CH // LANES):
            buf_ref[c, 0:HALO, :] = carry_ref[:, c * LANES:(c + 1) * LANES]
        for h in range(GD_HV):
            j = h % PAIR
            s_ref[h // PAIR, :, j * GD_DH:(j + 1) * GD_DH] = s0_ref[h]

    for c in range(GD_CONV_CH // LANES):
        lo, hi = c * LANES, (c + 1) * LANES
        buf_ref[c, HALO:HALO + tc, :] = qkv_ref[:, lo:hi]
        y = cw_ref[GD_CONV - 1:GD_CONV, lo:hi] * qkv_ref[:, lo:hi]
        for i in range(GD_CONV - 1):
            off = HALO - (GD_CONV - 1) + i
            y = y + cw_ref[i:i + 1, lo:hi] * buf_ref[c, off:off + tc, :]
        act = _silu(y)
        if lo < GD_K_DIM:
            q_ref[:, lo:hi] = _l2norm_heads(act, 1, GD_DH ** -0.5)[0].astype(BF16)
        elif lo < 2 * GD_K_DIM:
            k_ref[:, lo - GD_K_DIM:hi - GD_K_DIM] = _l2norm_heads(act, 1, 1.0)[0].astype(BF16)
        else:
            v_ref[:, lo - 2 * GD_K_DIM:hi - 2 * GD_K_DIM] = act
        tail = buf_ref[c, tc:tc + HALO, :]
        tail_ref[:, lo:hi] = tail
        buf_ref[c, 0:HALO, :] = tail

    row = lax.broadcasted_iota(jnp.int32, (CHUNK, CHUNK), 0)
    col = lax.broadcasted_iota(jnp.int32, (CHUNK, CHUNK), 1)
    tril_ones = jnp.where(row >= col, 1.0, 0.0).astype(BF16)
    triu_ones = jnp.where(row <= col, 1.0, 0.0).astype(BF16)
    row2 = lax.broadcasted_iota(jnp.int32, (CHUNK, PAIR_SQ), 0)
    col2 = lax.broadcasted_iota(jnp.int32, (CHUNK, PAIR_SQ), 1) % CHUNK
    incl2 = row2 >= col2
    strict2 = row2 > col2
    n_pairs = GD_HK

    eye2 = jnp.where(row2 == col2, 1.0, 0.0).astype(F32)
    n_chunks = tc // CHUNK

    def pair_rows(xt, p):
        return jnp.concatenate([xt[PAIR * p + j:PAIR * p + j + 1, :] for j in range(PAIR)], axis=1)

    items = [(c, p) for c in range(n_chunks) for p in range(n_pairs)]
    cums, cumts, betas, betats = [], [], [], []
    for c in range(n_chunks):
        ba = ba_ref[c]
        bat = bat_ref[c]
        beta = jax.nn.sigmoid(ba[:, :GD_HV])
        betat = jax.nn.sigmoid(bat[:GD_HV, :])
        g = -jnp.exp(alog_ref[...]) * _softplus(ba[:, GD_HV:] + dtb_ref[...])
        gt = -jnp.exp(alogt_ref[...]) * _softplus(bat[GD_HV:, :] + dtbt_ref[...])
        if valid_rows is not None:
            rid = lax.broadcasted_iota(jnp.int32, (CHUNK, GD_HV), 0) + c * CHUNK + t * tc
            beta = jnp.where(rid < valid_rows, beta, 0.0)
            g = jnp.where(rid < valid_rows, g, 0.0)
            cid = lax.broadcasted_iota(jnp.int32, (GD_HV, CHUNK), 1) + c * CHUNK + t * tc
            betat = jnp.where(cid < valid_rows, betat, 0.0)
            gt = jnp.where(cid < valid_rows, gt, 0.0)
        cum = sum(jnp.dot(tril_ones, piece, preferred_element_type=F32) for piece in _bf16_pieces(g))
        cumt = sum(jnp.dot(piece, triu_ones, preferred_element_type=F32) for piece in _bf16_pieces(gt))
        glast_ref[c] = jnp.concatenate(
            [jnp.broadcast_to(jnp.exp(cum[CHUNK - 1:CHUNK, h:h + 1]), (1, GD_DH)) for h in range(GD_HV)], axis=1)
        cums.append(cum)
        cumts.append(cumt)
        betas.append(beta)
        betats.append(betat)

    grams = []
    for c, p in items:
        kb = k_ref[c * CHUNK:(c + 1) * CHUNK, p * GD_DH:(p + 1) * GD_DH]
        qb = q_ref[c * CHUNK:(c + 1) * CHUNK, p * GD_DH:(p + 1) * GD_DH]
        grams.append(lax.dot_general(jnp.concatenate([kb, qb], axis=0), jnp.concatenate([kb, kb], axis=0),
                                     (((1,), (1,)), ((), ())), preferred_element_type=F32))
    a_list, attn_list = [], []
    for (c, p), gram in zip(items, grams):
        diff = _sq_cols(cums[c], p) - pair_rows(cumts[c], p)
        dec = jnp.where(incl2, jnp.exp(jnp.where(incl2, diff, 0.0)), 0.0)
        attn_list.append(gram[CHUNK:] * dec)
        a_list.append(jnp.where(strict2, dec * gram[:CHUNK], 0.0) * _sq_cols(betas[c], p))
    inv_list = _tri_inverse_pairs(a_list)
    for (c, p), inv, attn in zip(items, inv_list, attn_list):
        cum_row = pair_rows(cumts[c], p)
        last = jnp.concatenate([jnp.broadcast_to(cumts[c][PAIR * p + j:PAIR * p + j + 1, CHUNK - 1:CHUNK], (1, CHUNK))
                                for j in range(PAIR)], axis=1)
        ecum_row = jnp.exp(cum_row)
        t_beta = inv * pair_rows(betats[c], p)
        lhs_w_ref[c, p] = jnp.concatenate([t_beta, -(t_beta * ecum_row)], axis=1).astype(BF16)
        top = jnp.concatenate([attn, eye2 * ecum_row], axis=1)
        bot = jnp.concatenate([eye2 * jnp.exp(last - cum_row), jnp.zeros((CHUNK, PAIR_SQ), F32)], axis=1)
        lhs_o_ref[c, p] = jnp.concatenate([top, bot], axis=0).astype(BF16)

    def state_body(c, carry):
        r0 = pl.multiple_of(c * CHUNK, CHUNK)
        kbs, s_old, kqs = [], [], []
        for p in range(n_pairs):
            kb = k_ref[pl.ds(r0, CHUNK), p * GD_DH:(p + 1) * GD_DH]
            qb = q_ref[pl.ds(r0, CHUNK), p * GD_DH:(p + 1) * GD_DH]
            s2 = s_ref[p]
            kbs.append(kb)
            s_old.append(s2)
            kqs.append(jnp.dot(jnp.concatenate([kb, qb], axis=0), s2.astype(BF16), preferred_element_type=F32))
        ws = []
        for p in range(n_pairs):
            v2 = v_ref[pl.ds(r0, CHUNK), p * PAIR_DV:(p + 1) * PAIR_DV]
            rhs = jnp.concatenate([_block_diag_dv(v2), _block_diag_dv(kqs[p][:CHUNK])], axis=0)
            ws.append(jnp.dot(lhs_w_ref[c, p], rhs, preferred_element_type=F32))
        for p in range(n_pairs):
            rhs = jnp.concatenate([_block_diag_dv(ws[p]), _block_diag_dv(kqs[p][CHUNK:])], axis=0)
            ow = jnp.dot(lhs_o_ref[c, p], rhs, preferred_element_type=F32)
            o2 = ow[:CHUNK]
            upd = lax.dot_general(kbs[p], ow[CHUNK:].astype(BF16), (((0,), (0,)), ((), ())),
                                  preferred_element_type=F32)
            s_ref[p] = glast_ref[c, :, p * PAIR_DV:(p + 1) * PAIR_DV] * s_old[p] + upd
            for j in range(PAIR):
                h = PAIR * p + j
                z = z_ref[pl.ds(r0, CHUNK), h * GD_DH:(h + 1) * GD_DH].astype(F32)
                o_ref[pl.ds(r0, CHUNK), h * GD_DH:(h + 1) * GD_DH] = _gated_rmsnorm(
                    o2[:, j * GD_DH:(j + 1) * GD_DH], z, ng_ref[...]).astype(o_ref.dtype)
        return carry

    lax.fori_loop(0, tc // CHUNK, state_body, 0)
    for h in range(GD_HV):
        j = h % PAIR
        sout_ref[h] = s_ref[h // PAIR, :, j * GD_DH:(j + 1) * GD_DH]


def gdn_chunked(qkv, z, ba, bat, cw, a_log, dt_bias, norm_g, carry, s0, tc, valid_rows=None):
    bsz, seq, _ = qkv.shape
    nchunk = tc // CHUNK
    ba4 = ba.reshape(bsz, seq // CHUNK, CHUNK, 2 * GD_HV)
    bat4 = bat.reshape(bsz, 2 * GD_HV, seq // CHUNK, CHUNK).transpose(0, 2, 1, 3)
    alog = a_log.reshape(1, GD_HV)
    dtb = dt_bias.reshape(1, GD_HV)
    alogt = a_log.reshape(GD_HV, 1)
    dtbt = dt_bias.reshape(GD_HV, 1)
    ng = norm_g.reshape(1, GD_DH)
    return pl.pallas_call(
        functools.partial(_gdn_chunk_kernel, valid_rows=valid_rows),
        grid=(bsz, seq // tc),
        in_specs=[pl.BlockSpec((None, tc, GD_CONV_CH), lambda i, t: (i, t, 0)),
                  pl.BlockSpec((None, tc, GD_V_DIM), lambda i, t: (i, t, 0)),
                  pl.BlockSpec((None, nchunk, CHUNK, 2 * GD_HV), lambda i, t: (i, t, 0, 0)),
                  pl.BlockSpec((None, nchunk, 2 * GD_HV, CHUNK), lambda i, t: (i, t, 0, 0)),
                  _full(cw.shape), _full(alog.shape), _full(dtb.shape), _full(alogt.shape), _full(dtbt.shape),
                  _full(ng.shape), _full(carry.shape), _full(s0.shape)],
        out_specs=[pl.BlockSpec((None, tc, GD_V_DIM), lambda i, t: (i, t, 0)),
                   pl.BlockSpec((None, HALO, GD_CONV_CH), lambda i, t: (i, 0, 0)),
                   pl.BlockSpec((None, GD_HV, GD_DH, GD_DH), lambda i, t: (i, 0, 0, 0))],
        out_shape=[jax.ShapeDtypeStruct((bsz, seq, GD_V_DIM), BF16),
                   jax.ShapeDtypeStruct((bsz, HALO, GD_CONV_CH), F32),
                   jax.ShapeDtypeStruct((bsz, GD_HV, GD_DH, GD_DH), F32)],
        scratch_shapes=[pltpu.VMEM((GD_CONV_CH // LANES, tc + HALO, LANES), F32),
                        pltpu.VMEM((GD_HK, GD_DH, PAIR_DV), F32),
                        pltpu.VMEM((tc, GD_K_DIM), BF16), pltpu.VMEM((tc, GD_K_DIM), BF16),
                        pltpu.VMEM((tc, GD_V_DIM), F32),
                        pltpu.VMEM((nchunk, GD_HK, CHUNK, 2 * PAIR_SQ), BF16),
                        pltpu.VMEM((nchunk, GD_HK, 2 * CHUNK, 2 * PAIR_SQ), BF16),
                        pltpu.VMEM((nchunk, 1, GD_V_DIM), F32)],
        compiler_params=_cparams(("arbitrary", "arbitrary")),
        name="gdn_chunked",
    )(qkv, z, ba4, bat4, cw, alog, dtb, alogt, dtbt, ng, carry, s0)


def _gdn_step_kernel(cur_ref, st_ref, z_ref, ba_ref, cw_ref, alog_ref, dtb_ref, ng_ref, s_ref,
                     o_ref, sout_ref, oacc_ref):
    ns = cur_ref.shape[0]
    y = cw_ref[GD_CONV - 1:GD_CONV, :] * cur_ref[...]
    for i in range(GD_CONV - 1):
        y = y + cw_ref[i:i + 1, :] * st_ref[:, i, :]
    act = _silu(y)
    qh = _l2norm_heads(act[:, :GD_K_DIM], GD_HK, GD_DH ** -0.5)
    kh = _l2norm_heads(act[:, GD_K_DIM:2 * GD_K_DIM], GD_HK, 1.0)
    qk_t = jnp.concatenate(qh + kh, axis=0).T
    ba = ba_ref[...]
    beta = jax.nn.sigmoid(ba[:, :GD_HV])
    eg = jnp.exp(-jnp.exp(alog_ref[...]) * _softplus(ba[:, GD_HV:] + dtb_ref[...]))
    for s in range(ns):
        for h in range(GD_HV):
            g = h // (GD_HV // GD_HK)
            qcol = qk_t[:, g * ns + s:g * ns + s + 1]
            kcol = qk_t[:, (GD_HK + g) * ns + s:(GD_HK + g) * ns + s + 1]
            sd = s_ref[s, h] * eg[s:s + 1, h:h + 1]
            ks = jnp.sum(sd * kcol, axis=0, keepdims=True)
            v = act[s:s + 1, 2 * GD_K_DIM + h * GD_DH:2 * GD_K_DIM + (h + 1) * GD_DH]
            w = beta[s:s + 1, h:h + 1] * (v - ks)
            sn = sd + kcol * w
            sout_ref[s, h] = sn
            oacc_ref[s:s + 1, h * GD_DH:(h + 1) * GD_DH] = jnp.sum(sn * qcol, axis=0, keepdims=True)
    for h in range(GD_HV):
        lo, hi = h * GD_DH, (h + 1) * GD_DH
        o_ref[:, lo:hi] = _gated_rmsnorm(oacc_ref[:, lo:hi], z_ref[:, lo:hi].astype(F32),
                                         ng_ref[...]).astype(o_ref.dtype)


def gdn_step(cur, st, z, ba, cw, a_log, dt_bias, norm_g, s0, ns=8):
    n = cur.shape[0]
    alog = a_log.reshape(1, GD_HV)
    dtb = dt_bias.reshape(1, GD_HV)
    ng = norm_g.reshape(1, GD_DH)
    return pl.pallas_call(
        _gdn_step_kernel,
        grid=(n // ns,),
        in_specs=[pl.BlockSpec((ns, GD_CONV_CH), lambda i: (i, 0)),
                  pl.BlockSpec((ns, GD_CONV - 1, GD_CONV_CH), lambda i: (i, 0, 0)),
                  pl.BlockSpec((ns, GD_V_DIM), lambda i: (i, 0)),
                  pl.BlockSpec((ns, 2 * GD_HV), lambda i: (i, 0)),
                  _full(cw.shape), _full(alog.shape), _full(dtb.shape), _full(ng.shape),
                  pl.BlockSpec((ns, GD_HV, GD_DH, GD_DH), lambda i: (i, 0, 0, 0))],
        out_specs=[pl.BlockSpec((ns, GD_V_DIM), lambda i: (i, 0)),
                   pl.BlockSpec((ns, GD_HV, GD_DH, GD_DH), lambda i: (i, 0, 0, 0))],
        out_shape=[jax.ShapeDtypeStruct((n, GD_V_DIM), BF16),
                   jax.ShapeDtypeStruct((n, GD_HV, GD_DH, GD_DH), F32)],
        scratch_shapes=[pltpu.VMEM((ns, GD_V_DIM), F32)],
        compiler_params=_cparams(("arbitrary",)),
        name="gdn_step",
    )(cur, st, z, ba, cw, alog, dtb, ng, s0)


def _row_copy(src_hbm, row, dst, slot, sem):
    return pltpu.make_async_copy(src_hbm.at[pl.ds(pl.multiple_of(row * ROW_SLABS, ROW_SLABS), ROW_SLABS)],
                                 dst.at[pl.ds(pl.multiple_of(slot * ROW_SLABS, ROW_SLABS), ROW_SLABS)], sem)


def _start_row_gather(src_hbm, idx_ref, base, stride, n, dst, sem, both_queues=False):
    def issue(blk, c):
        r0 = blk * DMA_UNROLL
        for u in range(DMA_UNROLL):
            _row_copy(src_hbm, idx_ref[base + stride * (r0 + u)], dst, r0 + u, sem).start(
                priority=u % 2 if both_queues else 0)
        return c

    lax.fori_loop(0, n // DMA_UNROLL, issue, 0)


def _wait_row_gather(src_hbm, n, dst, sem):
    pltpu.make_async_copy(src_hbm.at[pl.ds(0, n * ROW_SLABS)], dst, sem).wait()


def _combine_kernel(dest_ref, x3_ref, pv_ref, g_ref, b_ref, y_hbm, o_ref, buf_ref, sem, *, first_block):
    tt = o_ref.shape[0]
    i = pl.program_id(0)
    slot = i % 2

    def start(blk, into):
        base = (blk + first_block) * tt * TOP_K
        for k in range(TOP_K):
            _start_row_gather(y_hbm, dest_ref, base + k, TOP_K, tt, buf_ref.at[into, k], sem.at[into, k],
                              both_queues=True)

    @pl.when(i == 0)
    def _():
        start(0, 0)

    @pl.when(i + 1 < pl.num_programs(0))
    def _():
        start(i + 1, 1 - slot)

    for k in range(TOP_K):
        _wait_row_gather(y_hbm, tt, buf_ref.at[slot, k], sem.at[slot, k])
    pv = pv_ref[...]
    slabs = []
    for s in range(ROW_SLABS):
        y = pv[:, 0:1] * _slab(buf_ref.at[slot, 0], s, tt) + pv[:, 1:2] * _slab(buf_ref.at[slot, 1], s, tt)
        slabs.append(DEEPNORM_ALPHA * _slab(x3_ref, s, tt) + y)
    mu = sum(jnp.sum(r, axis=-1, keepdims=True) for r in slabs) * (1.0 / D_MODEL)
    var = sum(jnp.sum((r - mu) * (r - mu), axis=-1, keepdims=True) for r in slabs) * (1.0 / D_MODEL)
    inv = lax.rsqrt(var + LN_EPS)
    for s in range(ROW_SLABS):
        cs = slice(s * LANES, (s + 1) * LANES)
        o_ref[:, cs] = (slabs[s] - mu) * inv * g_ref[:, cs] + b_ref[:, cs]


def combine(dest, x3, pv, g, b, yb3, tt, first_block, n_blocks):
    grid_spec = pltpu.PrefetchScalarGridSpec(
        num_scalar_prefetch=1,
        grid=(n_blocks,),
        in_specs=[pl.BlockSpec((tt * ROW_SLABS, LANES), lambda i, dst: (i + first_block, 0)),
                  pl.BlockSpec((tt, TOP_K), lambda i, dst: (i + first_block, 0)),
                  pl.BlockSpec(g.shape, lambda i, dst: (0, 0)),
                  pl.BlockSpec(b.shape, lambda i, dst: (0, 0)),
                  pl.BlockSpec(memory_space=pl.ANY)],
        out_specs=pl.BlockSpec((tt, D_MODEL), lambda i, dst: (i, 0)),
        scratch_shapes=[pltpu.VMEM((2, TOP_K, tt * ROW_SLABS, LANES), F32), pltpu.SemaphoreType.DMA((2, TOP_K))],
    )
    return pl.pallas_call(
        functools.partial(_combine_kernel, first_block=first_block),
        grid_spec=grid_spec,
        out_shape=jax.ShapeDtypeStruct((n_blocks * tt, D_MODEL), F32),
        compiler_params=_cparams(("arbitrary",)),
        name="moe_combine",
    )(dest, x3, pv, g, b, yb3)


def moe_experts(x3, topi, w1, w3, w2, tm):
    n = topi.shape[0]
    eid = topi.reshape(-1)
    onehot = (eid[:, None] == jnp.arange(N_EXPERTS, dtype=jnp.int32)[None, :]).astype(jnp.int32)
    csum = jnp.cumsum(onehot, axis=0)
    rank = jnp.sum((csum - onehot) * onehot, axis=1)
    counts = csum[-1]
    padded = (counts + tm - 1) // tm * tm
    pends = jnp.cumsum(padded)
    pstarts = pends - padded
    dest = (pstarts[eid] + rank).astype(jnp.int32)
    nb = (n * TOP_K + tm - 1) // tm + N_EXPERTS
    n_rows = nb * tm
    tok = jnp.arange(n * TOP_K, dtype=jnp.int32) // TOP_K
    src = jnp.zeros((n_rows,), jnp.int32).at[dest].set(tok, unique_indices=True)
    block_start = jnp.arange(nb, dtype=jnp.int32) * tm
    block_e = jnp.minimum(jnp.searchsorted(pends, block_start, side="right"), N_EXPERTS - 1).astype(jnp.int32)
    n_used = (pends[-1] // tm).astype(jnp.int32).reshape(1)

    yb3 = swiglu_experts(block_e, n_used, src, x3, w1, w3, w2, n_rows, tm)
    return yb3, dest


def kernel(x_prompt, x_sample, state_conv_a, state_conv_b, state_delta, meta_tokens, ln_g, ln_b, sc_w_in, sc_conv, sc_w_out, ffn_w1, ffn_w3, ffn_w2, gd_w_in, gd_conv, gd_a_log, gd_dt_bias, gd_norm_g, gd_w_out, moe_router, moe_w1, moe_w3, moe_w2):
    bsz, seq, d = x_prompt.shape
    n_s = x_sample.shape[0]
    n_small = N_META + n_s
    n_p = bsz * seq

    def row(v):
        return v.reshape(1, -1)

    sc_w_in_b = sc_w_in[0].astype(BF16)
    sc_w_out_b = sc_w_out[0].astype(BF16)
    ffn_w1_b, ffn_w3_b, ffn_w2_b = ffn_w1.astype(BF16), ffn_w3.astype(BF16), ffn_w2.astype(BF16)
    gd_w_qkv_b = gd_w_in[0][:, :GD_CONV_CH].astype(BF16)
    gd_w_z_b = gd_w_in[0][:, GD_CONV_CH:GD_CONV_CH + GD_V_DIM].astype(BF16)
    gd_w_ba_b = gd_w_in[0][:, GD_CONV_CH + GD_V_DIM:].astype(BF16)
    gd_w_out_b = gd_w_out[0].astype(BF16)
    moe_w1_b, moe_w3_b, moe_w2_b = moe_w1[0].astype(BF16), moe_w3[0].astype(BF16), moe_w2[0].astype(BF16)

    x_small = jnp.concatenate([meta_tokens.astype(F32), x_sample.reshape(n_s, d)], axis=0)

    xa_s, ch_s = l0_mix_small(x_small, state_conv_a[0, :, 0], state_conv_a[0, :, 1], sc_w_in_b, sc_w_out_b,
                              sc_conv[0], row(ln_g[0, 0]), row(ln_b[0, 0]))
    xa_p, tail_a = l0_mix_prompt(x_prompt, sc_w_in_b, sc_w_out_b, sc_conv[0], row(ln_g[0, 0]), row(ln_b[0, 0]),
                                 ch_s[N_META - HALO:N_META])
    def dense_ffn(x, tm):
        return swiglu_ln(x, ffn_w1_b[0], ffn_w3_b[0], ffn_w2_b[0], row(ln_g[0, 1]), row(ln_b[0, 1]), tm)

    xb_s = dense_ffn(xa_s, n_small)
    xb_p = dense_ffn(xa_p.reshape(n_p, d), 512)

    def gdn_inproj(x, tm):
        qkv = matmul(x, gd_w_qkv_b, F32, tm, 2048)
        z = matmul(x, gd_w_z_b, BF16, tm, 2048)
        ba = matmul(x, gd_w_ba_b, F32, tm, 2 * GD_HV)
        return qkv, z, ba

    qkv_s, z_s, ba_s = gdn_inproj(xb_s, n_small)
    qkv_p, z_p, ba_p = gdn_inproj(xb_p, 1024)

    pad = CHUNK - N_META

    def meta_pad(a):
        return jnp.pad(a[:N_META], ((0, pad), (0, 0)))[None]

    ba_m = meta_pad(ba_s)
    o_m, _, s_meta = gdn_chunked(meta_pad(qkv_s), meta_pad(z_s), ba_m, jnp.swapaxes(ba_m, 1, 2), gd_conv[0],
                                 gd_a_log[0], gd_dt_bias[0], gd_norm_g[0],
                                 jnp.zeros((HALO, GD_CONV_CH), F32), jnp.zeros((GD_HV, GD_DH, GD_DH), F32),
                                 CHUNK, valid_rows=N_META)
    o_smp, s_smp = gdn_step(qkv_s[N_META:], state_conv_b[0], z_s[N_META:], ba_s[N_META:], gd_conv[0],
                            gd_a_log[0], gd_dt_bias[0], gd_norm_g[0], state_delta[0])
    ba_p3 = ba_p.reshape(bsz, seq, 2 * GD_HV)
    o_p, tail_b, s_p = gdn_chunked(qkv_p.reshape(bsz, seq, GD_CONV_CH), z_p.reshape(bsz, seq, GD_V_DIM), ba_p3,
                                   jnp.swapaxes(ba_p3, 1, 2), gd_conv[0], gd_a_log[0], gd_dt_bias[0],
                                   gd_norm_g[0], qkv_s[N_META - HALO:N_META], s_meta[0], 256)
    tt = TOKEN_BLOCK
    small_pad = ((0, tt - n_small), (0, 0))
    o_s = jnp.pad(jnp.concatenate([o_m[0, :N_META], o_smp], axis=0), small_pad)
    xc3, topi, topv = matmul_ln_router(o_p.reshape(n_p, GD_V_DIM), xb_p, o_s, jnp.pad(xb_s, small_pad),
                                       gd_w_out_b, row(ln_g[1, 0]), row(ln_b[1, 0]), moe_router[0])
    yb3, dest = moe_experts(xc3, topi, moe_w1_b, moe_w3_b, moe_w2_b, EXPERT_BLOCK)
    xd_p = combine(dest, xc3, topv, row(ln_g[1, 1]), row(ln_b[1, 1]), yb3, tt, 0, n_p // tt)
    xd_s = combine(dest, xc3, topv, row(ln_g[1, 1]), row(ln_b[1, 1]), yb3, tt, n_p // tt, 1)

    y_prompt = xd_p.reshape(bsz, seq, d)
    y_sample = xd_s[N_META:n_small].reshape(n_s, 1, d)
    new_conv_a_prompt = tail_a[None, :, HALO - (SC_WIDTH - 1):]
    new_conv_b_prompt = tail_b[None, :, HALO - (GD_CONV - 1):]
    new_delta_prompt = s_p[None]
    new_conv_a_sample = jnp.stack([state_conv_a[0, :, 1], ch_s[N_META:]], axis=1)[None]
    new_conv_b_sample = jnp.concatenate([state_conv_b[0, :, 1:], qkv_s[N_META:, None]], axis=1)[None]
    new_delta_sample = s_smp[None]
    return (y_prompt, y_sample, new_conv_a_prompt, new_conv_b_prompt, new_delta_prompt,
            new_conv_a_sample, new_conv_b_sample, new_delta_sample)
```

```python
import functools

import jax
import jax.numpy as jnp
from jax import lax
from jax.experimental import pallas as pl
from jax.experimental.pallas import tpu as pltpu

F32 = jnp.float32
BF16 = jnp.bfloat16

D_MODEL = 1024
N_META = 16
SC_WIDTH = 3
GD_HK = 8
GD_HV = 16
GD_DH = 128
GD_K_DIM = GD_HK * GD_DH
GD_V_DIM = GD_HV * GD_DH
GD_CONV = 4
GD_CONV_CH = 2 * GD_K_DIM + GD_V_DIM
D_FF = 3584
N_EXPERTS = 8
TOP_K = 2
LN_EPS = 1e-5
RMS_EPS = 1e-6
DEPTH = 2
DEEPNORM_ALPHA = (2 * DEPTH) ** 0.25

LANES = 128
ROW_SLABS = D_MODEL // LANES
DMA_UNROLL = 8
EXPERT_F_STEPS = 2
TOKEN_CHUNK = 256
CHUNK_RING = 12
SEL_ALIGN = 16
SEL_ROWS = TOKEN_CHUNK + SEL_ALIGN
TOKEN_BLOCK = 256
EXPERT_BLOCK = 512
CHUNK = 64
HALO = 8
V7X_VMEM_LIMIT = 56 * 1024 * 1024


def _cparams(sem, vmem=V7X_VMEM_LIMIT):
    return pltpu.CompilerParams(dimension_semantics=sem, vmem_limit_bytes=vmem)


def _bdot(a, b):
    return jnp.dot(a.astype(BF16), b.astype(BF16), preferred_element_type=F32)


def _bdot_nt(a, b):
    return lax.dot_general(a.astype(BF16), b.astype(BF16), (((1,), (1,)), ((), ())),
                           preferred_element_type=F32)


def _bdot_tn(a, b):
    return lax.dot_general(a.astype(BF16), b.astype(BF16), (((0,), (0,)), ((), ())),
                           preferred_element_type=F32)


def _bf16_pieces(x):
    p0 = x.astype(BF16)
    r1 = x - p0.astype(F32)
    p1 = r1.astype(BF16)
    p2 = (r1 - p1.astype(F32)).astype(BF16)
    return p0, p1, p2


def _dot_f32x3(a, b):
    a_hi = a.astype(BF16)
    b_hi = b.astype(BF16)
    a_lo = (a - a_hi.astype(F32)).astype(BF16)
    b_lo = (b - b_hi.astype(F32)).astype(BF16)
    return (jnp.dot(a_hi, b_hi, preferred_element_type=F32) + jnp.dot(a_lo, b_hi, preferred_element_type=F32)
            + jnp.dot(a_hi, b_lo, preferred_element_type=F32))


def _layer_norm(r, g, b):
    mu = jnp.mean(r, axis=-1, keepdims=True)
    d = r - mu
    var = jnp.mean(d * d, axis=-1, keepdims=True)
    return d * lax.rsqrt(var + LN_EPS) * g + b


def _silu(x):
    return x * jax.nn.sigmoid(x)


def _l0_inproj(xb, w_in_ref, ch_ref, bg_ref, row0, rows):
    col_chunk = 512
    for j in range(D_MODEL // col_chunk):
        lo, hi = j * col_chunk, (j + 1) * col_chunk
        bg = jnp.dot(xb, w_in_ref[:, lo:hi], preferred_element_type=F32)
        c = jnp.dot(xb, w_in_ref[:, D_MODEL + lo:D_MODEL + hi], preferred_element_type=F32)
        h = jnp.dot(xb, w_in_ref[:, 2 * D_MODEL + lo:2 * D_MODEL + hi], preferred_element_type=F32)
        ch_ref[row0:row0 + rows, lo:hi] = c * h
        bg_ref[:, lo:hi] = bg


def _l0_prompt_kernel(x_ref, w_in_ref, w_out_ref, cw_ref, g_ref, b_ref, carry_ref,
                      o_ref, tail_ref, buf_ref, u_ref):
    tm = x_ref.shape[0]
    t = pl.program_id(1)
    n_slabs = D_MODEL // LANES

    @pl.when(t == 0)
    def _():
        for c in range(n_slabs):
            buf_ref[c, 0:HALO, :] = carry_ref[:, c * LANES:(c + 1) * LANES]

    x = x_ref[...]
    xb = x.astype(BF16)
    col_chunk = 512
    for j in range(D_MODEL // col_chunk):
        lo, hi = j * col_chunk, (j + 1) * col_chunk
        bg = jnp.dot(xb, w_in_ref[:, lo:hi], preferred_element_type=F32)
        cg = jnp.dot(xb, w_in_ref[:, D_MODEL + lo:D_MODEL + hi], preferred_element_type=F32)
        h = jnp.dot(xb, w_in_ref[:, 2 * D_MODEL + lo:2 * D_MODEL + hi], preferred_element_type=F32)
        ch = cg * h
        for s in range(col_chunk // LANES):
            c = j * (col_chunk // LANES) + s
            cs = slice(c * LANES, (c + 1) * LANES)
            cur = ch[:, s * LANES:(s + 1) * LANES]
            buf_ref[c, HALO:HALO + tm, :] = cur
            y = (cw_ref[0:1, cs] * buf_ref[c, HALO - 2:HALO - 2 + tm, :]
                 + cw_ref[1:2, cs] * buf_ref[c, HALO - 1:HALO - 1 + tm, :]
                 + cw_ref[2:3, cs] * cur)
            u_ref[:, cs] = (bg[:, s * LANES:(s + 1) * LANES] * y).astype(BF16)
            tail = buf_ref[c, tm:tm + HALO, :]
            tail_ref[:, cs] = tail
            buf_ref[c, 0:HALO, :] = tail
    m = jnp.dot(u_ref[...], w_out_ref[...], preferred_element_type=F32)
    o_ref[...] = _layer_norm(DEEPNORM_ALPHA * x + m, g_ref[...], b_ref[...])


def _l0_small_kernel(x_ref, st0_ref, st1_ref, w_in_ref, w_out_ref, cw_ref, g_ref, b_ref,
                     o_ref, ch_out_ref, buf_ref, bg_ref):
    n = x_ref.shape[0]
    x = x_ref[...]
    buf_ref[0:HALO, :] = jnp.zeros((HALO, D_MODEL), F32)
    _l0_inproj(x.astype(BF16), w_in_ref, buf_ref, bg_ref, HALO, n)
    y_meta = (cw_ref[0:1, :] * buf_ref[HALO - 2:HALO - 2 + N_META, :]
              + cw_ref[1:2, :] * buf_ref[HALO - 1:HALO - 1 + N_META, :]
              + cw_ref[2:3, :] * buf_ref[HALO:HALO + N_META, :])
    ch = buf_ref[HALO:HALO + n, :]
    y_s = (cw_ref[0:1, :] * st0_ref[...] + cw_ref[1:2, :] * st1_ref[...]
           + cw_ref[2:3, :] * ch[N_META:, :])
    y = jnp.concatenate([y_meta, y_s], axis=0)
    u = (bg_ref[...] * y).astype(BF16)
    m = jnp.dot(u, w_out_ref[...], preferred_element_type=F32)
    o_ref[...] = _layer_norm(DEEPNORM_ALPHA * x + m, g_ref[...], b_ref[...])
    ch_out_ref[...] = ch


def _full(shape):
    nd = len(shape)
    return pl.BlockSpec(shape, lambda *_: (0,) * nd)


def l0_mix_prompt(x, w_in, w_out, cw, g, b, carry, tm=512):
    bsz, seq, d = x.shape
    return pl.pallas_call(
        _l0_prompt_kernel,
        grid=(bsz, seq // tm),
        in_specs=[pl.BlockSpec((None, tm, d), lambda i, t: (i, t, 0)),
                  _full(w_in.shape), _full(w_out.shape), _full(cw.shape), _full(g.shape), _full(b.shape),
                  _full(carry.shape)],
        out_specs=[pl.BlockSpec((None, tm, d), lambda i, t: (i, t, 0)),
                   pl.BlockSpec((None, HALO, d), lambda i, t: (i, 0, 0))],
        out_shape=[jax.ShapeDtypeStruct((bsz, seq, d), F32),
                   jax.ShapeDtypeStruct((bsz, HALO, d), F32)],
        scratch_shapes=[pltpu.VMEM((d // LANES, tm + HALO, LANES), F32), pltpu.VMEM((tm, d), BF16)],
        compiler_params=_cparams(("arbitrary", "arbitrary")),
        name="l0_mix_prompt",
    )(x, w_in, w_out, cw, g, b, carry)


def l0_mix_small(x, st0, st1, w_in, w_out, cw, g, b):
    n, d = x.shape
    return pl.pallas_call(
        _l0_small_kernel,
        grid=(1,),
        in_specs=[_full(a.shape) for a in (x, st0, st1, w_in, w_out, cw, g, b)],
        out_specs=[_full((n, d)), _full((n, d))],
        out_shape=[jax.ShapeDtypeStruct((n, d), F32), jax.ShapeDtypeStruct((n, d), F32)],
        scratch_shapes=[pltpu.VMEM((n + HALO, d), F32), pltpu.VMEM((n, d), F32)],
        compiler_params=_cparams(("arbitrary",)),
        name="l0_mix_small",
    )(x, st0, st1, w_in, w_out, cw, g, b)


def _slab(ref, s, rows):
    return ref[pl.ds(s, rows, stride=ROW_SLABS), :]


def _to_slabs(o_ref, val):
    rows = val.shape[0]
    for s in range(ROW_SLABS):
        o_ref[pl.ds(s, rows, stride=ROW_SLABS), :] = val[:, s * LANES:(s + 1) * LANES]


def _swiglu_partial(xb, w1_ref, w3_ref, w2_ref, acc_ref):
    a = jnp.dot(xb, w1_ref[...], preferred_element_type=F32)
    b = jnp.dot(xb, w3_ref[...], preferred_element_type=F32)
    h = (_silu(a) * b).astype(BF16)
    part = jnp.dot(h, w2_ref[...], preferred_element_type=F32)
    acc_ref[...] = jnp.where(pl.program_id(1) > 0, acc_ref[...], 0.0) + part


def _zero_acc_once(acc_ref):
    @pl.when((pl.program_id(0) == 0) & (pl.program_id(1) == 0))
    def _():
        acc_ref[...] = jnp.zeros(acc_ref.shape, acc_ref.dtype)


def _swiglu_ln_kernel(x_ref, w1_ref, w3_ref, w2_ref, g_ref, b_ref, o_ref, acc_ref):
    _zero_acc_once(acc_ref)
    _swiglu_partial(x_ref[...].astype(BF16), w1_ref, w3_ref, w2_ref, acc_ref)

    @pl.when(pl.program_id(1) == pl.num_programs(1) - 1)
    def _():
        o_ref[...] = _layer_norm(DEEPNORM_ALPHA * x_ref[...] + acc_ref[...], g_ref[...], b_ref[...])


def swiglu_ln(x, w1, w3, w2, g, b, tm, tf=1792):
    n, d = x.shape
    return pl.pallas_call(
        _swiglu_ln_kernel,
        grid=(n // tm, D_FF // tf),
        in_specs=[pl.BlockSpec((tm, d), lambda i, f: (i, 0)),
                  pl.BlockSpec((d, tf), lambda i, f: (0, f)),
                  pl.BlockSpec((d, tf), lambda i, f: (0, f)),
                  pl.BlockSpec((tf, d), lambda i, f: (f, 0)),
                  _full(g.shape), _full(b.shape)],
        out_specs=pl.BlockSpec((tm, d), lambda i, f: (i, 0)),
        out_shape=jax.ShapeDtypeStruct((n, d), F32),
        scratch_shapes=[pltpu.VMEM((tm, d), F32)],
        compiler_params=_cparams(("arbitrary", "arbitrary")),
        name="swiglu_ln",
    )(x, w1, w3, w2, g, b)


def _token_chunk_copy(x_hbm, chunk, buf_ref, slot, sem):
    start = pl.multiple_of(chunk * TOKEN_CHUNK, TOKEN_CHUNK)
    return pltpu.make_async_copy(x_hbm.at[pl.ds(start, TOKEN_CHUNK)], buf_ref.at[slot], sem.at[slot])


def _swiglu_experts_kernel(be_ref, nu_ref, c0_ref, cn_ref, e0_ref, cend_ref, x_hbm, dest_ref, w1_ref, w3_ref,
                           w2_ref, o_ref, acc_ref, xsel_ref, xb_ref, chunk_ref, sem):
    tm = xb_ref.shape[0]
    i = pl.program_id(0)
    f = pl.program_id(1)
    nf = pl.num_programs(1)
    n_used = nu_ref[0]
    _zero_acc_once(acc_ref)

    def start_ring(blk):
        for slot in range(CHUNK_RING):
            @pl.when(slot < cn_ref[blk])
            def _():
                _token_chunk_copy(x_hbm, c0_ref[blk] + slot, chunk_ref, slot, sem).start()

    @pl.when(i < n_used)
    def _():
        @pl.when(f == 0)
        def _():
            first = c0_ref[i]
            count = cn_ref[i]
            expert = be_ref[i]
            row_iota = lax.broadcasted_iota(jnp.int32, (SEL_ROWS, TOKEN_CHUNK), 0)

            @pl.when(i == 0)
            def _():
                start_ring(0)

            xsel_ref[...] = jnp.zeros(xsel_ref.shape, xsel_ref.dtype)

            def pick(j, carry):
                slot = jnp.where(j < CHUNK_RING, j, 0)

                @pl.when(j >= CHUNK_RING)
                def _():
                    _token_chunk_copy(x_hbm, first + j, chunk_ref, 0, sem).start()

                chunk = first + j
                before = jnp.where(chunk > 0, cend_ref[jnp.maximum(chunk - 1, 0) * N_EXPERTS + expert], 0)
                row0 = jnp.clip(before - e0_ref[i], 0, tm - SEL_ROWS) // SEL_ALIGN * SEL_ALIGN
                row0 = pl.multiple_of(row0, SEL_ALIGN)
                row_id = row_iota + (i * tm + row0)
                _token_chunk_copy(x_hbm, chunk, chunk_ref, slot, sem).wait()
                lane0 = pl.multiple_of(chunk * TOKEN_CHUNK, TOKEN_CHUNK)
                hit = row_id == dest_ref[0:1, pl.ds(lane0, TOKEN_CHUNK)]
                for k in range(1, TOP_K):
                    hit = hit | (row_id == dest_ref[k:k + 1, pl.ds(lane0, TOKEN_CHUNK)])
                sel = jnp.where(hit, 1.0, 0.0).astype(BF16)
                xsel_ref[pl.ds(row0, SEL_ROWS), :] += jnp.dot(sel, chunk_ref[slot], preferred_element_type=F32)
                return carry

            lax.fori_loop(0, count, pick, 0)
            xb_ref[...] = xsel_ref[...].astype(BF16)

        @pl.when((f == nf - 1) & (i + 1 < n_used))
        def _():
            start_ring(i + 1)

        _swiglu_partial(xb_ref[...], w1_ref, w3_ref, w2_ref, acc_ref)

        @pl.when(f == nf - 1)
        def _():
            _to_slabs(o_ref, acc_ref[...])

    @pl.when(i >= nu_ref[0])
    def _():
        o_ref[...] = jnp.zeros(o_ref.shape, o_ref.dtype)


def swiglu_experts(block_e, n_used, chunk0, n_chunks, entry0, chunk_end, xb, dest_t, w1, w3, w2, n_rows, tm):
    d = D_MODEL
    nf = EXPERT_F_STEPS
    tf = D_FF // nf

    def f_idx(i, f, nu):
        return jnp.where(i < nu[0], f, nf - 1)

    grid_spec = pltpu.PrefetchScalarGridSpec(
        num_scalar_prefetch=6,
        grid=(n_rows // tm, nf),
        in_specs=[pl.BlockSpec(memory_space=pl.ANY),
                  pl.BlockSpec(dest_t.shape, lambda i, f, be, nu, c0, cn, e0, ce: (0, 0)),
                  pl.BlockSpec((None, d, tf), lambda i, f, be, nu, c0, cn, e0, ce: (be[i], 0, f_idx(i, f, nu))),
                  pl.BlockSpec((None, d, tf), lambda i, f, be, nu, c0, cn, e0, ce: (be[i], 0, f_idx(i, f, nu))),
                  pl.BlockSpec((None, tf, d), lambda i, f, be, nu, c0, cn, e0, ce: (be[i], f_idx(i, f, nu), 0))],
        out_specs=pl.BlockSpec((tm * ROW_SLABS, LANES), lambda i, f, be, nu, c0, cn, e0, ce: (i, 0)),
        scratch_shapes=[pltpu.VMEM((tm, d), F32),
                        pltpu.VMEM((tm, d), F32),
                        pltpu.VMEM((tm, d), BF16),
                        pltpu.VMEM((CHUNK_RING, TOKEN_CHUNK, d), BF16),
                        pltpu.SemaphoreType.DMA((CHUNK_RING,))],
    )
    return pl.pallas_call(
        _swiglu_experts_kernel,
        grid_spec=grid_spec,
        out_shape=jax.ShapeDtypeStruct((n_rows * ROW_SLABS, LANES), F32),
        compiler_params=_cparams(("arbitrary", "arbitrary")),
        name="swiglu_experts",
    )(block_e, n_used, chunk0, n_chunks, entry0, chunk_end, xb, dest_t, w1, w3, w2)


def _mm_kernel(x_ref, w_ref, o_ref):
    o_ref[...] = jnp.dot(x_ref[...].astype(BF16), w_ref[...], preferred_element_type=F32).astype(o_ref.dtype)


def matmul(x, w, out_dtype, tm, tn):
    n, k = x.shape
    _, m = w.shape
    return pl.pallas_call(
        _mm_kernel,
        grid=(n // tm, m // tn),
        in_specs=[pl.BlockSpec((tm, k), lambda i, j: (i, 0)),
                  pl.BlockSpec((k, tn), lambda i, j: (0, j))],
        out_specs=pl.BlockSpec((tm, tn), lambda i, j: (i, j)),
        out_shape=jax.ShapeDtypeStruct((n, m), out_dtype),
        compiler_params=_cparams(("arbitrary", "arbitrary")),
        name="matmul",
    )(x, w)


def _top2(logits):
    mx = jnp.max(logits, axis=-1, keepdims=True)
    ex = jnp.exp(logits - mx)
    p = ex / jnp.sum(ex, axis=-1, keepdims=True)
    lane = lax.broadcasted_iota(jnp.int32, p.shape, 1)
    p1 = jnp.max(p, axis=-1, keepdims=True)
    i1 = jnp.min(jnp.where(p == p1, lane, N_EXPERTS), axis=-1, keepdims=True)
    rest = jnp.where(lane == i1, -1.0, p)
    p2 = jnp.max(rest, axis=-1, keepdims=True)
    i2 = jnp.min(jnp.where(rest == p2, lane, N_EXPERTS), axis=-1, keepdims=True)
    tot = p1 + p2
    return jnp.concatenate([i1, i2], axis=-1), jnp.concatenate([p1 / tot, p2 / tot], axis=-1)


def _mm_ln_router_kernel(a_ref, x_ref, a_tail_ref, x_tail_ref, w_ref, g_ref, b_ref, wr_ref,
                         o_ref, ob_ref, idx_ref, val_ref, *, n_main):
    def block(a_blk, x_blk):
        m = jnp.dot(a_blk[...], w_ref[...], preferred_element_type=F32)
        y = _layer_norm(DEEPNORM_ALPHA * x_blk[...] + m, g_ref[...], b_ref[...])
        o_ref[...] = y
        ob_ref[...] = y.astype(BF16)
        logits = _dot_f32x3(y, wr_ref[...])
        idx_ref[...], val_ref[...] = _top2(logits)

    @pl.when(pl.program_id(0) < n_main)
    def _():
        block(a_ref, x_ref)

    @pl.when(pl.program_id(0) >= n_main)
    def _():
        block(a_tail_ref, x_tail_ref)


def matmul_ln_router(a, x, a_tail, x_tail, w, g, b, wr):
    tm, k = a_tail.shape
    n = a.shape[0]
    d = w.shape[1]
    n_main = n // tm
    n_all = n + tm

    def main(i):
        return (jnp.minimum(i, n_main - 1), 0)

    return pl.pallas_call(
        functools.partial(_mm_ln_router_kernel, n_main=n_main),
        grid=(n_main + 1,),
        in_specs=[pl.BlockSpec((tm, k), main), pl.BlockSpec((tm, d), main),
                  _full(a_tail.shape), _full(x_tail.shape),
                  _full(w.shape), _full(g.shape), _full(b.shape), _full(wr.shape)],
        out_specs=[pl.BlockSpec((tm, d), lambda i: (i, 0)), pl.BlockSpec((tm, d), lambda i: (i, 0)),
                   pl.BlockSpec((tm, TOP_K), lambda i: (i, 0)), pl.BlockSpec((tm, TOP_K), lambda i: (i, 0))],
        out_shape=[jax.ShapeDtypeStruct((n_all, d), F32), jax.ShapeDtypeStruct((n_all, d), BF16),
                   jax.ShapeDtypeStruct((n_all, TOP_K), jnp.int32), jax.ShapeDtypeStruct((n_all, TOP_K), F32)],
        compiler_params=_cparams(("arbitrary",)),
        name="matmul_ln_router",
    )(a, x, a_tail, x_tail, w, g, b, wr)


def _l2norm_heads(x, n_heads, scale):
    outs = []
    for h in range(n_heads):
        xh = x[:, h * GD_DH:(h + 1) * GD_DH]
        ss = jnp.sum(xh * xh, axis=-1, keepdims=True)
        outs.append(xh * (lax.rsqrt(ss + RMS_EPS) * scale))
    return outs


def _softplus(x):
    return jnp.maximum(x, 0.0) + jnp.log1p(jnp.exp(-jnp.abs(x)))


def _gated_rmsnorm(o, z, norm_g):
    ms = jnp.mean(o * o, axis=-1, keepdims=True)
    return o * lax.rsqrt(ms + RMS_EPS) * norm_g * _silu(z)


PAIR = GD_HV // GD_HK
PAIR_SQ = PAIR * CHUNK
PAIR_DV = PAIR * GD_DH


def _sq_cols(x, p):
    lane = lax.broadcasted_iota(jnp.int32, (CHUNK, PAIR_SQ), 1)
    a = jnp.broadcast_to(x[:, PAIR * p:PAIR * p + 1], (CHUNK, PAIR_SQ))
    b = jnp.broadcast_to(x[:, PAIR * p + 1:PAIR * p + 2], (CHUNK, PAIR_SQ))
    return jnp.where(lane < CHUNK, a, b)


def _block_diag_sq(m):
    lane = lax.broadcasted_iota(jnp.int32, m.shape, 1)
    top = jnp.where(lane < CHUNK, m, 0.0).astype(BF16)
    bot = jnp.where(lane < CHUNK, 0.0, m).astype(BF16)
    return jnp.concatenate([top, bot], axis=0)


def _block_diag_dv(m):
    mb = m.astype(BF16)
    zero = jnp.zeros((CHUNK, GD_DH), BF16)
    top = jnp.concatenate([mb[:, :GD_DH], zero], axis=1)
    bot = jnp.concatenate([zero, mb[:, GD_DH:]], axis=1)
    return jnp.concatenate([top, bot], axis=0)


def _tri_inverse_pairs(a_list):
    row = lax.broadcasted_iota(jnp.int32, (CHUNK, PAIR_SQ), 0)
    col = lax.broadcasted_iota(jnp.int32, (CHUNK, PAIR_SQ), 1) % CHUNK
    eye = jnp.where(row == col, 1.0, 0.0).astype(F32)
    s = 1
    inv = None
    while s < CHUNK:
        sub = ((row // (2 * s)) == (col // (2 * s))) & ((row // s) % 2 == 1) & ((col // s) % 2 == 0)
        e_list = [jnp.where(sub, a, 0.0) for a in a_list]
        if s == 1:
            inv = [eye - e for e in e_list]
        else:
            x_list = [jnp.dot(e.astype(BF16), _block_diag_sq(d), preferred_element_type=F32)
                      for e, d in zip(e_list, inv)]
            y_list = [jnp.dot(d.astype(BF16), _block_diag_sq(x), preferred_element_type=F32)
                      for d, x in zip(inv, x_list)]
            inv = [d - y for d, y in zip(inv, y_list)]
        s *= 2
    return inv


def _gdn_chunk_kernel(qkv_ref, z_ref, ba_ref, bat_ref, cw_ref, alog_ref, dtb_ref, alogt_ref, dtbt_ref,
                      ng_ref, carry_ref, s0_ref, o_ref, tail_ref, sout_ref,
                      buf_ref, s_ref, q_ref, k_ref, v_ref, lhs_w_ref, lhs_o_ref, glast_ref,
                      *, valid_rows):
    tc = qkv_ref.shape[0]
    t = pl.program_id(1)

    @pl.when(t == 0)
    def _():
        for c in range(GD_CONV_CH // LANES):
            buf_ref[c, 0:HALO, :] = carry_ref[:, c * LANES:(c + 1) * LANES]
        for h in range(GD_HV):
            j = h % PAIR
            s_ref[h // PAIR, :, j * GD_DH:(j + 1) * GD_DH] = s0_ref[h]

    for c in range(GD_CONV_CH // LANES):
        lo, hi = c * LANES, (c + 1) * LANES
        buf_ref[c, HALO:HALO + tc, :] = qkv_ref[:, lo:hi]
        y = cw_ref[GD_CONV - 1:GD_CONV, lo:hi] * qkv_ref[:, lo:hi]
        for i in range(GD_CONV - 1):
            off = HALO - (GD_CONV - 1) + i
            y = y + cw_ref[i:i + 1, lo:hi] * buf_ref[c, off:off + tc, :]
        act = _silu(y)
        if lo < GD_K_DIM:
            q_ref[:, lo:hi] = _l2norm_heads(act, 1, GD_DH ** -0.5)[0].astype(BF16)
        elif lo < 2 * GD_K_DIM:
            k_ref[:, lo - GD_K_DIM:hi - GD_K_DIM] = _l2norm_heads(act, 1, 1.0)[0].astype(BF16)
        else:
            v_ref[:, lo - 2 * GD_K_DIM:hi - 2 * GD_K_DIM] = act
        tail = buf_ref[c, tc:tc + HALO, :]
        tail_ref[:, lo:hi] = tail
        buf_ref[c, 0:HALO, :] = tail

    row = lax.broadcasted_iota(jnp.int32, (CHUNK, CHUNK), 0)
    col = lax.broadcasted_iota(jnp.int32, (CHUNK, CHUNK), 1)
    tril_ones = jnp.where(row >= col, 1.0, 0.0).astype(BF16)
    triu_ones = jnp.where(row <= col, 1.0, 0.0).astype(BF16)
    row2 = lax.broadcasted_iota(jnp.int32, (CHUNK, PAIR_SQ), 0)
    col2 = lax.broadcasted_iota(jnp.int32, (CHUNK, PAIR_SQ), 1) % CHUNK
    incl2 = row2 >= col2
    strict2 = row2 > col2
    n_pairs = GD_HK

    eye2 = jnp.where(row2 == col2, 1.0, 0.0).astype(F32)
    n_chunks = tc // CHUNK

    def pair_rows(xt, p):
        return jnp.concatenate([xt[PAIR * p + j:PAIR * p + j + 1, :] for j in range(PAIR)], axis=1)

    items = [(c, p) for c in range(n_chunks) for p in range(n_pairs)]
    cums, cumts, betas, betats = [], [], [], []
    for c in range(n_chunks):
        ba = ba_ref[c]
        bat = bat_ref[c]
        beta = jax.nn.sigmoid(ba[:, :GD_HV])
        betat = jax.nn.sigmoid(bat[:GD_HV, :])
        g = -jnp.exp(alog_ref[...]) * _softplus(ba[:, GD_HV:] + dtb_ref[...])
        gt = -jnp.exp(alogt_ref[...]) * _softplus(bat[GD_HV:, :] + dtbt_ref[...])
        if valid_rows is not None:
            rid = lax.broadcasted_iota(jnp.int32, (CHUNK, GD_HV), 0) + c * CHUNK + t * tc
            beta = jnp.where(rid < valid_rows, beta, 0.0)
            g = jnp.where(rid < valid_rows, g, 0.0)
            cid = lax.broadcasted_iota(jnp.int32, (GD_HV, CHUNK), 1) + c * CHUNK + t * tc
            betat = jnp.where(cid < valid_rows, betat, 0.0)
            gt = jnp.where(cid < valid_rows, gt, 0.0)
        cum = sum(jnp.dot(tril_ones, piece, preferred_element_type=F32) for piece in _bf16_pieces(g))
        cumt = sum(jnp.dot(piece, triu_ones, preferred_element_type=F32) for piece in _bf16_pieces(gt))
        glast_ref[c] = jnp.concatenate(
            [jnp.broadcast_to(jnp.exp(cum[CHUNK - 1:CHUNK, h:h + 1]), (1, GD_DH)) for h in range(GD_HV)], axis=1)
        cums.append(cum)
        cumts.append(cumt)
        betas.append(beta)
        betats.append(betat)

    grams = []
    for c, p in items:
        kb = k_ref[c * CHUNK:(c + 1) * CHUNK, p * GD_DH:(p + 1) * GD_DH]
        qb = q_ref[c * CHUNK:(c + 1) * CHUNK, p * GD_DH:(p + 1) * GD_DH]
        grams.append(lax.dot_general(jnp.concatenate([kb, qb], axis=0), jnp.concatenate([kb, kb], axis=0),
                                     (((1,), (1,)), ((), ())), preferred_element_type=F32))
    a_list, attn_list = [], []
    for (c, p), gram in zip(items, grams):
        diff = _sq_cols(cums[c], p) - pair_rows(cumts[c], p)
        dec = jnp.where(incl2, jnp.exp(jnp.where(incl2, diff, 0.0)), 0.0)
        attn_list.append(gram[CHUNK:] * dec)
        a_list.append(jnp.where(strict2, dec * gram[:CHUNK], 0.0) * _sq_cols(betas[c], p))
    inv_list = _tri_inverse_pairs(a_list)
    for (c, p), inv, attn in zip(items, inv_list, attn_list):
        cum_row = pair_rows(cumts[c], p)
        last = jnp.concatenate([jnp.broadcast_to(cumts[c][PAIR * p + j:PAIR * p + j + 1, CHUNK - 1:CHUNK], (1, CHUNK))
                                for j in range(PAIR)], axis=1)
        ecum_row = jnp.exp(cum_row)
        t_beta = inv * pair_rows(betats[c], p)
        lhs_w_ref[c, p] = jnp.concatenate([t_beta, -(t_beta * ecum_row)], axis=1).astype(BF16)
        top = jnp.concatenate([attn, eye2 * ecum_row], axis=1)
        bot = jnp.concatenate([eye2 * jnp.exp(last - cum_row), jnp.zeros((CHUNK, PAIR_SQ), F32)], axis=1)
        lhs_o_ref[c, p] = jnp.concatenate([top, bot], axis=0).astype(BF16)

    def state_body(c, carry):
        r0 = pl.multiple_of(c * CHUNK, CHUNK)
        kbs, s_old, kqs = [], [], []
        for p in range(n_pairs):
            kb = k_ref[pl.ds(r0, CHUNK), p * GD_DH:(p + 1) * GD_DH]
            qb = q_ref[pl.ds(r0, CHUNK), p * GD_DH:(p + 1) * GD_DH]
            s2 = s_ref[p]
            kbs.append(kb)
            s_old.append(s2)
            kqs.append(jnp.dot(jnp.concatenate([kb, qb], axis=0), s2.astype(BF16), preferred_element_type=F32))
        ws = []
        for p in range(n_pairs):
            v2 = v_ref[pl.ds(r0, CHUNK), p * PAIR_DV:(p + 1) * PAIR_DV]
            rhs = jnp.concatenate([_block_diag_dv(v2), _block_diag_dv(kqs[p][:CHUNK])], axis=0)
            ws.append(jnp.dot(lhs_w_ref[c, p], rhs, preferred_element_type=F32))
        for p in range(n_pairs):
            rhs = jnp.concatenate([_block_diag_dv(ws[p]), _block_diag_dv(kqs[p][CHUNK:])], axis=0)
            ow = jnp.dot(lhs_o_ref[c, p], rhs, preferred_element_type=F32)
            o2 = ow[:CHUNK]
            upd = lax.dot_general(kbs[p], ow[CHUNK:].astype(BF16), (((0,), (0,)), ((), ())),
                                  preferred_element_type=F32)
            s_ref[p] = glast_ref[c, :, p * PAIR_DV:(p + 1) * PAIR_DV] * s_old[p] + upd
            for j in range(PAIR):
                h = PAIR * p + j
                z = z_ref[pl.ds(r0, CHUNK), h * GD_DH:(h + 1) * GD_DH].astype(F32)
                o_ref[pl.ds(r0, CHUNK), h * GD_DH:(h + 1) * GD_DH] = _gated_rmsnorm(
                    o2[:, j * GD_DH:(j + 1) * GD_DH], z, ng_ref[...]).astype(o_ref.dtype)
        return carry

    lax.fori_loop(0, tc // CHUNK, state_body, 0)
    for h in range(GD_HV):
        j = h % PAIR
        sout_ref[h] = s_ref[h // PAIR, :, j * GD_DH:(j + 1) * GD_DH]


def gdn_chunked(qkv, z, ba, bat, cw, a_log, dt_bias, norm_g, carry, s0, tc, valid_rows=None):
    bsz, seq, _ = qkv.shape
    nchunk = tc // CHUNK
    ba4 = ba.reshape(bsz, seq // CHUNK, CHUNK, 2 * GD_HV)
    bat4 = bat.reshape(bsz, 2 * GD_HV, seq // CHUNK, CHUNK).transpose(0, 2, 1, 3)
    alog = a_log.reshape(1, GD_HV)
    dtb = dt_bias.reshape(1, GD_HV)
    alogt = a_log.reshape(GD_HV, 1)
    dtbt = dt_bias.reshape(GD_HV, 1)
    ng = norm_g.reshape(1, GD_DH)
    return pl.pallas_call(
        functools.partial(_gdn_chunk_kernel, valid_rows=valid_rows),
        grid=(bsz, seq // tc),
        in_specs=[pl.BlockSpec((None, tc, GD_CONV_CH), lambda i, t: (i, t, 0)),
                  pl.BlockSpec((None, tc, GD_V_DIM), lambda i, t: (i, t, 0)),
                  pl.BlockSpec((None, nchunk, CHUNK, 2 * GD_HV), lambda i, t: (i, t, 0, 0)),
                  pl.BlockSpec((None, nchunk, 2 * GD_HV, CHUNK), lambda i, t: (i, t, 0, 0)),
                  _full(cw.shape), _full(alog.shape), _full(dtb.shape), _full(alogt.shape), _full(dtbt.shape),
                  _full(ng.shape), _full(carry.shape), _full(s0.shape)],
        out_specs=[pl.BlockSpec((None, tc, GD_V_DIM), lambda i, t: (i, t, 0)),
                   pl.BlockSpec((None, HALO, GD_CONV_CH), lambda i, t: (i, 0, 0)),
                   pl.BlockSpec((None, GD_HV, GD_DH, GD_DH), lambda i, t: (i, 0, 0, 0))],
        out_shape=[jax.ShapeDtypeStruct((bsz, seq, GD_V_DIM), BF16),
                   jax.ShapeDtypeStruct((bsz, HALO, GD_CONV_CH), F32),
                   jax.ShapeDtypeStruct((bsz, GD_HV, GD_DH, GD_DH), F32)],
        scratch_shapes=[pltpu.VMEM((GD_CONV_CH // LANES, tc + HALO, LANES), F32),
                        pltpu.VMEM((GD_HK, GD_DH, PAIR_DV), F32),
                        pltpu.VMEM((tc, GD_K_DIM), BF16), pltpu.VMEM((tc, GD_K_DIM), BF16),
                        pltpu.VMEM((tc, GD_V_DIM), F32),
                        pltpu.VMEM((nchunk, GD_HK, CHUNK, 2 * PAIR_SQ), BF16),
                        pltpu.VMEM((nchunk, GD_HK, 2 * CHUNK, 2 * PAIR_SQ), BF16),
                        pltpu.VMEM((nchunk, 1, GD_V_DIM), F32)],
        compiler_params=_cparams(("arbitrary", "arbitrary")),
        name="gdn_chunked",
    )(qkv, z, ba4, bat4, cw, alog, dtb, alogt, dtbt, ng, carry, s0)


def _gdn_step_kernel(cur_ref, st_ref, z_ref, ba_ref, cw_ref, alog_ref, dtb_ref, ng_ref, s_ref,
                     o_ref, sout_ref, oacc_ref):
    ns = cur_ref.shape[0]
    y = cw_ref[GD_CONV - 1:GD_CONV, :] * cur_ref[...]
    for i in range(GD_CONV - 1):
        y = y + cw_ref[i:i + 1, :] * st_ref[:, i, :]
    act = _silu(y)
    qh = _l2norm_heads(act[:, :GD_K_DIM], GD_HK, GD_DH ** -0.5)
    kh = _l2norm_heads(act[:, GD_K_DIM:2 * GD_K_DIM], GD_HK, 1.0)
    qk_t = jnp.concatenate(qh + kh, axis=0).T
    ba = ba_ref[...]
    beta = jax.nn.sigmoid(ba[:, :GD_HV])
    eg = jnp.exp(-jnp.exp(alog_ref[...]) * _softplus(ba[:, GD_HV:] + dtb_ref[...]))
    for s in range(ns):
        for h in range(GD_HV):
            g = h // (GD_HV // GD_HK)
            qcol = qk_t[:, g * ns + s:g * ns + s + 1]
            kcol = qk_t[:, (GD_HK + g) * ns + s:(GD_HK + g) * ns + s + 1]
            sd = s_ref[s, h] * eg[s:s + 1, h:h + 1]
            ks = jnp.sum(sd * kcol, axis=0, keepdims=True)
            v = act[s:s + 1, 2 * GD_K_DIM + h * GD_DH:2 * GD_K_DIM + (h + 1) * GD_DH]
            w = beta[s:s + 1, h:h + 1] * (v - ks)
            sn = sd + kcol * w
            sout_ref[s, h] = sn
            oacc_ref[s:s + 1, h * GD_DH:(h + 1) * GD_DH] = jnp.sum(sn * qcol, axis=0, keepdims=True)
    for h in range(GD_HV):
        lo, hi = h * GD_DH, (h + 1) * GD_DH
        o_ref[:, lo:hi] = _gated_rmsnorm(oacc_ref[:, lo:hi], z_ref[:, lo:hi].astype(F32),
                                         ng_ref[...]).astype(o_ref.dtype)


def gdn_step(cur, st, z, ba, cw, a_log, dt_bias, norm_g, s0, ns=8):
    n = cur.shape[0]
    alog = a_log.reshape(1, GD_HV)
    dtb = dt_bias.reshape(1, GD_HV)
    ng = norm_g.reshape(1, GD_DH)
    return pl.pallas_call(
        _gdn_step_kernel,
        grid=(n // ns,),
        in_specs=[pl.BlockSpec((ns, GD_CONV_CH), lambda i: (i, 0)),
                  pl.BlockSpec((ns, GD_CONV - 1, GD_CONV_CH), lambda i: (i, 0, 0)),
                  pl.BlockSpec((ns, GD_V_DIM), lambda i: (i, 0)),
                  pl.BlockSpec((ns, 2 * GD_HV), lambda i: (i, 0)),
                  _full(cw.shape), _full(alog.shape), _full(dtb.shape), _full(ng.shape),
                  pl.BlockSpec((ns, GD_HV, GD_DH, GD_DH), lambda i: (i, 0, 0, 0))],
        out_specs=[pl.BlockSpec((ns, GD_V_DIM), lambda i: (i, 0)),
                   pl.BlockSpec((ns, GD_HV, GD_DH, GD_DH), lambda i: (i, 0, 0, 0))],
        out_shape=[jax.ShapeDtypeStruct((n, GD_V_DIM), BF16),
                   jax.ShapeDtypeStruct((n, GD_HV, GD_DH, GD_DH), F32)],
        scratch_shapes=[pltpu.VMEM((ns, GD_V_DIM), F32)],
        compiler_params=_cparams(("arbitrary",)),
        name="gdn_step",
    )(cur, st, z, ba, cw, alog, dtb, ng, s0)


def _row_copy(src_hbm, row, dst, slot, sem):
    return pltpu.make_async_copy(src_hbm.at[pl.ds(pl.multiple_of(row * ROW_SLABS, ROW_SLABS), ROW_SLABS)],
                                 dst.at[pl.ds(pl.multiple_of(slot * ROW_SLABS, ROW_SLABS), ROW_SLABS)], sem)


def _start_row_gather(src_hbm, idx_ref, base, stride, n, dst, sem, both_queues=False):
    def issue(blk, c):
        r0 = blk * DMA_UNROLL
        for u in range(DMA_UNROLL):
            _row_copy(src_hbm, idx_ref[base + stride * (r0 + u)], dst, r0 + u, sem).start(
                priority=u % 2 if both_queues else 0)
        return c

    lax.fori_loop(0, n // DMA_UNROLL, issue, 0)


def _wait_row_gather(src_hbm, n, dst, sem):
    pltpu.make_async_copy(src_hbm.at[pl.ds(0, n * ROW_SLABS)], dst, sem).wait()


def _combine_kernel(dest_ref, x_ref, pv_ref, g_ref, b_ref, y_hbm, o_ref, buf_ref, sem, *, first_block):
    tt = o_ref.shape[0]
    i = pl.program_id(0)
    slot = i % 2

    def start(blk, into):
        base = (blk + first_block) * tt * TOP_K
        for k in range(TOP_K):
            _start_row_gather(y_hbm, dest_ref, base + k, TOP_K, tt, buf_ref.at[into, k], sem.at[into, k],
                              both_queues=True)

    @pl.when(i == 0)
    def _():
        start(0, 0)

    @pl.when(i + 1 < pl.num_programs(0))
    def _():
        start(i + 1, 1 - slot)

    for k in range(TOP_K):
        _wait_row_gather(y_hbm, tt, buf_ref.at[slot, k], sem.at[slot, k])
    pv = pv_ref[...]
    slabs = []
    for s in range(ROW_SLABS):
        y = pv[:, 0:1] * _slab(buf_ref.at[slot, 0], s, tt) + pv[:, 1:2] * _slab(buf_ref.at[slot, 1], s, tt)
        slabs.append(DEEPNORM_ALPHA * x_ref[:, s * LANES:(s + 1) * LANES] + y)
    mu = sum(jnp.sum(r, axis=-1, keepdims=True) for r in slabs) * (1.0 / D_MODEL)
    var = sum(jnp.sum((r - mu) * (r - mu), axis=-1, keepdims=True) for r in slabs) * (1.0 / D_MODEL)
    inv = lax.rsqrt(var + LN_EPS)
    for s in range(ROW_SLABS):
        cs = slice(s * LANES, (s + 1) * LANES)
        o_ref[:, cs] = (slabs[s] - mu) * inv * g_ref[:, cs] + b_ref[:, cs]


def combine(dest, x, pv, g, b, yb3, tt, first_block, n_blocks):
    grid_spec = pltpu.PrefetchScalarGridSpec(
        num_scalar_prefetch=1,
        grid=(n_blocks,),
        in_specs=[pl.BlockSpec((tt, D_MODEL), lambda i, dst: (i + first_block, 0)),
                  pl.BlockSpec((tt, TOP_K), lambda i, dst: (i + first_block, 0)),
                  pl.BlockSpec(g.shape, lambda i, dst: (0, 0)),
                  pl.BlockSpec(b.shape, lambda i, dst: (0, 0)),
                  pl.BlockSpec(memory_space=pl.ANY)],
        out_specs=pl.BlockSpec((tt, D_MODEL), lambda i, dst: (i, 0)),
        scratch_shapes=[pltpu.VMEM((2, TOP_K, tt * ROW_SLABS, LANES), F32), pltpu.SemaphoreType.DMA((2, TOP_K))],
    )
    return pl.pallas_call(
        functools.partial(_combine_kernel, first_block=first_block),
        grid_spec=grid_spec,
        out_shape=jax.ShapeDtypeStruct((n_blocks * tt, D_MODEL), F32),
        compiler_params=_cparams(("arbitrary",)),
        name="moe_combine",
    )(dest, x, pv, g, b, yb3)


def moe_experts(xb, topi, w1, w3, w2, tm):
    n = topi.shape[0]
    eid = topi.reshape(-1)
    onehot = (eid[:, None] == jnp.arange(N_EXPERTS, dtype=jnp.int32)[None, :]).astype(jnp.int32)
    csum = jnp.cumsum(onehot, axis=0)
    rank = jnp.sum((csum - onehot) * onehot, axis=1)
    counts = csum[-1]
    padded = (counts + tm - 1) // tm * tm
    pends = jnp.cumsum(padded)
    pstarts = pends - padded
    dest = (pstarts[eid] + rank).astype(jnp.int32)
    nb = (n * TOP_K + tm - 1) // tm + N_EXPERTS
    n_rows = nb * tm
    block_start = jnp.arange(nb, dtype=jnp.int32) * tm
    block_e = jnp.minimum(jnp.searchsorted(pends, block_start, side="right"), N_EXPERTS - 1).astype(jnp.int32)
    n_used = (pends[-1] // tm).astype(jnp.int32).reshape(1)
    n_chunks = n // TOKEN_CHUNK
    chunk_end = csum[TOKEN_CHUNK * TOP_K - 1::TOKEN_CHUNK * TOP_K]
    ends = chunk_end[:, block_e]
    first_entry = block_start - pstarts[block_e]
    last_entry = jnp.minimum(first_entry + tm, counts[block_e]) - 1
    chunk0 = jnp.minimum(jnp.sum(ends <= first_entry[None, :], axis=0), n_chunks - 1).astype(jnp.int32)
    chunk1 = jnp.minimum(jnp.sum(ends <= last_entry[None, :], axis=0), n_chunks - 1).astype(jnp.int32)
    chunk_cnt = jnp.maximum(chunk1 - chunk0 + 1, 1).astype(jnp.int32)
    dest_t = dest.reshape(n, TOP_K).T

    yb3 = swiglu_experts(block_e, n_used, chunk0, chunk_cnt, first_entry.astype(jnp.int32),
                         chunk_end.reshape(-1).astype(jnp.int32), xb, dest_t, w1, w3, w2, n_rows, tm)
    return yb3, dest


def kernel(x_prompt, x_sample, state_conv_a, state_conv_b, state_delta, meta_tokens, ln_g, ln_b, sc_w_in, sc_conv, sc_w_out, ffn_w1, ffn_w3, ffn_w2, gd_w_in, gd_conv, gd_a_log, gd_dt_bias, gd_norm_g, gd_w_out, moe_router, moe_w1, moe_w3, moe_w2):
    bsz, seq, d = x_prompt.shape
    n_s = x_sample.shape[0]
    n_small = N_META + n_s
    n_p = bsz * seq

    def row(v):
        return v.reshape(1, -1)

    sc_w_in_b = sc_w_in[0].astype(BF16)
    sc_w_out_b = sc_w_out[0].astype(BF16)
    ffn_w1_b, ffn_w3_b, ffn_w2_b = ffn_w1.astype(BF16), ffn_w3.astype(BF16), ffn_w2.astype(BF16)
    gd_w_qkv_b = gd_w_in[0][:, :GD_CONV_CH].astype(BF16)
    gd_w_z_b = gd_w_in[0][:, GD_CONV_CH:GD_CONV_CH + GD_V_DIM].astype(BF16)
    gd_w_ba_b = gd_w_in[0][:, GD_CONV_CH + GD_V_DIM:].astype(BF16)
    gd_w_out_b = gd_w_out[0].astype(BF16)
    moe_w1_b, moe_w3_b, moe_w2_b = moe_w1[0].astype(BF16), moe_w3[0].astype(BF16), moe_w2[0].astype(BF16)

    x_small = jnp.concatenate([meta_tokens.astype(F32), x_sample.reshape(n_s, d)], axis=0)

    xa_s, ch_s = l0_mix_small(x_small, state_conv_a[0, :, 0], state_conv_a[0, :, 1], sc_w_in_b, sc_w_out_b,
                              sc_conv[0], row(ln_g[0, 0]), row(ln_b[0, 0]))
    xa_p, tail_a = l0_mix_prompt(x_prompt, sc_w_in_b, sc_w_out_b, sc_conv[0], row(ln_g[0, 0]), row(ln_b[0, 0]),
                                 ch_s[N_META - HALO:N_META])
    def dense_ffn(x, tm):
        return swiglu_ln(x, ffn_w1_b[0], ffn_w3_b[0], ffn_w2_b[0], row(ln_g[0, 1]), row(ln_b[0, 1]), tm)

    xb_s = dense_ffn(xa_s, n_small)
    xb_p = dense_ffn(xa_p.reshape(n_p, d), 512)

    def gdn_inproj(x, tm):
        qkv = matmul(x, gd_w_qkv_b, F32, tm, 2048)
        z = matmul(x, gd_w_z_b, BF16, tm, 2048)
        ba = matmul(x, gd_w_ba_b, F32, tm, 2 * GD_HV)
        return qkv, z, ba

    qkv_s, z_s, ba_s = gdn_inproj(xb_s, n_small)
    qkv_p, z_p, ba_p = gdn_inproj(xb_p, 1024)

    pad = CHUNK - N_META

    def meta_pad(a):
        return jnp.pad(a[:N_META], ((0, pad), (0, 0)))[None]

    ba_m = meta_pad(ba_s)
    o_m, _, s_meta = gdn_chunked(meta_pad(qkv_s), meta_pad(z_s), ba_m, jnp.swapaxes(ba_m, 1, 2), gd_conv[0],
                                 gd_a_log[0], gd_dt_bias[0], gd_norm_g[0],
                                 jnp.zeros((HALO, GD_CONV_CH), F32), jnp.zeros((GD_HV, GD_DH, GD_DH), F32),
                                 CHUNK, valid_rows=N_META)
    o_smp, s_smp = gdn_step(qkv_s[N_META:], state_conv_b[0], z_s[N_META:], ba_s[N_META:], gd_conv[0],
                            gd_a_log[0], gd_dt_bias[0], gd_norm_g[0], state_delta[0])
    ba_p3 = ba_p.reshape(bsz, seq, 2 * GD_HV)
    o_p, tail_b, s_p = gdn_chunked(qkv_p.reshape(bsz, seq, GD_CONV_CH), z_p.reshape(bsz, seq, GD_V_DIM), ba_p3,
                                   jnp.swapaxes(ba_p3, 1, 2), gd_conv[0], gd_a_log[0], gd_dt_bias[0],
                                   gd_norm_g[0], qkv_s[N_META - HALO:N_META], s_meta[0], 256)
    tt = TOKEN_BLOCK
    small_pad = ((0, tt - n_small), (0, 0))
    o_s = jnp.pad(jnp.concatenate([o_m[0, :N_META], o_smp], axis=0), small_pad)
    xc, xc_b, topi, topv = matmul_ln_router(o_p.reshape(n_p, GD_V_DIM), xb_p, o_s, jnp.pad(xb_s, small_pad),
                                            gd_w_out_b, row(ln_g[1, 0]), row(ln_b[1, 0]), moe_router[0])
    yb3, dest = moe_experts(xc_b, topi, moe_w1_b, moe_w3_b, moe_w2_b, EXPERT_BLOCK)
    xd_p = combine(dest, xc, topv, row(ln_g[1, 1]), row(ln_b[1, 1]), yb3, tt, 0, n_p // tt)
    xd_s = combine(dest, xc, topv, row(ln_g[1, 1]), row(ln_b[1, 1]), yb3, tt, n_p // tt, 1)

    y_prompt = xd_p.reshape(bsz, seq, d)
    y_sample = xd_s[N_META:n_small].reshape(n_s, 1, d)
    new_conv_a_prompt = tail_a[None, :, HALO - (SC_WIDTH - 1):]
    new_conv_b_prompt = tail_b[None, :, HALO - (GD_CONV - 1):]
    new_delta_prompt = s_p[None]
    new_conv_a_sample = jnp.stack([state_conv_a[0, :, 1], ch_s[N_META:]], axis=1)[None]
    new_conv_b_sample = jnp.concatenate([state_conv_b[0, :, 1:], qkv_s[N_META:, None]], axis=1)[None]
    new_delta_sample = s_smp[None]
    return (y_prompt, y_sample, new_conv_a_prompt, new_conv_b_prompt, new_delta_prompt,
            new_conv_a_sample, new_conv_b_sample, new_delta_sample)
```

```python
import functools

import jax
import jax.numpy as jnp
from jax import lax
from jax.experimental import pallas as pl
from jax.experimental.pallas import tpu as pltpu

F32 = jnp.float32
BF16 = jnp.bfloat16

D_MODEL = 1024
N_META = 16
SC_WIDTH = 3
GD_HK = 8
GD_HV = 16
GD_DH = 128
GD_K_DIM = GD_HK * GD_DH
GD_V_DIM = GD_HV * GD_DH
GD_CONV = 4
GD_CONV_CH = 2 * GD_K_DIM + GD_V_DIM
D_FF = 3584
N_EXPERTS = 8
TOP_K = 2
LN_EPS = 1e-5
RMS_EPS = 1e-6
DEPTH = 2
DEEPNORM_ALPHA = (2 * DEPTH) ** 0.25

LANES = 128
ROW_SLABS = D_MODEL // LANES
DMA_UNROLL = 8
EXPERT_F_STEPS = 2
TOKEN_CHUNK = 256
CHUNK_RING = 12
SEL_ALIGN = 16
SEL_ROWS = TOKEN_CHUNK + SEL_ALIGN
TOKEN_BLOCK = 256
EXPERT_BLOCK = 512
CHUNK = 64
HALO = 8
V7X_VMEM_LIMIT = 56 * 1024 * 1024


def _cparams(sem, vmem=V7X_VMEM_LIMIT):
    return pltpu.CompilerParams(dimension_semantics=sem, vmem_limit_bytes=vmem)


def _bdot(a, b):
    return jnp.dot(a.astype(BF16), b.astype(BF16), preferred_element_type=F32)


def _bdot_nt(a, b):
    return lax.dot_general(a.astype(BF16), b.astype(BF16), (((1,), (1,)), ((), ())),
                           preferred_element_type=F32)


def _bdot_tn(a, b):
    return lax.dot_general(a.astype(BF16), b.astype(BF16), (((0,), (0,)), ((), ())),
                           preferred_element_type=F32)


def _bf16_pieces(x):
    p0 = x.astype(BF16)
    r1 = x - p0.astype(F32)
    p1 = r1.astype(BF16)
    p2 = (r1 - p1.astype(F32)).astype(BF16)
    return p0, p1, p2


def _dot_f32x3(a, b):
    a_hi = a.astype(BF16)
    b_hi = b.astype(BF16)
    a_lo = (a - a_hi.astype(F32)).astype(BF16)
    b_lo = (b - b_hi.astype(F32)).astype(BF16)
    return (jnp.dot(a_hi, b_hi, preferred_element_type=F32) + jnp.dot(a_lo, b_hi, preferred_element_type=F32)
            + jnp.dot(a_hi, b_lo, preferred_element_type=F32))


def _layer_norm(r, g, b):
    mu = jnp.mean(r, axis=-1, keepdims=True)
    d = r - mu
    var = jnp.mean(d * d, axis=-1, keepdims=True)
    return d * lax.rsqrt(var + LN_EPS) * g + b


def _silu(x):
    return x * jax.nn.sigmoid(x)


def _l0_inproj(xb, w_in_ref, ch_ref, bg_ref, row0, rows):
    col_chunk = 512
    for j in range(D_MODEL // col_chunk):
        lo, hi = j * col_chunk, (j + 1) * col_chunk
        bg = jnp.dot(xb, w_in_ref[:, lo:hi], preferred_element_type=F32)
        c = jnp.dot(xb, w_in_ref[:, D_MODEL + lo:D_MODEL + hi], preferred_element_type=F32)
        h = jnp.dot(xb, w_in_ref[:, 2 * D_MODEL + lo:2 * D_MODEL + hi], preferred_element_type=F32)
        ch_ref[row0:row0 + rows, lo:hi] = c * h
        bg_ref[:, lo:hi] = bg


def _l0_prompt_kernel(x_ref, w_in_ref, w_out_ref, cw_ref, g_ref, b_ref, carry_ref,
                      o_ref, tail_ref, buf_ref, bg_ref):
    tm = x_ref.shape[0]
    t = pl.program_id(1)

    @pl.when(t == 0)
    def _():
        buf_ref[0:HALO, :] = carry_ref[...]

    x = x_ref[...]
    _l0_inproj(x.astype(BF16), w_in_ref, buf_ref, bg_ref, HALO, tm)
    y = (cw_ref[0:1, :] * buf_ref[HALO - 2:HALO - 2 + tm, :]
         + cw_ref[1:2, :] * buf_ref[HALO - 1:HALO - 1 + tm, :]
         + cw_ref[2:3, :] * buf_ref[HALO:HALO + tm, :])
    u = (bg_ref[...] * y).astype(BF16)
    m = jnp.dot(u, w_out_ref[...], preferred_element_type=F32)
    o_ref[...] = _layer_norm(DEEPNORM_ALPHA * x + m, g_ref[...], b_ref[...])
    tail = buf_ref[tm:tm + HALO, :]
    tail_ref[...] = tail
    buf_ref[0:HALO, :] = tail


def _l0_small_kernel(x_ref, st0_ref, st1_ref, w_in_ref, w_out_ref, cw_ref, g_ref, b_ref,
                     o_ref, ch_out_ref, buf_ref, bg_ref):
    n = x_ref.shape[0]
    x = x_ref[...]
    buf_ref[0:HALO, :] = jnp.zeros((HALO, D_MODEL), F32)
    _l0_inproj(x.astype(BF16), w_in_ref, buf_ref, bg_ref, HALO, n)
    y_meta = (cw_ref[0:1, :] * buf_ref[HALO - 2:HALO - 2 + N_META, :]
              + cw_ref[1:2, :] * buf_ref[HALO - 1:HALO - 1 + N_META, :]
              + cw_ref[2:3, :] * buf_ref[HALO:HALO + N_META, :])
    ch = buf_ref[HALO:HALO + n, :]
    y_s = (cw_ref[0:1, :] * st0_ref[...] + cw_ref[1:2, :] * st1_ref[...]
           + cw_ref[2:3, :] * ch[N_META:, :])
    y = jnp.concatenate([y_meta, y_s], axis=0)
    u = (bg_ref[...] * y).astype(BF16)
    m = jnp.dot(u, w_out_ref[...], preferred_element_type=F32)
    o_ref[...] = _layer_norm(DEEPNORM_ALPHA * x + m, g_ref[...], b_ref[...])
    ch_out_ref[...] = ch


def _full(shape):
    nd = len(shape)
    return pl.BlockSpec(shape, lambda *_: (0,) * nd)


def l0_mix_prompt(x, w_in, w_out, cw, g, b, carry, tm=512):
    bsz, seq, d = x.shape
    return pl.pallas_call(
        _l0_prompt_kernel,
        grid=(bsz, seq // tm),
        in_specs=[pl.BlockSpec((None, tm, d), lambda i, t: (i, t, 0)),
                  _full(w_in.shape), _full(w_out.shape), _full(cw.shape), _full(g.shape), _full(b.shape),
                  _full(carry.shape)],
        out_specs=[pl.BlockSpec((None, tm, d), lambda i, t: (i, t, 0)),
                   pl.BlockSpec((None, HALO, d), lambda i, t: (i, 0, 0))],
        out_shape=[jax.ShapeDtypeStruct((bsz, seq, d), F32),
                   jax.ShapeDtypeStruct((bsz, HALO, d), F32)],
        scratch_shapes=[pltpu.VMEM((tm + HALO, d), F32), pltpu.VMEM((tm, d), F32)],
        compiler_params=_cparams(("arbitrary", "arbitrary")),
        name="l0_mix_prompt",
    )(x, w_in, w_out, cw, g, b, carry)


def l0_mix_small(x, st0, st1, w_in, w_out, cw, g, b):
    n, d = x.shape
    return pl.pallas_call(
        _l0_small_kernel,
        grid=(1,),
        in_specs=[_full(a.shape) for a in (x, st0, st1, w_in, w_out, cw, g, b)],
        out_specs=[_full((n, d)), _full((n, d))],
        out_shape=[jax.ShapeDtypeStruct((n, d), F32), jax.ShapeDtypeStruct((n, d), F32)],
        scratch_shapes=[pltpu.VMEM((n + HALO, d), F32), pltpu.VMEM((n, d), F32)],
        compiler_params=_cparams(("arbitrary",)),
        name="l0_mix_small",
    )(x, st0, st1, w_in, w_out, cw, g, b)


def _slab(ref, s, rows):
    return ref[pl.ds(s, rows, stride=ROW_SLABS), :]


def _to_slabs(o_ref, val):
    rows = val.shape[0]
    for s in range(ROW_SLABS):
        o_ref[pl.ds(s, rows, stride=ROW_SLABS), :] = val[:, s * LANES:(s + 1) * LANES]


def _swiglu_partial(xb, w1_ref, w3_ref, w2_ref, acc_ref):
    a = jnp.dot(xb, w1_ref[...], preferred_element_type=F32)
    b = jnp.dot(xb, w3_ref[...], preferred_element_type=F32)
    h = (_silu(a) * b).astype(BF16)
    part = jnp.dot(h, w2_ref[...], preferred_element_type=F32)
    acc_ref[...] = jnp.where(pl.program_id(1) > 0, acc_ref[...], 0.0) + part


def _zero_acc_once(acc_ref):
    @pl.when((pl.program_id(0) == 0) & (pl.program_id(1) == 0))
    def _():
        acc_ref[...] = jnp.zeros(acc_ref.shape, acc_ref.dtype)


def _swiglu_ln_kernel(x_ref, w1_ref, w3_ref, w2_ref, g_ref, b_ref, o_ref, acc_ref):
    _zero_acc_once(acc_ref)
    _swiglu_partial(x_ref[...].astype(BF16), w1_ref, w3_ref, w2_ref, acc_ref)

    @pl.when(pl.program_id(1) == pl.num_programs(1) - 1)
    def _():
        o_ref[...] = _layer_norm(DEEPNORM_ALPHA * x_ref[...] + acc_ref[...], g_ref[...], b_ref[...])


def swiglu_ln(x, w1, w3, w2, g, b, tm, tf=1792):
    n, d = x.shape
    return pl.pallas_call(
        _swiglu_ln_kernel,
        grid=(n // tm, D_FF // tf),
        in_specs=[pl.BlockSpec((tm, d), lambda i, f: (i, 0)),
                  pl.BlockSpec((d, tf), lambda i, f: (0, f)),
                  pl.BlockSpec((d, tf), lambda i, f: (0, f)),
                  pl.BlockSpec((tf, d), lambda i, f: (f, 0)),
                  _full(g.shape), _full(b.shape)],
        out_specs=pl.BlockSpec((tm, d), lambda i, f: (i, 0)),
        out_shape=jax.ShapeDtypeStruct((n, d), F32),
        scratch_shapes=[pltpu.VMEM((tm, d), F32)],
        compiler_params=_cparams(("arbitrary", "arbitrary")),
        name="swiglu_ln",
    )(x, w1, w3, w2, g, b)


def _token_chunk_copy(x_hbm, chunk, buf_ref, slot, sem):
    start = pl.multiple_of(chunk * TOKEN_CHUNK, TOKEN_CHUNK)
    return pltpu.make_async_copy(x_hbm.at[pl.ds(start, TOKEN_CHUNK)], buf_ref.at[slot], sem.at[slot])


def _swiglu_experts_kernel(be_ref, nu_ref, c0_ref, cn_ref, e0_ref, cend_ref, x_hbm, dest_ref, w1_ref, w3_ref,
                           w2_ref, o_ref, acc_ref, xsel_ref, xb_ref, chunk_ref, sem):
    tm = xb_ref.shape[0]
    i = pl.program_id(0)
    f = pl.program_id(1)
    nf = pl.num_programs(1)
    n_used = nu_ref[0]
    _zero_acc_once(acc_ref)

    def start_ring(blk):
        for slot in range(CHUNK_RING):
            @pl.when(slot < cn_ref[blk] + cn_ref[blk] % 2)
            def _():
                _token_chunk_copy(x_hbm, c0_ref[blk] + jnp.minimum(slot, cn_ref[blk] - 1), chunk_ref, slot, sem).start()

    @pl.when(i < n_used)
    def _():
        @pl.when(f == 0)
        def _():
            first = c0_ref[i]
            count = cn_ref[i]
            expert = be_ref[i]
            row_iota = lax.broadcasted_iota(jnp.int32, (SEL_ROWS, TOKEN_CHUNK), 0)

            @pl.when(i == 0)
            def _():
                start_ring(0)

            xsel_ref[...] = jnp.zeros(xsel_ref.shape, xsel_ref.dtype)

            def pick_pair(jj, carry):
                picked = []
                for u in range(2):
                    j = 2 * jj + u
                    slot = jnp.where(j < CHUNK_RING, j, u)

                    chunk = first + jnp.minimum(j, count - 1)

                    @pl.when(j >= CHUNK_RING)
                    def _():
                        _token_chunk_copy(x_hbm, chunk, chunk_ref, u, sem).start()

                    before = jnp.where(chunk > 0, cend_ref[jnp.maximum(chunk - 1, 0) * N_EXPERTS + expert], 0)
                    row0 = jnp.clip(before - e0_ref[i], 0, tm - SEL_ROWS) // SEL_ALIGN * SEL_ALIGN
                    row0 = pl.multiple_of(row0, SEL_ALIGN)
                    row_id = row_iota + (i * tm + row0)
                    _token_chunk_copy(x_hbm, chunk, chunk_ref, slot, sem).wait()
                    lane0 = pl.multiple_of(chunk * TOKEN_CHUNK, TOKEN_CHUNK)
                    hit = row_id == dest_ref[0:1, pl.ds(lane0, TOKEN_CHUNK)]
                    for k in range(1, TOP_K):
                        hit = hit | (row_id == dest_ref[k:k + 1, pl.ds(lane0, TOKEN_CHUNK)])
                    sel = jnp.where(hit & (j < count), 1.0, 0.0).astype(BF16)
                    picked.append((row0, jnp.dot(sel, chunk_ref[slot], preferred_element_type=F32)))
                for row0, rows in picked:
                    xsel_ref[pl.ds(row0, SEL_ROWS), :] += rows
                return carry

            lax.fori_loop(0, (count + 1) // 2, pick_pair, 0)
            xb_ref[...] = xsel_ref[...].astype(BF16)

        @pl.when((f == nf - 1) & (i + 1 < n_used))
        def _():
            start_ring(i + 1)

        _swiglu_partial(xb_ref[...], w1_ref, w3_ref, w2_ref, acc_ref)

        @pl.when(f == nf - 1)
        def _():
            _to_slabs(o_ref, acc_ref[...])

    @pl.when(i >= nu_ref[0])
    def _():
        o_ref[...] = jnp.zeros(o_ref.shape, o_ref.dtype)


def swiglu_experts(block_e, n_used, chunk0, n_chunks, entry0, chunk_end, xb, dest_t, w1, w3, w2, n_rows, tm):
    d = D_MODEL
    nf = EXPERT_F_STEPS
    tf = D_FF // nf

    def f_idx(i, f, nu):
        return jnp.where(i < nu[0], f, nf - 1)

    grid_spec = pltpu.PrefetchScalarGridSpec(
        num_scalar_prefetch=6,
        grid=(n_rows // tm, nf),
        in_specs=[pl.BlockSpec(memory_space=pl.ANY),
                  pl.BlockSpec(dest_t.shape, lambda i, f, be, nu, c0, cn, e0, ce: (0, 0)),
                  pl.BlockSpec((None, d, tf), lambda i, f, be, nu, c0, cn, e0, ce: (be[i], 0, f_idx(i, f, nu))),
                  pl.BlockSpec((None, d, tf), lambda i, f, be, nu, c0, cn, e0, ce: (be[i], 0, f_idx(i, f, nu))),
                  pl.BlockSpec((None, tf, d), lambda i, f, be, nu, c0, cn, e0, ce: (be[i], f_idx(i, f, nu), 0))],
        out_specs=pl.BlockSpec((tm * ROW_SLABS, LANES), lambda i, f, be, nu, c0, cn, e0, ce: (i, 0)),
        scratch_shapes=[pltpu.VMEM((tm, d), F32),
                        pltpu.VMEM((tm, d), F32),
                        pltpu.VMEM((tm, d), BF16),
                        pltpu.VMEM((CHUNK_RING, TOKEN_CHUNK, d), BF16),
                        pltpu.SemaphoreType.DMA((CHUNK_RING,))],
    )
    return pl.pallas_call(
        _swiglu_experts_kernel,
        grid_spec=grid_spec,
        out_shape=jax.ShapeDtypeStruct((n_rows * ROW_SLABS, LANES), F32),
        compiler_params=_cparams(("arbitrary", "arbitrary")),
        name="swiglu_experts",
    )(block_e, n_used, chunk0, n_chunks, entry0, chunk_end, xb, dest_t, w1, w3, w2)


def _mm_kernel(x_ref, w_ref, o_ref):
    o_ref[...] = jnp.dot(x_ref[...].astype(BF16), w_ref[...], preferred_element_type=F32).astype(o_ref.dtype)


def matmul(x, w, out_dtype, tm, tn):
    n, k = x.shape
    _, m = w.shape
    return pl.pallas_call(
        _mm_kernel,
        grid=(n // tm, m // tn),
        in_specs=[pl.BlockSpec((tm, k), lambda i, j: (i, 0)),
                  pl.BlockSpec((k, tn), lambda i, j: (0, j))],
        out_specs=pl.BlockSpec((tm, tn), lambda i, j: (i, j)),
        out_shape=jax.ShapeDtypeStruct((n, m), out_dtype),
        compiler_params=_cparams(("arbitrary", "arbitrary")),
        name="matmul",
    )(x, w)


def _top2(logits):
    mx = jnp.max(logits, axis=-1, keepdims=True)
    ex = jnp.exp(logits - mx)
    p = ex / jnp.sum(ex, axis=-1, keepdims=True)
    lane = lax.broadcasted_iota(jnp.int32, p.shape, 1)
    p1 = jnp.max(p, axis=-1, keepdims=True)
    i1 = jnp.min(jnp.where(p == p1, lane, N_EXPERTS), axis=-1, keepdims=True)
    rest = jnp.where(lane == i1, -1.0, p)
    p2 = jnp.max(rest, axis=-1, keepdims=True)
    i2 = jnp.min(jnp.where(rest == p2, lane, N_EXPERTS), axis=-1, keepdims=True)
    tot = p1 + p2
    return jnp.concatenate([i1, i2], axis=-1), jnp.concatenate([p1 / tot, p2 / tot], axis=-1)


def _mm_ln_router_kernel(a_ref, x_ref, a_tail_ref, x_tail_ref, w_ref, g_ref, b_ref, wr_ref,
                         o_ref, ob_ref, idx_ref, val_ref, *, n_main):
    def block(a_blk, x_blk):
        m = jnp.dot(a_blk[...], w_ref[...], preferred_element_type=F32)
        y = _layer_norm(DEEPNORM_ALPHA * x_blk[...] + m, g_ref[...], b_ref[...])
        o_ref[...] = y
        ob_ref[...] = y.astype(BF16)
        logits = _dot_f32x3(y, wr_ref[...])
        idx_ref[...], val_ref[...] = _top2(logits)

    @pl.when(pl.program_id(0) < n_main)
    def _():
        block(a_ref, x_ref)

    @pl.when(pl.program_id(0) >= n_main)
    def _():
        block(a_tail_ref, x_tail_ref)


def matmul_ln_router(a, x, a_tail, x_tail, w, g, b, wr):
    tm, k = a_tail.shape
    n = a.shape[0]
    d = w.shape[1]
    n_main = n // tm
    n_all = n + tm

    def main(i):
        return (jnp.minimum(i, n_main - 1), 0)

    return pl.pallas_call(
        functools.partial(_mm_ln_router_kernel, n_main=n_main),
        grid=(n_main + 1,),
        in_specs=[pl.BlockSpec((tm, k), main), pl.BlockSpec((tm, d), main),
                  _full(a_tail.shape), _full(x_tail.shape),
                  _full(w.shape), _full(g.shape), _full(b.shape), _full(wr.shape)],
        out_specs=[pl.BlockSpec((tm, d), lambda i: (i, 0)), pl.BlockSpec((tm, d), lambda i: (i, 0)),
                   pl.BlockSpec((tm, TOP_K), lambda i: (i, 0)), pl.BlockSpec((tm, TOP_K), lambda i: (i, 0))],
        out_shape=[jax.ShapeDtypeStruct((n_all, d), F32), jax.ShapeDtypeStruct((n_all, d), BF16),
                   jax.ShapeDtypeStruct((n_all, TOP_K), jnp.int32), jax.ShapeDtypeStruct((n_all, TOP_K), F32)],
        compiler_params=_cparams(("arbitrary",)),
        name="matmul_ln_router",
    )(a, x, a_tail, x_tail, w, g, b, wr)


def _l2norm_heads(x, n_heads, scale):
    outs = []
    for h in range(n_heads):
        xh = x[:, h * GD_DH:(h + 1) * GD_DH]
        ss = jnp.sum(xh * xh, axis=-1, keepdims=True)
        outs.append(xh * (lax.rsqrt(ss + RMS_EPS) * scale))
    return outs


def _softplus(x):
    return jnp.maximum(x, 0.0) + jnp.log1p(jnp.exp(-jnp.abs(x)))


def _gated_rmsnorm(o, z, norm_g):
    ms = jnp.mean(o * o, axis=-1, keepdims=True)
    return o * lax.rsqrt(ms + RMS_EPS) * norm_g * _silu(z)


PAIR = GD_HV // GD_HK
PAIR_SQ = PAIR * CHUNK
PAIR_DV = PAIR * GD_DH


def _sq_cols(x, p):
    lane = lax.broadcasted_iota(jnp.int32, (CHUNK, PAIR_SQ), 1)
    a = jnp.broadcast_to(x[:, PAIR * p:PAIR * p + 1], (CHUNK, PAIR_SQ))
    b = jnp.broadcast_to(x[:, PAIR * p + 1:PAIR * p + 2], (CHUNK, PAIR_SQ))
    return jnp.where(lane < CHUNK, a, b)


def _block_diag_sq(m):
    lane = lax.broadcasted_iota(jnp.int32, m.shape, 1)
    top = jnp.where(lane < CHUNK, m, 0.0).astype(BF16)
    bot = jnp.where(lane < CHUNK, 0.0, m).astype(BF16)
    return jnp.concatenate([top, bot], axis=0)


def _block_diag_dv(m):
    mb = m.astype(BF16)
    zero = jnp.zeros((CHUNK, GD_DH), BF16)
    top = jnp.concatenate([mb[:, :GD_DH], zero], axis=1)
    bot = jnp.concatenate([zero, mb[:, GD_DH:]], axis=1)
    return jnp.concatenate([top, bot], axis=0)


def _tri_inverse_pairs(a_list):
    row = lax.broadcasted_iota(jnp.int32, (CHUNK, PAIR_SQ), 0)
    col = lax.broadcasted_iota(jnp.int32, (CHUNK, PAIR_SQ), 1) % CHUNK
    eye = jnp.where(row == col, 1.0, 0.0).astype(F32)
    s = 1
    inv = None
    while s < CHUNK:
        sub = ((row // (2 * s)) == (col // (2 * s))) & ((row // s) % 2 == 1) & ((col // s) % 2 == 0)
        e_list = [jnp.where(sub, a, 0.0) for a in a_list]
        if s == 1:
            inv = [eye - e for e in e_list]
        else:
            x_list = [jnp.dot(e.astype(BF16), _block_diag_sq(d), preferred_element_type=F32)
                      for e, d in zip(e_list, inv)]
            y_list = [jnp.dot(d.astype(BF16), _block_diag_sq(x), preferred_element_type=F32)
                      for d, x in zip(inv, x_list)]
            inv = [d - y for d, y in zip(inv, y_list)]
        s *= 2
    return inv


def _gdn_chunk_kernel(qkv_ref, z_ref, ba_ref, bat_ref, cw_ref, alog_ref, dtb_ref, alogt_ref, dtbt_ref,
                      ng_ref, carry_ref, s0_ref, o_ref, tail_ref, sout_ref,
                      buf_ref, s_ref, q_ref, k_ref, v_ref, lhs_w_ref, lhs_o_ref, glast_ref,
                      *, valid_rows):
    tc = qkv_ref.shape[0]
    t = pl.program_id(1)

    @pl.when(t == 0)
    def _():
        for c in range(GD_CONV_CH // LANES):
            buf_ref[c, 0:HALO, :] = carry_ref[:, c * LANES:(c + 1) * LANES]
        for h in range(GD_HV):
            j = h % PAIR
            s_ref[h // PAIR, :, j * GD_DH:(j + 1) * GD_DH] = s0_ref[h]

    for c in range(GD_CONV_CH // LANES):
        lo, hi = c * LANES, (c + 1) * LANES
        buf_ref[c, HALO:HALO + tc, :] = qkv_ref[:, lo:hi]
        y = cw_ref[GD_CONV - 1:GD_CONV, lo:hi] * qkv_ref[:, lo:hi]
        for i in range(GD_CONV - 1):
            off = HALO - (GD_CONV - 1) + i
            y = y + cw_ref[i:i + 1, lo:hi] * buf_ref[c, off:off + tc, :]
        act = _silu(y)
        if lo < GD_K_DIM:
            q_ref[:, lo:hi] = _l2norm_heads(act, 1, GD_DH ** -0.5)[0].astype(BF16)
        elif lo < 2 * GD_K_DIM:
            k_ref[:, lo - GD_K_DIM:hi - GD_K_DIM] = _l2norm_heads(act, 1, 1.0)[0].astype(BF16)
        else:
            v_ref[:, lo - 2 * GD_K_DIM:hi - 2 * GD_K_DIM] = act
        tail = buf_ref[c, tc:tc + HALO, :]
        tail_ref[:, lo:hi] = tail
        buf_ref[c, 0:HALO, :] = tail

    row = lax.broadcasted_iota(jnp.int32, (CHUNK, CHUNK), 0)
    col = lax.broadcasted_iota(jnp.int32, (CHUNK, CHUNK), 1)
    tril_ones = jnp.where(row >= col, 1.0, 0.0).astype(BF16)
    triu_ones = jnp.where(row <= col, 1.0, 0.0).astype(BF16)
    row2 = lax.broadcasted_iota(jnp.int32, (CHUNK, PAIR_SQ), 0)
    col2 = lax.broadcasted_iota(jnp.int32, (CHUNK, PAIR_SQ), 1) % CHUNK
    incl2 = row2 >= col2
    strict2 = row2 > col2
    n_pairs = GD_HK

    eye2 = jnp.where(row2 == col2, 1.0, 0.0).astype(F32)
    n_chunks = tc // CHUNK

    def pair_rows(xt, p):
        return jnp.concatenate([xt[PAIR * p + j:PAIR * p + j + 1, :] for j in range(PAIR)], axis=1)

    items = [(c, p) for c in range(n_chunks) for p in range(n_pairs)]
    cums, cumts, betas, betats = [], [], [], []
    for c in range(n_chunks):
        ba = ba_ref[c]
        bat = bat_ref[c]
        beta = jax.nn.sigmoid(ba[:, :GD_HV])
        betat = jax.nn.sigmoid(bat[:GD_HV, :])
        g = -jnp.exp(alog_ref[...]) * _softplus(ba[:, GD_HV:] + dtb_ref[...])
        gt = -jnp.exp(alogt_ref[...]) * _softplus(bat[GD_HV:, :] + dtbt_ref[...])
        if valid_rows is not None:
            rid = lax.broadcasted_iota(jnp.int32, (CHUNK, GD_HV), 0) + c * CHUNK + t * tc
            beta = jnp.where(rid < valid_rows, beta, 0.0)
            g = jnp.where(rid < valid_rows, g, 0.0)
            cid = lax.broadcasted_iota(jnp.int32, (GD_HV, CHUNK), 1) + c * CHUNK + t * tc
            betat = jnp.where(cid < valid_rows, betat, 0.0)
            gt = jnp.where(cid < valid_rows, gt, 0.0)
        cum = sum(jnp.dot(tril_ones, piece, preferred_element_type=F32) for piece in _bf16_pieces(g))
        cumt = sum(jnp.dot(piece, triu_ones, preferred_element_type=F32) for piece in _bf16_pieces(gt))
        glast_ref[c] = jnp.concatenate(
            [jnp.broadcast_to(jnp.exp(cum[CHUNK - 1:CHUNK, h:h + 1]), (1, GD_DH)) for h in range(GD_HV)], axis=1)
        cums.append(cum)
        cumts.append(cumt)
        betas.append(beta)
        betats.append(betat)

    grams = []
    for c, p in items:
        kb = k_ref[c * CHUNK:(c + 1) * CHUNK, p * GD_DH:(p + 1) * GD_DH]
        qb = q_ref[c * CHUNK:(c + 1) * CHUNK, p * GD_DH:(p + 1) * GD_DH]
        grams.append(lax.dot_general(jnp.concatenate([kb, qb], axis=0), jnp.concatenate([kb, kb], axis=0),
                                     (((1,), (1,)), ((), ())), preferred_element_type=F32))
    a_list, attn_list = [], []
    for (c, p), gram in zip(items, grams):
        diff = _sq_cols(cums[c], p) - pair_rows(cumts[c], p)
        dec = jnp.where(incl2, jnp.exp(jnp.where(incl2, diff, 0.0)), 0.0)
        attn_list.append(gram[CHUNK:] * dec)
        a_list.append(jnp.where(strict2, dec * gram[:CHUNK], 0.0) * _sq_cols(betas[c], p))
    inv_list = _tri_inverse_pairs(a_list)
    for (c, p), inv, attn in zip(items, inv_list, attn_list):
        cum_row = pair_rows(cumts[c], p)
        last = jnp.concatenate([jnp.broadcast_to(cumts[c][PAIR * p + j:PAIR * p + j + 1, CHUNK - 1:CHUNK], (1, CHUNK))
                                for j in range(PAIR)], axis=1)
        ecum_row = jnp.exp(cum_row)
        t_beta = inv * pair_rows(betats[c], p)
        lhs_w_ref[c, p] = jnp.concatenate([t_beta, -(t_beta * ecum_row)], axis=1).astype(BF16)
        top = jnp.concatenate([attn, eye2 * ecum_row], axis=1)
        bot = jnp.concatenate([eye2 * jnp.exp(last - cum_row), jnp.zeros((CHUNK, PAIR_SQ), F32)], axis=1)
        lhs_o_ref[c, p] = jnp.concatenate([top, bot], axis=0).astype(BF16)

    def state_body(c, carry):
        r0 = pl.multiple_of(c * CHUNK, CHUNK)
        kbs, s_old, kqs = [], [], []
        for p in range(n_pairs):
            kb = k_ref[pl.ds(r0, CHUNK), p * GD_DH:(p + 1) * GD_DH]
            qb = q_ref[pl.ds(r0, CHUNK), p * GD_DH:(p + 1) * GD_DH]
            s2 = s_ref[p]
            kbs.append(kb)
            s_old.append(s2)
            kqs.append(jnp.dot(jnp.concatenate([kb, qb], axis=0), s2.astype(BF16), preferred_element_type=F32))
        ws = []
        for p in range(n_pairs):
            v2 = v_ref[pl.ds(r0, CHUNK), p * PAIR_DV:(p + 1) * PAIR_DV]
            rhs = jnp.concatenate([_block_diag_dv(v2), _block_diag_dv(kqs[p][:CHUNK])], axis=0)
            ws.append(jnp.dot(lhs_w_ref[c, p], rhs, preferred_element_type=F32))
        for p in range(n_pairs):
            rhs = jnp.concatenate([_block_diag_dv(ws[p]), _block_diag_dv(kqs[p][CHUNK:])], axis=0)
            ow = jnp.dot(lhs_o_ref[c, p], rhs, preferred_element_type=F32)
            o2 = ow[:CHUNK]
            upd = lax.dot_general(kbs[p], ow[CHUNK:].astype(BF16), (((0,), (0,)), ((), ())),
                                  preferred_element_type=F32)
            s_ref[p] = glast_ref[c, :, p * PAIR_DV:(p + 1) * PAIR_DV] * s_old[p] + upd
            for j in range(PAIR):
                h = PAIR * p + j
                z = z_ref[pl.ds(r0, CHUNK), h * GD_DH:(h + 1) * GD_DH].astype(F32)
                o_ref[pl.ds(r0, CHUNK), h * GD_DH:(h + 1) * GD_DH] = _gated_rmsnorm(
                    o2[:, j * GD_DH:(j + 1) * GD_DH], z, ng_ref[...]).astype(o_ref.dtype)
        return carry

    lax.fori_loop(0, tc // CHUNK, state_body, 0)
    for h in range(GD_HV):
        j = h % PAIR
        sout_ref[h] = s_ref[h // PAIR, :, j * GD_DH:(j + 1) * GD_DH]


def gdn_chunked(qkv, z, ba, bat, cw, a_log, dt_bias, norm_g, carry, s0, tc, valid_rows=None):
    bsz, seq, _ = qkv.shape
    nchunk = tc // CHUNK
    ba4 = ba.reshape(bsz, seq // CHUNK, CHUNK, 2 * GD_HV)
    bat4 = bat.reshape(bsz, 2 * GD_HV, seq // CHUNK, CHUNK).transpose(0, 2, 1, 3)
    alog = a_log.reshape(1, GD_HV)
    dtb = dt_bias.reshape(1, GD_HV)
    alogt = a_log.reshape(GD_HV, 1)
    dtbt = dt_bias.reshape(GD_HV, 1)
    ng = norm_g.reshape(1, GD_DH)
    return pl.pallas_call(
        functools.partial(_gdn_chunk_kernel, valid_rows=valid_rows),
        grid=(bsz, seq // tc),
        in_specs=[pl.BlockSpec((None, tc, GD_CONV_CH), lambda i, t: (i, t, 0)),
                  pl.BlockSpec((None, tc, GD_V_DIM), lambda i, t: (i, t, 0)),
                  pl.BlockSpec((None, nchunk, CHUNK, 2 * GD_HV), lambda i, t: (i, t, 0, 0)),
                  pl.BlockSpec((None, nchunk, 2 * GD_HV, CHUNK), lambda i, t: (i, t, 0, 0)),
                  _full(cw.shape), _full(alog.shape), _full(dtb.shape), _full(alogt.shape), _full(dtbt.shape),
                  _full(ng.shape), _full(carry.shape), _full(s0.shape)],
        out_specs=[pl.BlockSpec((None, tc, GD_V_DIM), lambda i, t: (i, t, 0)),
                   pl.BlockSpec((None, HALO, GD_CONV_CH), lambda i, t: (i, 0, 0)),
                   pl.BlockSpec((None, GD_HV, GD_DH, GD_DH), lambda i, t: (i, 0, 0, 0))],
        out_shape=[jax.ShapeDtypeStruct((bsz, seq, GD_V_DIM), BF16),
                   jax.ShapeDtypeStruct((bsz, HALO, GD_CONV_CH), F32),
                   jax.ShapeDtypeStruct((bsz, GD_HV, GD_DH, GD_DH), F32)],
        scratch_shapes=[pltpu.VMEM((GD_CONV_CH // LANES, tc + HALO, LANES), F32),
                        pltpu.VMEM((GD_HK, GD_DH, PAIR_DV), F32),
                        pltpu.VMEM((tc, GD_K_DIM), BF16), pltpu.VMEM((tc, GD_K_DIM), BF16),
                        pltpu.VMEM((tc, GD_V_DIM), F32),
                        pltpu.VMEM((nchunk, GD_HK, CHUNK, 2 * PAIR_SQ), BF16),
                        pltpu.VMEM((nchunk, GD_HK, 2 * CHUNK, 2 * PAIR_SQ), BF16),
                        pltpu.VMEM((nchunk, 1, GD_V_DIM), F32)],
        compiler_params=_cparams(("arbitrary", "arbitrary")),
        name="gdn_chunked",
    )(qkv, z, ba4, bat4, cw, alog, dtb, alogt, dtbt, ng, carry, s0)


def _gdn_step_kernel(cur_ref, st_ref, z_ref, ba_ref, cw_ref, alog_ref, dtb_ref, ng_ref, s_ref,
                     o_ref, sout_ref, oacc_ref):
    ns = cur_ref.shape[0]
    y = cw_ref[GD_CONV - 1:GD_CONV, :] * cur_ref[...]
    for i in range(GD_CONV - 1):
        y = y + cw_ref[i:i + 1, :] * st_ref[:, i, :]
    act = _silu(y)
    qh = _l2norm_heads(act[:, :GD_K_DIM], GD_HK, GD_DH ** -0.5)
    kh = _l2norm_heads(act[:, GD_K_DIM:2 * GD_K_DIM], GD_HK, 1.0)
    qk_t = jnp.concatenate(qh + kh, axis=0).T
    ba = ba_ref[...]
    beta = jax.nn.sigmoid(ba[:, :GD_HV])
    eg = jnp.exp(-jnp.exp(alog_ref[...]) * _softplus(ba[:, GD_HV:] + dtb_ref[...]))
    for s in range(ns):
        for h in range(GD_HV):
            g = h // (GD_HV // GD_HK)
            qcol = qk_t[:, g * ns + s:g * ns + s + 1]
            kcol = qk_t[:, (GD_HK + g) * ns + s:(GD_HK + g) * ns + s + 1]
            sd = s_ref[s, h] * eg[s:s + 1, h:h + 1]
            ks = jnp.sum(sd * kcol, axis=0, keepdims=True)
            v = act[s:s + 1, 2 * GD_K_DIM + h * GD_DH:2 * GD_K_DIM + (h + 1) * GD_DH]
            w = beta[s:s + 1, h:h + 1] * (v - ks)
            sn = sd + kcol * w
            sout_ref[s, h] = sn
            oacc_ref[s:s + 1, h * GD_DH:(h + 1) * GD_DH] = jnp.sum(sn * qcol, axis=0, keepdims=True)
    for h in range(GD_HV):
        lo, hi = h * GD_DH, (h + 1) * GD_DH
        o_ref[:, lo:hi] = _gated_rmsnorm(oacc_ref[:, lo:hi], z_ref[:, lo:hi].astype(F32),
                                         ng_ref[...]).astype(o_ref.dtype)


def gdn_step(cur, st, z, ba, cw, a_log, dt_bias, norm_g, s0, ns=8):
    n = cur.shape[0]
    alog = a_log.reshape(1, GD_HV)
    dtb = dt_bias.reshape(1, GD_HV)
    ng = norm_g.reshape(1, GD_DH)
    return pl.pallas_call(
        _gdn_step_kernel,
        grid=(n // ns,),
        in_specs=[pl.BlockSpec((ns, GD_CONV_CH), lambda i: (i, 0)),
                  pl.BlockSpec((ns, GD_CONV - 1, GD_CONV_CH), lambda i: (i, 0, 0)),
                  pl.BlockSpec((ns, GD_V_DIM), lambda i: (i, 0)),
                  pl.BlockSpec((ns, 2 * GD_HV), lambda i: (i, 0)),
                  _full(cw.shape), _full(alog.shape), _full(dtb.shape), _full(ng.shape),
                  pl.BlockSpec((ns, GD_HV, GD_DH, GD_DH), lambda i: (i, 0, 0, 0))],
        out_specs=[pl.BlockSpec((ns, GD_V_DIM), lambda i: (i, 0)),
                   pl.BlockSpec((ns, GD_HV, GD_DH, GD_DH), lambda i: (i, 0, 0, 0))],
        out_shape=[jax.ShapeDtypeStruct((n, GD_V_DIM), BF16),
                   jax.ShapeDtypeStruct((n, GD_HV, GD_DH, GD_DH), F32)],
        scratch_shapes=[pltpu.VMEM((ns, GD_V_DIM), F32)],
        compiler_params=_cparams(("arbitrary",)),
        name="gdn_step",
    )(cur, st, z, ba, cw, alog, dtb, ng, s0)


def _row_copy(src_hbm, row, dst, slot, sem):
    return pltpu.make_async_copy(src_hbm.at[pl.ds(pl.multiple_of(row * ROW_SLABS, ROW_SLABS), ROW_SLABS)],
                                 dst.at[pl.ds(pl.multiple_of(slot * ROW_SLABS, ROW_SLABS), ROW_SLABS)], sem)


def _start_row_gather(src_hbm, idx_ref, base, stride, n, dst, sem, both_queues=False):
    def issue(blk, c):
        r0 = blk * DMA_UNROLL
        for u in range(DMA_UNROLL):
            _row_copy(src_hbm, idx_ref[base + stride * (r0 + u)], dst, r0 + u, sem).start(
                priority=u % 2 if both_queues else 0)
        return c

    lax.fori_loop(0, n // DMA_UNROLL, issue, 0)


def _wait_row_gather(src_hbm, n, dst, sem):
    pltpu.make_async_copy(src_hbm.at[pl.ds(0, n * ROW_SLABS)], dst, sem).wait()


def _combine_kernel(dest_ref, x_ref, pv_ref, g_ref, b_ref, y_hbm, o_ref, buf_ref, sem, *, first_block):
    tt = o_ref.shape[0]
    i = pl.program_id(0)
    slot = i % 2

    def start(blk, into):
        base = (blk + first_block) * tt * TOP_K
        for k in range(TOP_K):
            _start_row_gather(y_hbm, dest_ref, base + k, TOP_K, tt, buf_ref.at[into, k], sem.at[into, k],
                              both_queues=True)

    @pl.when(i == 0)
    def _():
        start(0, 0)

    @pl.when(i + 1 < pl.num_programs(0))
    def _():
        start(i + 1, 1 - slot)

    for k in range(TOP_K):
        _wait_row_gather(y_hbm, tt, buf_ref.at[slot, k], sem.at[slot, k])
    pv = pv_ref[...]
    slabs = []
    for s in range(ROW_SLABS):
        y = pv[:, 0:1] * _slab(buf_ref.at[slot, 0], s, tt) + pv[:, 1:2] * _slab(buf_ref.at[slot, 1], s, tt)
        slabs.append(DEEPNORM_ALPHA * x_ref[:, s * LANES:(s + 1) * LANES] + y)
    mu = sum(jnp.sum(r, axis=-1, keepdims=True) for r in slabs) * (1.0 / D_MODEL)
    var = sum(jnp.sum((r - mu) * (r - mu), axis=-1, keepdims=True) for r in slabs) * (1.0 / D_MODEL)
    inv = lax.rsqrt(var + LN_EPS)
    for s in range(ROW_SLABS):
        cs = slice(s * LANES, (s + 1) * LANES)
        o_ref[:, cs] = (slabs[s] - mu) * inv * g_ref[:, cs] + b_ref[:, cs]


def combine(dest, x, pv, g, b, yb3, tt, first_block, n_blocks):
    grid_spec = pltpu.PrefetchScalarGridSpec(
        num_scalar_prefetch=1,
        grid=(n_blocks,),
        in_specs=[pl.BlockSpec((tt, D_MODEL), lambda i, dst: (i + first_block, 0)),
                  pl.BlockSpec((tt, TOP_K), lambda i, dst: (i + first_block, 0)),
                  pl.BlockSpec(g.shape, lambda i, dst: (0, 0)),
                  pl.BlockSpec(b.shape, lambda i, dst: (0, 0)),
                  pl.BlockSpec(memory_space=pl.ANY)],
        out_specs=pl.BlockSpec((tt, D_MODEL), lambda i, dst: (i, 0)),
        scratch_shapes=[pltpu.VMEM((2, TOP_K, tt * ROW_SLABS, LANES), F32), pltpu.SemaphoreType.DMA((2, TOP_K))],
    )
    return pl.pallas_call(
        functools.partial(_combine_kernel, first_block=first_block),
        grid_spec=grid_spec,
        out_shape=jax.ShapeDtypeStruct((n_blocks * tt, D_MODEL), F32),
        compiler_params=_cparams(("arbitrary",)),
        name="moe_combine",
    )(dest, x, pv, g, b, yb3)


def moe_experts(xb, topi, w1, w3, w2, tm):
    n = topi.shape[0]
    eid = topi.reshape(-1)
    onehot = (eid[:, None] == jnp.arange(N_EXPERTS, dtype=jnp.int32)[None, :]).astype(jnp.int32)
    csum = jnp.cumsum(onehot, axis=0)
    rank = jnp.sum((csum - onehot) * onehot, axis=1)
    counts = csum[-1]
    padded = (counts + tm - 1) // tm * tm
    pends = jnp.cumsum(padded)
    pstarts = pends - padded
    dest = (pstarts[eid] + rank).astype(jnp.int32)
    nb = (n * TOP_K + tm - 1) // tm + N_EXPERTS
    n_rows = nb * tm
    block_start = jnp.arange(nb, dtype=jnp.int32) * tm
    block_e = jnp.minimum(jnp.searchsorted(pends, block_start, side="right"), N_EXPERTS - 1).astype(jnp.int32)
    n_used = (pends[-1] // tm).astype(jnp.int32).reshape(1)
    n_chunks = n // TOKEN_CHUNK
    chunk_end = csum[TOKEN_CHUNK * TOP_K - 1::TOKEN_CHUNK * TOP_K]
    ends = chunk_end[:, block_e]
    first_entry = block_start - pstarts[block_e]
    last_entry = jnp.minimum(first_entry + tm, counts[block_e]) - 1
    chunk0 = jnp.minimum(jnp.sum(ends <= first_entry[None, :], axis=0), n_chunks - 1).astype(jnp.int32)
    chunk1 = jnp.minimum(jnp.sum(ends <= last_entry[None, :], axis=0), n_chunks - 1).astype(jnp.int32)
    chunk_cnt = jnp.maximum(chunk1 - chunk0 + 1, 1).astype(jnp.int32)
    dest_t = dest.reshape(n, TOP_K).T

    yb3 = swiglu_experts(block_e, n_used, chunk0, chunk_cnt, first_entry.astype(jnp.int32),
                         chunk_end.reshape(-1).astype(jnp.int32), xb, dest_t, w1, w3, w2, n_rows, tm)
    return yb3, dest


def kernel(x_prompt, x_sample, state_conv_a, state_conv_b, state_delta, meta_tokens, ln_g, ln_b, sc_w_in, sc_conv, sc_w_out, ffn_w1, ffn_w3, ffn_w2, gd_w_in, gd_conv, gd_a_log, gd_dt_bias, gd_norm_g, gd_w_out, moe_router, moe_w1, moe_w3, moe_w2):
    bsz, seq, d = x_prompt.shape
    n_s = x_sample.shape[0]
    n_small = N_META + n_s
    n_p = bsz * seq

    def row(v):
        return v.reshape(1, -1)

    sc_w_in_b = sc_w_in[0].astype(BF16)
    sc_w_out_b = sc_w_out[0].astype(BF16)
    ffn_w1_b, ffn_w3_b, ffn_w2_b = ffn_w1.astype(BF16), ffn_w3.astype(BF16), ffn_w2.astype(BF16)
    gd_w_qkv_b = gd_w_in[0][:, :GD_CONV_CH].astype(BF16)
    gd_w_z_b = gd_w_in[0][:, GD_CONV_CH:GD_CONV_CH + GD_V_DIM].astype(BF16)
    gd_w_ba_b = gd_w_in[0][:, GD_CONV_CH + GD_V_DIM:].astype(BF16)
    gd_w_out_b = gd_w_out[0].astype(BF16)
    moe_w1_b, moe_w3_b, moe_w2_b = moe_w1[0].astype(BF16), moe_w3[0].astype(BF16), moe_w2[0].astype(BF16)

    x_small = jnp.concatenate([meta_tokens.astype(F32), x_sample.reshape(n_s, d)], axis=0)

    xa_s, ch_s = l0_mix_small(x_small, state_conv_a[0, :, 0], state_conv_a[0, :, 1], sc_w_in_b, sc_w_out_b,
                              sc_conv[0], row(ln_g[0, 0]), row(ln_b[0, 0]))
    xa_p, tail_a = l0_mix_prompt(x_prompt, sc_w_in_b, sc_w_out_b, sc_conv[0], row(ln_g[0, 0]), row(ln_b[0, 0]),
                                 ch_s[N_META - HALO:N_META])
    def dense_ffn(x, tm):
        return swiglu_ln(x, ffn_w1_b[0], ffn_w3_b[0], ffn_w2_b[0], row(ln_g[0, 1]), row(ln_b[0, 1]), tm)

    xb_s = dense_ffn(xa_s, n_small)
    xb_p = dense_ffn(xa_p.reshape(n_p, d), 512)

    def gdn_inproj(x, tm):
        qkv = matmul(x, gd_w_qkv_b, F32, tm, 2048)
        z = matmul(x, gd_w_z_b, BF16, tm, 2048)
        ba = matmul(x, gd_w_ba_b, F32, tm, 2 * GD_HV)
        return qkv, z, ba

    qkv_s, z_s, ba_s = gdn_inproj(xb_s, n_small)
    qkv_p, z_p, ba_p = gdn_inproj(xb_p, 1024)

    pad = CHUNK - N_META

    def meta_pad(a):
        return jnp.pad(a[:N_META], ((0, pad), (0, 0)))[None]

    ba_m = meta_pad(ba_s)
    o_m, _, s_meta = gdn_chunked(meta_pad(qkv_s), meta_pad(z_s), ba_m, jnp.swapaxes(ba_m, 1, 2), gd_conv[0],
                                 gd_a_log[0], gd_dt_bias[0], gd_norm_g[0],
                                 jnp.zeros((HALO, GD_CONV_CH), F32), jnp.zeros((GD_HV, GD_DH, GD_DH), F32),
                                 CHUNK, valid_rows=N_META)
    o_smp, s_smp = gdn_step(qkv_s[N_META:], state_conv_b[0], z_s[N_META:], ba_s[N_META:], gd_conv[0],
                            gd_a_log[0], gd_dt_bias[0], gd_norm_g[0], state_delta[0])
    ba_p3 = ba_p.reshape(bsz, seq, 2 * GD_HV)
    o_p, tail_b, s_p = gdn_chunked(qkv_p.reshape(bsz, seq, GD_CONV_CH), z_p.reshape(bsz, seq, GD_V_DIM), ba_p3,
                                   jnp.swapaxes(ba_p3, 1, 2), gd_conv[0], gd_a_log[0], gd_dt_bias[0],
                                   gd_norm_g[0], qkv_s[N_META - HALO:N_META], s_meta[0], 256)
    tt = TOKEN_BLOCK
    small_pad = ((0, tt - n_small), (0, 0))
    o_s = jnp.pad(jnp.concatenate([o_m[0, :N_META], o_smp], axis=0), small_pad)
    xc, xc_b, topi, topv = matmul_ln_router(o_p.reshape(n_p, GD_V_DIM), xb_p, o_s, jnp.pad(xb_s, small_pad),
                                            gd_w_out_b, row(ln_g[1, 0]), row(ln_b[1, 0]), moe_router[0])
    yb3, dest = moe_experts(xc_b, topi, moe_w1_b, moe_w3_b, moe_w2_b, EXPERT_BLOCK)
    xd_p = combine(dest, xc, topv, row(ln_g[1, 1]), row(ln_b[1, 1]), yb3, tt, 0, n_p // tt)
    xd_s = combine(dest, xc, topv, row(ln_g[1, 1]), row(ln_b[1, 1]), yb3, tt, n_p // tt, 1)

    y_prompt = xd_p.reshape(bsz, seq, d)
    y_sample = xd_s[N_META:n_small].reshape(n_s, 1, d)
    new_conv_a_prompt = tail_a[None, :, HALO - (SC_WIDTH - 1):]
    new_conv_b_prompt = tail_b[None, :, HALO - (GD_CONV - 1):]
    new_delta_prompt = s_p[None]
    new_conv_a_sample = jnp.stack([state_conv_a[0, :, 1], ch_s[N_META:]], axis=1)[None]
    new_conv_b_sample = jnp.concatenate([state_conv_b[0, :, 1:], qkv_s[N_META:, None]], axis=1)[None]
    new_delta_sample = s_smp[None]
    return (y_prompt, y_sample, new_conv_a_prompt, new_conv_b_prompt, new_delta_prompt,
            new_conv_a_sample, new_conv_b_sample, new_delta_sample)
```

```python
import functools

import jax
import jax.numpy as jnp
from jax import lax
from jax.experimental import pallas as pl
from jax.experimental.pallas import tpu as pltpu

F32 = jnp.float32
BF16 = jnp.bfloat16

D_MODEL = 1024
N_META = 16
SC_WIDTH = 3
GD_HK = 8
GD_HV = 16
GD_DH = 128
GD_K_DIM = GD_HK * GD_DH
GD_V_DIM = GD_HV * GD_DH
GD_CONV = 4
GD_CONV_CH = 2 * GD_K_DIM + GD_V_DIM
D_FF = 3584
N_EXPERTS = 8
TOP_K = 2
LN_EPS = 1e-5
RMS_EPS = 1e-6
DEPTH = 2
DEEPNORM_ALPHA = (2 * DEPTH) ** 0.25

LANES = 128
ROW_SLABS = D_MODEL // LANES
DMA_UNROLL = 8
EXPERT_F_STEPS = 2
TOKEN_CHUNK = 256
CHUNK_RING = 12
SEL_ALIGN = 16
SEL_ROWS = TOKEN_CHUNK + SEL_ALIGN
TOKEN_BLOCK = 256
EXPERT_BLOCK = 512
CHUNK = 64
HALO = 8
V7X_VMEM_LIMIT = 56 * 1024 * 1024


def _cparams(sem, vmem=V7X_VMEM_LIMIT):
    return pltpu.CompilerParams(dimension_semantics=sem, vmem_limit_bytes=vmem)


def _bdot(a, b):
    return jnp.dot(a.astype(BF16), b.astype(BF16), preferred_element_type=F32)


def _bdot_nt(a, b):
    return lax.dot_general(a.astype(BF16), b.astype(BF16), (((1,), (1,)), ((), ())),
                           preferred_element_type=F32)


def _bdot_tn(a, b):
    return lax.dot_general(a.astype(BF16), b.astype(BF16), (((0,), (0,)), ((), ())),
                           preferred_element_type=F32)


def _bf16_pieces(x):
    p0 = x.astype(BF16)
    r1 = x - p0.astype(F32)
    p1 = r1.astype(BF16)
    p2 = (r1 - p1.astype(F32)).astype(BF16)
    return p0, p1, p2


def _dot_f32x3(a, b):
    a_hi = a.astype(BF16)
    b_hi = b.astype(BF16)
    a_lo = (a - a_hi.astype(F32)).astype(BF16)
    b_lo = (b - b_hi.astype(F32)).astype(BF16)
    return (jnp.dot(a_hi, b_hi, preferred_element_type=F32) + jnp.dot(a_lo, b_hi, preferred_element_type=F32)
            + jnp.dot(a_hi, b_lo, preferred_element_type=F32))


def _layer_norm(r, g, b):
    mu = jnp.mean(r, axis=-1, keepdims=True)
    d = r - mu
    var = jnp.mean(d * d, axis=-1, keepdims=True)
    return d * lax.rsqrt(var + LN_EPS) * g + b


def _silu(x):
    return x * jax.nn.sigmoid(x)


def _l0_inproj(xb, w_in_ref, ch_ref, bg_ref, row0, rows):
    col_chunk = 512
    for j in range(D_MODEL // col_chunk):
        lo, hi = j * col_chunk, (j + 1) * col_chunk
        bg = jnp.dot(xb, w_in_ref[:, lo:hi], preferred_element_type=F32)
        c = jnp.dot(xb, w_in_ref[:, D_MODEL + lo:D_MODEL + hi], preferred_element_type=F32)
        h = jnp.dot(xb, w_in_ref[:, 2 * D_MODEL + lo:2 * D_MODEL + hi], preferred_element_type=F32)
        ch_ref[row0:row0 + rows, lo:hi] = c * h
        bg_ref[:, lo:hi] = bg


def _l0_prompt_kernel(x_ref, w_in_ref, w_out_ref, cw_ref, g_ref, b_ref, carry_ref,
                      o_ref, tail_ref, buf_ref, bg_ref):
    tm = x_ref.shape[0]
    t = pl.program_id(1)

    @pl.when(t == 0)
    def _():
        buf_ref[0:HALO, :] = carry_ref[...]

    x = x_ref[...]
    _l0_inproj(x.astype(BF16), w_in_ref, buf_ref, bg_ref, HALO, tm)
    y = (cw_ref[0:1, :] * buf_ref[HALO - 2:HALO - 2 + tm, :]
         + cw_ref[1:2, :] * buf_ref[HALO - 1:HALO - 1 + tm, :]
         + cw_ref[2:3, :] * buf_ref[HALO:HALO + tm, :])
    u = (bg_ref[...] * y).astype(BF16)
    m = jnp.dot(u, w_out_ref[...], preferred_element_type=F32)
    o_ref[...] = _layer_norm(DEEPNORM_ALPHA * x + m, g_ref[...], b_ref[...])
    tail = buf_ref[tm:tm + HALO, :]
    tail_ref[...] = tail
    buf_ref[0:HALO, :] = tail


def _l0_small_kernel(x_ref, st0_ref, st1_ref, w_in_ref, w_out_ref, cw_ref, g_ref, b_ref,
                     o_ref, ch_out_ref, buf_ref, bg_ref):
    n = x_ref.shape[0]
    x = x_ref[...]
    buf_ref[0:HALO, :] = jnp.zeros((HALO, D_MODEL), F32)
    _l0_inproj(x.astype(BF16), w_in_ref, buf_ref, bg_ref, HALO, n)
    y_meta = (cw_ref[0:1, :] * buf_ref[HALO - 2:HALO - 2 + N_META, :]
              + cw_ref[1:2, :] * buf_ref[HALO - 1:HALO - 1 + N_META, :]
              + cw_ref[2:3, :] * buf_ref[HALO:HALO + N_META, :])
    ch = buf_ref[HALO:HALO + n, :]
    y_s = (cw_ref[0:1, :] * st0_ref[...] + cw_ref[1:2, :] * st1_ref[...]
           + cw_ref[2:3, :] * ch[N_META:, :])
    y = jnp.concatenate([y_meta, y_s], axis=0)
    u = (bg_ref[...] * y).astype(BF16)
    m = jnp.dot(u, w_out_ref[...], preferred_element_type=F32)
    o_ref[...] = _layer_norm(DEEPNORM_ALPHA * x + m, g_ref[...], b_ref[...])
    ch_out_ref[...] = ch


def _full(shape):
    nd = len(shape)
    return pl.BlockSpec(shape, lambda *_: (0,) * nd)


def l0_mix_prompt(x, w_in, w_out, cw, g, b, carry, tm=512):
    bsz, seq, d = x.shape
    return pl.pallas_call(
        _l0_prompt_kernel,
        grid=(bsz, seq // tm),
        in_specs=[pl.BlockSpec((None, tm, d), lambda i, t: (i, t, 0)),
                  _full(w_in.shape), _full(w_out.shape), _full(cw.shape), _full(g.shape), _full(b.shape),
                  _full(carry.shape)],
        out_specs=[pl.BlockSpec((None, tm, d), lambda i, t: (i, t, 0)),
                   pl.BlockSpec((None, HALO, d), lambda i, t: (i, 0, 0))],
        out_shape=[jax.ShapeDtypeStruct((bsz, seq, d), F32),
                   jax.ShapeDtypeStruct((bsz, HALO, d), F32)],
        scratch_shapes=[pltpu.VMEM((tm + HALO, d), F32), pltpu.VMEM((tm, d), F32)],
        compiler_params=_cparams(("arbitrary", "arbitrary")),
        name="l0_mix_prompt",
    )(x, w_in, w_out, cw, g, b, carry)


def l0_mix_small(x, st0, st1, w_in, w_out, cw, g, b):
    n, d = x.shape
    return pl.pallas_call(
        _l0_small_kernel,
        grid=(1,),
        in_specs=[_full(a.shape) for a in (x, st0, st1, w_in, w_out, cw, g, b)],
        out_specs=[_full((n, d)), _full((n, d))],
        out_shape=[jax.ShapeDtypeStruct((n, d), F32), jax.ShapeDtypeStruct((n, d), F32)],
        scratch_shapes=[pltpu.VMEM((n + HALO, d), F32), pltpu.VMEM((n, d), F32)],
        compiler_params=_cparams(("arbitrary",)),
        name="l0_mix_small",
    )(x, st0, st1, w_in, w_out, cw, g, b)


def _slab(ref, s, rows):
    return ref[pl.ds(s, rows, stride=ROW_SLABS), :]


def _to_slabs(o_ref, val):
    rows = val.shape[0]
    for s in range(ROW_SLABS):
        o_ref[pl.ds(s, rows, stride=ROW_SLABS), :] = val[:, s * LANES:(s + 1) * LANES]


def _swiglu_partial(xb, w1_ref, w3_ref, w2_ref, acc_ref):
    a = jnp.dot(xb, w1_ref[...], preferred_element_type=F32)
    b = jnp.dot(xb, w3_ref[...], preferred_element_type=F32)
    h = (_silu(a) * b).astype(BF16)
    part = jnp.dot(h, w2_ref[...], preferred_element_type=F32)
    acc_ref[...] = jnp.where(pl.program_id(1) > 0, acc_ref[...], 0.0) + part


def _zero_acc_once(acc_ref):
    @pl.when((pl.program_id(0) == 0) & (pl.program_id(1) == 0))
    def _():
        acc_ref[...] = jnp.zeros(acc_ref.shape, acc_ref.dtype)


def _swiglu_ln_kernel(x_ref, w1_ref, w3_ref, w2_ref, g_ref, b_ref, o_ref, acc_ref):
    _zero_acc_once(acc_ref)
    _swiglu_partial(x_ref[...].astype(BF16), w1_ref, w3_ref, w2_ref, acc_ref)

    @pl.when(pl.program_id(1) == pl.num_programs(1) - 1)
    def _():
        o_ref[...] = _layer_norm(DEEPNORM_ALPHA * x_ref[...] + acc_ref[...], g_ref[...], b_ref[...])


def swiglu_ln(x, w1, w3, w2, g, b, tm, tf=1792):
    n, d = x.shape
    return pl.pallas_call(
        _swiglu_ln_kernel,
        grid=(n // tm, D_FF // tf),
        in_specs=[pl.BlockSpec((tm, d), lambda i, f: (i, 0)),
                  pl.BlockSpec((d, tf), lambda i, f: (0, f)),
                  pl.BlockSpec((d, tf), lambda i, f: (0, f)),
                  pl.BlockSpec((tf, d), lambda i, f: (f, 0)),
                  _full(g.shape), _full(b.shape)],
        out_specs=pl.BlockSpec((tm, d), lambda i, f: (i, 0)),
        out_shape=jax.ShapeDtypeStruct((n, d), F32),
        scratch_shapes=[pltpu.VMEM((tm, d), F32)],
        compiler_params=_cparams(("arbitrary", "arbitrary")),
        name="swiglu_ln",
    )(x, w1, w3, w2, g, b)


def _token_chunk_copy(x_hbm, chunk, buf_ref, slot, sem):
    start = pl.multiple_of(chunk * TOKEN_CHUNK, TOKEN_CHUNK)
    return pltpu.make_async_copy(x_hbm.at[pl.ds(start, TOKEN_CHUNK)], buf_ref.at[slot], sem.at[slot])


def _swiglu_experts_kernel(be_ref, nu_ref, c0_ref, cn_ref, e0_ref, cend_ref, x_hbm, dest_ref, w1_ref, w3_ref,
                           w2_ref, o_ref, acc_ref, xsel_ref, xb_ref, chunk_ref, sem):
    tm = xb_ref.shape[0]
    i = pl.program_id(0)
    f = pl.program_id(1)
    nf = pl.num_programs(1)
    n_used = nu_ref[0]
    _zero_acc_once(acc_ref)

    def start_ring(blk):
        for slot in range(CHUNK_RING):
            @pl.when(slot < cn_ref[blk])
            def _():
                _token_chunk_copy(x_hbm, c0_ref[blk] + slot, chunk_ref, slot, sem).start()

    @pl.when(i < n_used)
    def _():
        @pl.when(f == 0)
        def _():
            first = c0_ref[i]
            count = cn_ref[i]
            expert = be_ref[i]
            row_iota = lax.broadcasted_iota(jnp.int32, (SEL_ROWS, TOKEN_CHUNK), 0)

            @pl.when(i == 0)
            def _():
                start_ring(0)

            xsel_ref[...] = jnp.zeros(xsel_ref.shape, xsel_ref.dtype)

            def pick(j, carry):
                slot = jnp.where(j < CHUNK_RING, j, 0)

                @pl.when(j >= CHUNK_RING)
                def _():
                    _token_chunk_copy(x_hbm, first + j, chunk_ref, 0, sem).start()

                chunk = first + j
                before = jnp.where(chunk > 0, cend_ref[jnp.maximum(chunk - 1, 0) * N_EXPERTS + expert], 0)
                row0 = jnp.clip(before - e0_ref[i], 0, tm - SEL_ROWS) // SEL_ALIGN * SEL_ALIGN
                row0 = pl.multiple_of(row0, SEL_ALIGN)
                row_id = row_iota + (i * tm + row0)
                _token_chunk_copy(x_hbm, chunk, chunk_ref, slot, sem).wait()
                lane0 = pl.multiple_of(chunk * TOKEN_CHUNK, TOKEN_CHUNK)
                hit = row_id == dest_ref[0:1, pl.ds(lane0, TOKEN_CHUNK)]
                for k in range(1, TOP_K):
                    hit = hit | (row_id == dest_ref[k:k + 1, pl.ds(lane0, TOKEN_CHUNK)])
                sel = jnp.where(hit, 1.0, 0.0).astype(BF16)
                xsel_ref[pl.ds(row0, SEL_ROWS), :] += jnp.dot(sel, chunk_ref[slot], preferred_element_type=F32)
                return carry

            lax.fori_loop(0, count, pick, 0)
            xb_ref[...] = xsel_ref[...].astype(BF16)

        @pl.when((f == nf - 1) & (i + 1 < n_used))
        def _():
            start_ring(i + 1)

        _swiglu_partial(xb_ref[...], w1_ref, w3_ref, w2_ref, acc_ref)

        @pl.when(f == nf - 1)
        def _():
            _to_slabs(o_ref, acc_ref[...])

    @pl.when(i >= nu_ref[0])
    def _():
        o_ref[...] = jnp.zeros(o_ref.shape, o_ref.dtype)


def swiglu_experts(block_e, n_used, chunk0, n_chunks, entry0, chunk_end, xb, dest_t, w1, w3, w2, n_rows, tm):
    d = D_MODEL
    nf = EXPERT_F_STEPS
    tf = D_FF // nf

    def f_idx(i, f, nu):
        return jnp.where(i < nu[0], f, nf - 1)

    grid_spec = pltpu.PrefetchScalarGridSpec(
        num_scalar_prefetch=6,
        grid=(n_rows // tm, nf),
        in_specs=[pl.BlockSpec(memory_space=pl.ANY),
                  pl.BlockSpec(dest_t.shape, lambda i, f, be, nu, c0, cn, e0, ce: (0, 0)),
                  pl.BlockSpec((None, d, tf), lambda i, f, be, nu, c0, cn, e0, ce: (be[i], 0, f_idx(i, f, nu))),
                  pl.BlockSpec((None, d, tf), lambda i, f, be, nu, c0, cn, e0, ce: (be[i], 0, f_idx(i, f, nu))),
                  pl.BlockSpec((None, tf, d), lambda i, f, be, nu, c0, cn, e0, ce: (be[i], f_idx(i, f, nu), 0))],
        out_specs=pl.BlockSpec((tm * ROW_SLABS, LANES), lambda i, f, be, nu, c0, cn, e0, ce: (i, 0)),
        scratch_shapes=[pltpu.VMEM((tm, d), F32),
                        pltpu.VMEM((tm, d), F32),
                        pltpu.VMEM((tm, d), BF16),
                        pltpu.VMEM((CHUNK_RING, TOKEN_CHUNK, d), BF16),
                        pltpu.SemaphoreType.DMA((CHUNK_RING,))],
    )
    return pl.pallas_call(
        _swiglu_experts_kernel,
        grid_spec=grid_spec,
        out_shape=jax.ShapeDtypeStruct((n_rows * ROW_SLABS, LANES), F32),
        compiler_params=_cparams(("arbitrary", "arbitrary")),
        name="swiglu_experts",
    )(block_e, n_used, chunk0, n_chunks, entry0, chunk_end, xb, dest_t, w1, w3, w2)


def _mm_kernel(x_ref, w_ref, o_ref):
    o_ref[...] = jnp.dot(x_ref[...].astype(BF16), w_ref[...], preferred_element_type=F32).astype(o_ref.dtype)


def matmul(x, w, out_dtype, tm, tn):
    n, k = x.shape
    _, m = w.shape
    return pl.pallas_call(
        _mm_kernel,
        grid=(n // tm, m // tn),
        in_specs=[pl.BlockSpec((tm, k), lambda i, j: (i, 0)),
                  pl.BlockSpec((k, tn), lambda i, j: (0, j))],
        out_specs=pl.BlockSpec((tm, tn), lambda i, j: (i, j)),
        out_shape=jax.ShapeDtypeStruct((n, m), out_dtype),
        compiler_params=_cparams(("arbitrary", "arbitrary")),
        name="matmul",
    )(x, w)


def _top2(logits):
    mx = jnp.max(logits, axis=-1, keepdims=True)
    ex = jnp.exp(logits - mx)
    p = ex / jnp.sum(ex, axis=-1, keepdims=True)
    lane = lax.broadcasted_iota(jnp.int32, p.shape, 1)
    p1 = jnp.max(p, axis=-1, keepdims=True)
    i1 = jnp.min(jnp.where(p == p1, lane, N_EXPERTS), axis=-1, keepdims=True)
    rest = jnp.where(lane == i1, -1.0, p)
    p2 = jnp.max(rest, axis=-1, keepdims=True)
    i2 = jnp.min(jnp.where(rest == p2, lane, N_EXPERTS), axis=-1, keepdims=True)
    tot = p1 + p2
    return jnp.concatenate([i1, i2], axis=-1), jnp.concatenate([p1 / tot, p2 / tot], axis=-1)


def _mm_ln_router_kernel(a_ref, x_ref, a_tail_ref, x_tail_ref, w_ref, g_ref, b_ref, wr_ref,
                         o_ref, ob_ref, idx_ref, val_ref, *, n_main):
    def block(a_blk, x_blk):
        m = jnp.dot(a_blk[...], w_ref[...], preferred_element_type=F32)
        y = _layer_norm(DEEPNORM_ALPHA * x_blk[...] + m, g_ref[...], b_ref[...])
        o_ref[...] = y
        ob_ref[...] = y.astype(BF16)
        logits = _dot_f32x3(y, wr_ref[...])
        idx_ref[...], val_ref[...] = _top2(logits)

    @pl.when(pl.program_id(0) < n_main)
    def _():
        block(a_ref, x_ref)

    @pl.when(pl.program_id(0) >= n_main)
    def _():
        block(a_tail_ref, x_tail_ref)


def matmul_ln_router(a, x, a_tail, x_tail, w, g, b, wr):
    tm, k = a_tail.shape
    n = a.shape[0]
    d = w.shape[1]
    n_main = n // tm
    n_all = n + tm

    def main(i):
        return (jnp.minimum(i, n_main - 1), 0)

    return pl.pallas_call(
        functools.partial(_mm_ln_router_kernel, n_main=n_main),
        grid=(n_main + 1,),
        in_specs=[pl.BlockSpec((tm, k), main), pl.BlockSpec((tm, d), main),
                  _full(a_tail.shape), _full(x_tail.shape),
                  _full(w.shape), _full(g.shape), _full(b.shape), _full(wr.shape)],
        out_specs=[pl.BlockSpec((tm, d), lambda i: (i, 0)), pl.BlockSpec((tm, d), lambda i: (i, 0)),
                   pl.BlockSpec((tm, TOP_K), lambda i: (i, 0)), pl.BlockSpec((tm, TOP_K), lambda i: (i, 0))],
        out_shape=[jax.ShapeDtypeStruct((n_all, d), F32), jax.ShapeDtypeStruct((n_all, d), BF16),
                   jax.ShapeDtypeStruct((n_all, TOP_K), jnp.int32), jax.ShapeDtypeStruct((n_all, TOP_K), F32)],
        compiler_params=_cparams(("arbitrary",)),
        name="matmul_ln_router",
    )(a, x, a_tail, x_tail, w, g, b, wr)


def _l2norm_heads(x, n_heads, scale):
    outs = []
    for h in range(n_heads):
        xh = x[:, h * GD_DH:(h + 1) * GD_DH]
        ss = jnp.sum(xh * xh, axis=-1, keepdims=True)
        outs.append(xh * (lax.rsqrt(ss + RMS_EPS) * scale))
    return outs


def _softplus(x):
    return jnp.maximum(x, 0.0) + jnp.log1p(jnp.exp(-jnp.abs(x)))


def _gated_rmsnorm(o, z, norm_g):
    ms = jnp.mean(o * o, axis=-1, keepdims=True)
    return o * lax.rsqrt(ms + RMS_EPS) * norm_g * _silu(z)


PAIR = GD_HV // GD_HK
PAIR_SQ = PAIR * CHUNK
PAIR_DV = PAIR * GD_DH


def _sq_cols(x, p):
    lane = lax.broadcasted_iota(jnp.int32, (CHUNK, PAIR_SQ), 1)
    a = jnp.broadcast_to(x[:, PAIR * p:PAIR * p + 1], (CHUNK, PAIR_SQ))
    b = jnp.broadcast_to(x[:, PAIR * p + 1:PAIR * p + 2], (CHUNK, PAIR_SQ))
    return jnp.where(lane < CHUNK, a, b)


def _block_diag_sq(m):
    lane = lax.broadcasted_iota(jnp.int32, m.shape, 1)
    top = jnp.where(lane < CHUNK, m, 0.0).astype(BF16)
    bot = jnp.where(lane < CHUNK, 0.0, m).astype(BF16)
    return jnp.concatenate([top, bot], axis=0)


def _block_diag_dv(m):
    mb = m.astype(BF16)
    zero = jnp.zeros((CHUNK, GD_DH), BF16)
    top = jnp.concatenate([mb[:, :GD_DH], zero], axis=1)
    bot = jnp.concatenate([zero, mb[:, GD_DH:]], axis=1)
    return jnp.concatenate([top, bot], axis=0)


def _tri_inverse_pairs(a_list):
    row = lax.broadcasted_iota(jnp.int32, (CHUNK, PAIR_SQ), 0)
    col = lax.broadcasted_iota(jnp.int32, (CHUNK, PAIR_SQ), 1) % CHUNK
    eye = jnp.where(row == col, 1.0, 0.0).astype(F32)
    s = 1
    inv = None
    while s < CHUNK:
        sub = ((row // (2 * s)) == (col // (2 * s))) & ((row // s) % 2 == 1) & ((col // s) % 2 == 0)
        e_list = [jnp.where(sub, a, 0.0) for a in a_list]
        if s == 1:
            inv = [eye - e for e in e_list]
        else:
            x_list = [jnp.dot(e.astype(BF16), _block_diag_sq(d), preferred_element_type=F32)
                      for e, d in zip(e_list, inv)]
            y_list = [jnp.dot(d.astype(BF16), _block_diag_sq(x), preferred_element_type=F32)
                      for d, x in zip(inv, x_list)]
            inv = [d - y for d, y in zip(inv, y_list)]
        s *= 2
    return inv


def _gdn_chunk_kernel(qkv_ref, z_ref, ba_ref, bat_ref, cw_ref, alog_ref, dtb_ref, alogt_ref, dtbt_ref,
                      ng_ref, carry_ref, s0_ref, o_ref, tail_ref, sout_ref,
                      buf_ref, s_ref, q_ref, k_ref, v_ref, lhs_w_ref, lhs_o_ref, glast_ref,
                      *, valid_rows):
    tc = qkv_ref.shape[0]
    t = pl.program_id(1)

    @pl.when(t == 0)
    def _():
        for c in range(GD_CONV_CH // LANES):
            buf_ref[c, 0:HALO, :] = carry_ref[:, c * LANES:(c + 1) * LANES]
        for h in range(GD_HV):
            j = h % PAIR
            s_ref[h // PAIR, :, j * GD_DH:(j + 1) * GD_DH] = s0_ref[h]

    for c in range(GD_CONV_CH // LANES):
        lo, hi = c * LANES, (c + 1) * LANES
        buf_ref[c, HALO:HALO + tc, :] = qkv_ref[:, lo:hi]
        y = cw_ref[GD_CONV - 1:GD_CONV, lo:hi] * qkv_ref[:, lo:hi]
        for i in range(GD_CONV - 1):
            off = HALO - (GD_CONV - 1) + i
            y = y + cw_ref[i:i + 1, lo:hi] * buf_ref[c, off:off + tc, :]
        act = _silu(y)
        if lo < GD_K_DIM:
            q_ref[:, lo:hi] = _l2norm_heads(act, 1, GD_DH ** -0.5)[0].astype(BF16)
        elif lo < 2 * GD_K_DIM:
            k_ref[:, lo - GD_K_DIM:hi - GD_K_DIM] = _l2norm_heads(act, 1, 1.0)[0].astype(BF16)
        else:
            v_ref[:, lo - 2 * GD_K_DIM:hi - 2 * GD_K_DIM] = act
        tail = buf_ref[c, tc:tc + HALO, :]
        tail_ref[:, lo:hi] = tail
        buf_ref[c, 0:HALO, :] = tail

    row = lax.broadcasted_iota(jnp.int32, (CHUNK, CHUNK), 0)
    col = lax.broadcasted_iota(jnp.int32, (CHUNK, CHUNK), 1)
    tril_ones = jnp.where(row >= col, 1.0, 0.0).astype(BF16)
    triu_ones = jnp.where(row <= col, 1.0, 0.0).astype(BF16)
    row2 = lax.broadcasted_iota(jnp.int32, (CHUNK, PAIR_SQ), 0)
    col2 = lax.broadcasted_iota(jnp.int32, (CHUNK, PAIR_SQ), 1) % CHUNK
    incl2 = row2 >= col2
    strict2 = row2 > col2
    n_pairs = GD_HK

    eye2 = jnp.where(row2 == col2, 1.0, 0.0).astype(F32)
    n_chunks = tc // CHUNK

    def pair_rows(xt, p):
        return jnp.concatenate([xt[PAIR * p + j:PAIR * p + j + 1, :] for j in range(PAIR)], axis=1)

    items = [(c, p) for c in range(n_chunks) for p in range(n_pairs)]
    cums, cumts, betas, betats = [], [], [], []
    for c in range(n_chunks):
        ba = ba_ref[c]
        bat = bat_ref[c]
        beta = jax.nn.sigmoid(ba[:, :GD_HV])
        betat = jax.nn.sigmoid(bat[:GD_HV, :])
        g = -jnp.exp(alog_ref[...]) * _softplus(ba[:, GD_HV:] + dtb_ref[...])
        gt = -jnp.exp(alogt_ref[...]) * _softplus(bat[GD_HV:, :] + dtbt_ref[...])
        if valid_rows is not None:
            rid = lax.broadcasted_iota(jnp.int32, (CHUNK, GD_HV), 0) + c * CHUNK + t * tc
            beta = jnp.where(rid < valid_rows, beta, 0.0)
            g = jnp.where(rid < valid_rows, g, 0.0)
            cid = lax.broadcasted_iota(jnp.int32, (GD_HV, CHUNK), 1) + c * CHUNK + t * tc
            betat = jnp.where(cid < valid_rows, betat, 0.0)
            gt = jnp.where(cid < valid_rows, gt, 0.0)
        cum = sum(jnp.dot(tril_ones, piece, preferred_element_type=F32) for piece in _bf16_pieces(g))
        cumt = sum(jnp.dot(piece, triu_ones, preferred_element_type=F32) for piece in _bf16_pieces(gt))
        glast_ref[c] = jnp.concatenate(
            [jnp.broadcast_to(jnp.exp(cum[CHUNK - 1:CHUNK, h:h + 1]), (1, GD_DH)) for h in range(GD_HV)], axis=1)
        cums.append(cum)
        cumts.append(cumt)
        betas.append(beta)
        betats.append(betat)

    grams = []
    for c, p in items:
        kb = k_ref[c * CHUNK:(c + 1) * CHUNK, p * GD_DH:(p + 1) * GD_DH]
        qb = q_ref[c * CHUNK:(c + 1) * CHUNK, p * GD_DH:(p + 1) * GD_DH]
        grams.append(lax.dot_general(jnp.concatenate([kb, qb], axis=0), jnp.concatenate([kb, kb], axis=0),
                                     (((1,), (1,)), ((), ())), preferred_element_type=F32))
    a_list, attn_list = [], []
    for (c, p), gram in zip(items, grams):
        diff = _sq_cols(cums[c], p) - pair_rows(cumts[c], p)
        dec = jnp.where(incl2, jnp.exp(jnp.where(incl2, diff, 0.0)), 0.0)
        attn_list.append(gram[CHUNK:] * dec)
        a_list.append(jnp.where(strict2, dec * gram[:CHUNK], 0.0) * _sq_cols(betas[c], p))
    inv_list = _tri_inverse_pairs(a_list)
    for (c, p), inv, attn in zip(items, inv_list, attn_list):
        cum_row = pair_rows(cumts[c], p)
        last = jnp.concatenate([jnp.broadcast_to(cumts[c][PAIR * p + j:PAIR * p + j + 1, CHUNK - 1:CHUNK], (1, CHUNK))
                                for j in range(PAIR)], axis=1)
        ecum_row = jnp.exp(cum_row)
        t_beta = inv * pair_rows(betats[c], p)
        lhs_w_ref[c, p] = jnp.concatenate([t_beta, -(t_beta * ecum_row)], axis=1).astype(BF16)
        top = jnp.concatenate([attn, eye2 * ecum_row], axis=1)
        bot = jnp.concatenate([eye2 * jnp.exp(last - cum_row), jnp.zeros((CHUNK, PAIR_SQ), F32)], axis=1)
        lhs_o_ref[c, p] = jnp.concatenate([top, bot], axis=0).astype(BF16)

    def state_body(c, carry):
        r0 = pl.multiple_of(c * CHUNK, CHUNK)
        kbs, s_old, kqs = [], [], []
        for p in range(n_pairs):
            kb = k_ref[pl.ds(r0, CHUNK), p * GD_DH:(p + 1) * GD_DH]
            qb = q_ref[pl.ds(r0, CHUNK), p * GD_DH:(p + 1) * GD_DH]
            s2 = s_ref[p]
            kbs.append(kb)
            s_old.append(s2)
            kqs.append(jnp.dot(jnp.concatenate([kb, qb], axis=0), s2.astype(BF16), preferred_element_type=F32))
        ws = []
        for p in range(n_pairs):
            v2 = v_ref[pl.ds(r0, CHUNK), p * PAIR_DV:(p + 1) * PAIR_DV]
            rhs = jnp.concatenate([_block_diag_dv(v2), _block_diag_dv(kqs[p][:CHUNK])], axis=0)
            ws.append(jnp.dot(lhs_w_ref[c, p], rhs, preferred_element_type=F32))
        for p in range(n_pairs):
            rhs = jnp.concatenate([_block_diag_dv(ws[p]), _block_diag_dv(kqs[p][CHUNK:])], axis=0)
            ow = jnp.dot(lhs_o_ref[c, p], rhs, preferred_element_type=F32)
            o2 = ow[:CHUNK]
            upd = lax.dot_general(kbs[p], ow[CHUNK:].astype(BF16), (((0,), (0,)), ((), ())),
                                  preferred_element_type=F32)
            s_ref[p] = glast_ref[c, :, p * PAIR_DV:(p + 1) * PAIR_DV] * s_old[p] + upd
            for j in range(PAIR):
                h = PAIR * p + j
                z = z_ref[pl.ds(r0, CHUNK), h * GD_DH:(h + 1) * GD_DH].astype(F32)
                o_ref[pl.ds(r0, CHUNK), h * GD_DH:(h + 1) * GD_DH] = _gated_rmsnorm(
                    o2[:, j * GD_DH:(j + 1) * GD_DH], z, ng_ref[...]).astype(o_ref.dtype)
        return carry

    lax.fori_loop(0, tc // CHUNK, state_body, 0)
    for h in range(GD_HV):
        j = h % PAIR
        sout_ref[h] = s_ref[h // PAIR, :, j * GD_DH:(j + 1) * GD_DH]


def gdn_chunked(qkv, z, ba, bat, cw, a_log, dt_bias, norm_g, carry, s0, tc, valid_rows=None):
    bsz, seq, _ = qkv.shape
    nchunk = tc // CHUNK
    ba4 = ba.reshape(bsz, seq // CHUNK, CHUNK, 2 * GD_HV)
    bat4 = bat.reshape(bsz, 2 * GD_HV, seq // CHUNK, CHUNK).transpose(0, 2, 1, 3)
    alog = a_log.reshape(1, GD_HV)
    dtb = dt_bias.reshape(1, GD_HV)
    alogt = a_log.reshape(GD_HV, 1)
    dtbt = dt_bias.reshape(GD_HV, 1)
    ng = norm_g.reshape(1, GD_DH)
    return pl.pallas_call(
        functools.partial(_gdn_chunk_kernel, valid_rows=valid_rows),
        grid=(bsz, seq // tc),
        in_specs=[pl.BlockSpec((None, tc, GD_CONV_CH), lambda i, t: (i, t, 0)),
                  pl.BlockSpec((None, tc, GD_V_DIM), lambda i, t: (i, t, 0)),
                  pl.BlockSpec((None, nchunk, CHUNK, 2 * GD_HV), lambda i, t: (i, t, 0, 0)),
                  pl.BlockSpec((None, nchunk, 2 * GD_HV, CHUNK), lambda i, t: (i, t, 0, 0)),
                  _full(cw.shape), _full(alog.shape), _full(dtb.shape), _full(alogt.shape), _full(dtbt.shape),
                  _full(ng.shape), _full(carry.shape), _full(s0.shape)],
        out_specs=[pl.BlockSpec((None, tc, GD_V_DIM), lambda i, t: (i, t, 0)),
                   pl.BlockSpec((None, HALO, GD_CONV_CH), lambda i, t: (i, 0, 0)),
                   pl.BlockSpec((None, GD_HV, GD_DH, GD_DH), lambda i, t: (i, 0, 0, 0))],
        out_shape=[jax.ShapeDtypeStruct((bsz, seq, GD_V_DIM), BF16),
                   jax.ShapeDtypeStruct((bsz, HALO, GD_CONV_CH), F32),
                   jax.ShapeDtypeStruct((bsz, GD_HV, GD_DH, GD_DH), F32)],
        scratch_shapes=[pltpu.VMEM((GD_CONV_CH // LANES, tc + HALO, LANES), F32),
                        pltpu.VMEM((GD_HK, GD_DH, PAIR_DV), F32),
                        pltpu.VMEM((tc, GD_K_DIM), BF16), pltpu.VMEM((tc, GD_K_DIM), BF16),
                        pltpu.VMEM((tc, GD_V_DIM), F32),
                        pltpu.VMEM((nchunk, GD_HK, CHUNK, 2 * PAIR_SQ), BF16),
                        pltpu.VMEM((nchunk, GD_HK, 2 * CHUNK, 2 * PAIR_SQ), BF16),
                        pltpu.VMEM((nchunk, 1, GD_V_DIM), F32)],
        compiler_params=_cparams(("arbitrary", "arbitrary")),
        name="gdn_chunked",
    )(qkv, z, ba4, bat4, cw, alog, dtb, alogt, dtbt, ng, carry, s0)


def _gdn_step_kernel(cur_ref, st_ref, z_ref, ba_ref, cw_ref, alog_ref, dtb_ref, ng_ref, s_ref,
                     o_ref, sout_ref, oacc_ref):
    ns = cur_ref.shape[0]
    y = cw_ref[GD_CONV - 1:GD_CONV, :] * cur_ref[...]
    for i in range(GD_CONV - 1):
        y = y + cw_ref[i:i + 1, :] * st_ref[:, i, :]
    act = _silu(y)
    qh = _l2norm_heads(act[:, :GD_K_DIM], GD_HK, GD_DH ** -0.5)
    kh = _l2norm_heads(act[:, GD_K_DIM:2 * GD_K_DIM], GD_HK, 1.0)
    qk_t = jnp.concatenate(qh + kh, axis=0).T
    ba = ba_ref[...]
    beta = jax.nn.sigmoid(ba[:, :GD_HV])
    eg = jnp.exp(-jnp.exp(alog_ref[...]) * _softplus(ba[:, GD_HV:] + dtb_ref[...]))
    for s in range(ns):
        for h in range(GD_HV):
            g = h // (GD_HV // GD_HK)
            qcol = qk_t[:, g * ns + s:g * ns + s + 1]
            kcol = qk_t[:, (GD_HK + g) * ns + s:(GD_HK + g) * ns + s + 1]
            sd = s_ref[s, h] * eg[s:s + 1, h:h + 1]
            ks = jnp.sum(sd * kcol, axis=0, keepdims=True)
            v = act[s:s + 1, 2 * GD_K_DIM + h * GD_DH:2 * GD_K_DIM + (h + 1) * GD_DH]
            w = beta[s:s + 1, h:h + 1] * (v - ks)
            sn = sd + kcol * w
            sout_ref[s, h] = sn
            oacc_ref[s:s + 1, h * GD_DH:(h + 1) * GD_DH] = jnp.sum(sn * qcol, axis=0, keepdims=True)
    for h in range(GD_HV):
        lo, hi = h * GD_DH, (h + 1) * GD_DH
        o_ref[:, lo:hi] = _gated_rmsnorm(oacc_ref[:, lo:hi], z_ref[:, lo:hi].astype(F32),
                                         ng_ref[...]).astype(o_ref.dtype)


def gdn_step(cur, st, z, ba, cw, a_log, dt_bias, norm_g, s0, ns=8):
    n = cur.shape[0]
    alog = a_log.reshape(1, GD_HV)
    dtb = dt_bias.reshape(1, GD_HV)
    ng = norm_g.reshape(1, GD_DH)
    return pl.pallas_call(
        _gdn_step_kernel,
        grid=(n // ns,),
        in_specs=[pl.BlockSpec((ns, GD_CONV_CH), lambda i: (i, 0)),
                  pl.BlockSpec((ns, GD_CONV - 1, GD_CONV_CH), lambda i: (i, 0, 0)),
                  pl.BlockSpec((ns, GD_V_DIM), lambda i: (i, 0)),
                  pl.BlockSpec((ns, 2 * GD_HV), lambda i: (i, 0)),
                  _full(cw.shape), _full(alog.shape), _full(dtb.shape), _full(ng.shape),
                  pl.BlockSpec((ns, GD_HV, GD_DH, GD_DH), lambda i: (i, 0, 0, 0))],
        out_specs=[pl.BlockSpec((ns, GD_V_DIM), lambda i: (i, 0)),
                   pl.BlockSpec((ns, GD_HV, GD_DH, GD_DH), lambda i: (i, 0, 0, 0))],
        out_shape=[jax.ShapeDtypeStruct((n, GD_V_DIM), BF16),
                   jax.ShapeDtypeStruct((n, GD_HV, GD_DH, GD_DH), F32)],
        scratch_shapes=[pltpu.VMEM((ns, GD_V_DIM), F32)],
        compiler_params=_cparams(("arbitrary",)),
        name="gdn_step",
    )(cur, st, z, ba, cw, alog, dtb, ng, s0)


def _row_copy(src_hbm, row, dst, slot, sem):
    return pltpu.make_async_copy(src_hbm.at[pl.ds(pl.multiple_of(row * ROW_SLABS, ROW_SLABS), ROW_SLABS)],
                                 dst.at[pl.ds(pl.multiple_of(slot * ROW_SLABS, ROW_SLABS), ROW_SLABS)], sem)


def _start_row_gather(src_hbm, idx_ref, base, stride, n, dst, sem, both_queues=False):
    def issue(blk, c):
        r0 = blk * DMA_UNROLL
        for u in range(DMA_UNROLL):
            _row_copy(src_hbm, idx_ref[base + stride * (r0 + u)], dst, r0 + u, sem).start(
                priority=u % 2 if both_queues else 0)
        return c

    lax.fori_loop(0, n // DMA_UNROLL, issue, 0)


def _wait_row_gather(src_hbm, n, dst, sem):
    pltpu.make_async_copy(src_hbm.at[pl.ds(0, n * ROW_SLABS)], dst, sem).wait()


def _combine_kernel(dest_ref, x_ref, pv_ref, g_ref, b_ref, y_hbm, o_ref, buf_ref, sem, *, first_block):
    tt = o_ref.shape[0]
    i = pl.program_id(0)
    slot = i % 2

    def start(blk, into):
        base = (blk + first_block) * tt * TOP_K
        for k in range(TOP_K):
            _start_row_gather(y_hbm, dest_ref, base + k, TOP_K, tt, buf_ref.at[into, k], sem.at[into, k],
                              both_queues=True)

    @pl.when(i == 0)
    def _():
        start(0, 0)

    @pl.when(i + 1 < pl.num_programs(0))
    def _():
        start(i + 1, 1 - slot)

    for k in range(TOP_K):
        _wait_row_gather(y_hbm, tt, buf_ref.at[slot, k], sem.at[slot, k])
    pv = pv_ref[...]
    slabs = []
    for s in range(ROW_SLABS):
        y = pv[:, 0:1] * _slab(buf_ref.at[slot, 0], s, tt) + pv[:, 1:2] * _slab(buf_ref.at[slot, 1], s, tt)
        slabs.append(DEEPNORM_ALPHA * x_ref[:, s * LANES:(s + 1) * LANES] + y)
    mu = sum(jnp.sum(r, axis=-1, keepdims=True) for r in slabs) * (1.0 / D_MODEL)
    var = sum(jnp.sum((r - mu) * (r - mu), axis=-1, keepdims=True) for r in slabs) * (1.0 / D_MODEL)
    inv = lax.rsqrt(var + LN_EPS)
    for s in range(ROW_SLABS):
        cs = slice(s * LANES, (s + 1) * LANES)
        o_ref[:, cs] = (slabs[s] - mu) * inv * g_ref[:, cs] + b_ref[:, cs]


def combine(dest, x, pv, g, b, yb3, tt, first_block, n_blocks):
    grid_spec = pltpu.PrefetchScalarGridSpec(
        num_scalar_prefetch=1,
        grid=(n_blocks,),
        in_specs=[pl.BlockSpec((tt, D_MODEL), lambda i, dst: (i + first_block, 0)),
                  pl.BlockSpec((tt, TOP_K), lambda i, dst: (i + first_block, 0)),
                  pl.BlockSpec(g.shape, lambda i, dst: (0, 0)),
                  pl.BlockSpec(b.shape, lambda i, dst: (0, 0)),
                  pl.BlockSpec(memory_space=pl.ANY)],
        out_specs=pl.BlockSpec((tt, D_MODEL), lambda i, dst: (i, 0)),
        scratch_shapes=[pltpu.VMEM((2, TOP_K, tt * ROW_SLABS, LANES), F32), pltpu.SemaphoreType.DMA((2, TOP_K))],
    )
    return pl.pallas_call(
        functools.partial(_combine_kernel, first_block=first_block),
        grid_spec=grid_spec,
        out_shape=jax.ShapeDtypeStruct((n_blocks * tt, D_MODEL), F32),
        compiler_params=_cparams(("arbitrary",)),
        name="moe_combine",
    )(dest, x, pv, g, b, yb3)


def moe_experts(xb, topi, w1, w3, w2, tm):
    n = topi.shape[0]
    eid = topi.reshape(-1)
    onehot = (eid[:, None] == jnp.arange(N_EXPERTS, dtype=jnp.int32)[None, :]).astype(jnp.int32)
    csum = jnp.cumsum(onehot, axis=0)
    rank = jnp.sum((csum - onehot) * onehot, axis=1)
    counts = csum[-1]
    padded = (counts + tm - 1) // tm * tm
    pends = jnp.cumsum(padded)
    pstarts = pends - padded
    dest = (pstarts[eid] + rank).astype(jnp.int32)
    nb = (n * TOP_K + tm - 1) // tm + N_EXPERTS
    n_rows = nb * tm
    block_start = jnp.arange(nb, dtype=jnp.int32) * tm
    block_e = jnp.minimum(jnp.searchsorted(pends, block_start, side="right"), N_EXPERTS - 1).astype(jnp.int32)
    n_used = (pends[-1] // tm).astype(jnp.int32).reshape(1)
    n_chunks = n // TOKEN_CHUNK
    chunk_end = csum[TOKEN_CHUNK * TOP_K - 1::TOKEN_CHUNK * TOP_K]
    ends = chunk_end[:, block_e]
    first_entry = block_start - pstarts[block_e]
    last_entry = jnp.minimum(first_entry + tm, counts[block_e]) - 1
    chunk0 = jnp.minimum(jnp.sum(ends <= first_entry[None, :], axis=0), n_chunks - 1).astype(jnp.int32)
    chunk1 = jnp.minimum(jnp.sum(ends <= last_entry[None, :], axis=0), n_chunks - 1).astype(jnp.int32)
    chunk_cnt = jnp.maximum(chunk1 - chunk0 + 1, 1).astype(jnp.int32)
    dest_t = dest.reshape(n, TOP_K).T

    yb3 = swiglu_experts(block_e, n_used, chunk0, chunk_cnt, first_entry.astype(jnp.int32),
                         chunk_end.reshape(-1).astype(jnp.int32), xb, dest_t, w1, w3, w2, n_rows, tm)
    return yb3, dest


def kernel(x_prompt, x_sample, state_conv_a, state_conv_b, state_delta, meta_tokens, ln_g, ln_b, sc_w_in, sc_conv, sc_w_out, ffn_w1, ffn_w3, ffn_w2, gd_w_in, gd_conv, gd_a_log, gd_dt_bias, gd_norm_g, gd_w_out, moe_router, moe_w1, moe_w3, moe_w2):
    bsz, seq, d = x_prompt.shape
    n_s = x_sample.shape[0]
    n_small = N_META + n_s
    n_p = bsz * seq

    def row(v):
        return v.reshape(1, -1)

    sc_w_in_b = sc_w_in[0].astype(BF16)
    sc_w_out_b = sc_w_out[0].astype(BF16)
    ffn_w1_b, ffn_w3_b, ffn_w2_b = ffn_w1.astype(BF16), ffn_w3.astype(BF16), ffn_w2.astype(BF16)
    gd_w_qkv_b = gd_w_in[0][:, :GD_CONV_CH].astype(BF16)
    gd_w_z_b = gd_w_in[0][:, GD_CONV_CH:GD_CONV_CH + GD_V_DIM].astype(BF16)
    gd_w_ba_b = gd_w_in[0][:, GD_CONV_CH + GD_V_DIM:].astype(BF16)
    gd_w_out_b = gd_w_out[0].astype(BF16)
    moe_w1_b, moe_w3_b, moe_w2_b = moe_w1[0].astype(BF16), moe_w3[0].astype(BF16), moe_w2[0].astype(BF16)

    x_small = jnp.concatenate([meta_tokens.astype(F32), x_sample.reshape(n_s, d)], axis=0)

    xa_s, ch_s = l0_mix_small(x_small, state_conv_a[0, :, 0], state_conv_a[0, :, 1], sc_w_in_b, sc_w_out_b,
                              sc_conv[0], row(ln_g[0, 0]), row(ln_b[0, 0]))
    xa_p, tail_a = l0_mix_prompt(x_prompt, sc_w_in_b, sc_w_out_b, sc_conv[0], row(ln_g[0, 0]), row(ln_b[0, 0]),
                                 ch_s[N_META - HALO:N_META])
    def dense_ffn(x, tm):
        return swiglu_ln(x, ffn_w1_b[0], ffn_w3_b[0], ffn_w2_b[0], row(ln_g[0, 1]), row(ln_b[0, 1]), tm)

    xb_s = dense_ffn(xa_s, n_small)
    xb_p = dense_ffn(xa_p.reshape(n_p, d), 512)

    def gdn_inproj(x, tm):
        qkv = matmul(x, gd_w_qkv_b, F32, tm, 2048)
        z = matmul(x, gd_w_z_b, BF16, tm, 2048)
        ba = matmul(x, gd_w_ba_b, F32, tm, 2 * GD_HV)
        return qkv, z, ba

    qkv_s, z_s, ba_s = gdn_inproj(xb_s, n_small)
    qkv_p, z_p, ba_p = gdn_inproj(xb_p, 1024)

    pad = CHUNK - N_META

    def meta_pad(a):
        return jnp.pad(a[:N_META], ((0, pad), (0, 0)))[None]

    ba_m = meta_pad(ba_s)
    o_m, _, s_meta = gdn_chunked(meta_pad(qkv_s), meta_pad(z_s), ba_m, jnp.swapaxes(ba_m, 1, 2), gd_conv[0],
                                 gd_a_log[0], gd_dt_bias[0], gd_norm_g[0],
                                 jnp.zeros((HALO, GD_CONV_CH), F32), jnp.zeros((GD_HV, GD_DH, GD_DH), F32),
                                 CHUNK, valid_rows=N_META)
    o_smp, s_smp = gdn_step(qkv_s[N_META:], state_conv_b[0], z_s[N_META:], ba_s[N_META:], gd_conv[0],
                            gd_a_log[0], gd_dt_bias[0], gd_norm_g[0], state_delta[0])
    ba_p3 = ba_p.reshape(bsz, seq, 2 * GD_HV)
    o_p, tail_b, s_p = gdn_chunked(qkv_p.reshape(bsz, seq, GD_CONV_CH), z_p.reshape(bsz, seq, GD_V_DIM), ba_p3,
                                   jnp.swapaxes(ba_p3, 1, 2), gd_conv[0], gd_a_log[0], gd_dt_bias[0],
                                   gd_norm_g[0], qkv_s[N_META - HALO:N_META], s_meta[0], 256)
    tt = TOKEN_BLOCK
    small_pad = ((0, tt - n_small), (0, 0))
    o_s = jnp.pad(jnp.concatenate([o_m[0, :N_META], o_smp], axis=0), small_pad)
    xc, xc_b, topi, topv = matmul_ln_router(o_p.reshape(n_p, GD_V_DIM), xb_p, o_s, jnp.pad(xb_s, small_pad),
                                            gd_w_out_b, row(ln_g[1, 0]), row(ln_b[1, 0]), moe_router[0])
    yb3, dest = moe_experts(xc_b, topi, moe_w1_b, moe_w3_b, moe_w2_b, EXPERT_BLOCK)
    xd_p = combine(dest, xc, topv, row(ln_g[1, 1]), row(ln_b[1, 1]), yb3, tt, 0, n_p // tt)
    xd_s = combine(dest, xc, topv, row(ln_g[1, 1]), row(ln_b[1, 1]), yb3, tt, n_p // tt, 1)

    y_prompt = xd_p.reshape(bsz, seq, d)
    y_sample = xd_s[N_META:n_small].reshape(n_s, 1, d)
    new_conv_a_prompt = tail_a[None, :, HALO - (SC_WIDTH - 1):]
    new_conv_b_prompt = tail_b[None, :, HALO - (GD_CONV - 1):]
    new_delta_prompt = s_p[None]
    new_conv_a_sample = jnp.stack([state_conv_a[0, :, 1], ch_s[N_META:]], axis=1)[None]
    new_conv_b_sample = jnp.concatenate([state_conv_b[0, :, 1:], qkv_s[N_META:, None]], axis=1)[None]
    new_delta_sample = s_smp[None]
    return (y_prompt, y_sample, new_conv_a_prompt, new_conv_b_prompt, new_delta_prompt,
            new_conv_a_sample, new_conv_b_sample, new_delta_sample)
```

```python
import functools

import jax
import jax.numpy as jnp
from jax import lax
from jax.experimental import pallas as pl
from jax.experimental.pallas import tpu as pltpu

F32 = jnp.float32
BF16 = jnp.bfloat16

D_MODEL = 1024
N_META = 16
SC_WIDTH = 3
GD_HK = 8
GD_HV = 16
GD_DH = 128
GD_K_DIM = GD_HK * GD_DH
GD_V_DIM = GD_HV * GD_DH
GD_CONV = 4
GD_CONV_CH = 2 * GD_K_DIM + GD_V_DIM
D_FF = 3584
N_EXPERTS = 8
TOP_K = 2
LN_EPS = 1e-5
RMS_EPS = 1e-6
DEPTH = 2
DEEPNORM_ALPHA = (2 * DEPTH) ** 0.25

LANES = 128
ROW_SLABS = D_MODEL // LANES
DMA_UNROLL = 8
EXPERT_F_STEPS = 2
TOKEN_CHUNK = 256
CHUNK_RING = 12
SEL_ALIGN = 16
SEL_ROWS = TOKEN_CHUNK + SEL_ALIGN
TOKEN_BLOCK = 256
EXPERT_BLOCK = 512
CHUNK = 64
HALO = 8
V7X_VMEM_LIMIT = 56 * 1024 * 1024


def _cparams(sem, vmem=V7X_VMEM_LIMIT):
    return pltpu.CompilerParams(dimension_semantics=sem, vmem_limit_bytes=vmem)


def _bdot(a, b):
    return jnp.dot(a.astype(BF16), b.astype(BF16), preferred_element_type=F32)


def _bdot_nt(a, b):
    return lax.dot_general(a.astype(BF16), b.astype(BF16), (((1,), (1,)), ((), ())),
                           preferred_element_type=F32)


def _bdot_tn(a, b):
    return lax.dot_general(a.astype(BF16), b.astype(BF16), (((0,), (0,)), ((), ())),
                           preferred_element_type=F32)


def _bf16_pieces(x):
    p0 = x.astype(BF16)
    r1 = x - p0.astype(F32)
    p1 = r1.astype(BF16)
    p2 = (r1 - p1.astype(F32)).astype(BF16)
    return p0, p1, p2


def _dot_f32x3(a, b):
    a_hi = a.astype(BF16)
    b_hi = b.astype(BF16)
    a_lo = (a - a_hi.astype(F32)).astype(BF16)
    b_lo = (b - b_hi.astype(F32)).astype(BF16)
    return (jnp.dot(a_hi, b_hi, preferred_element_type=F32) + jnp.dot(a_lo, b_hi, preferred_element_type=F32)
            + jnp.dot(a_hi, b_lo, preferred_element_type=F32))


def _layer_norm(r, g, b):
    mu = jnp.mean(r, axis=-1, keepdims=True)
    d = r - mu
    var = jnp.mean(d * d, axis=-1, keepdims=True)
    return d * lax.rsqrt(var + LN_EPS) * g + b


def _silu(x):
    return x * jax.nn.sigmoid(x)


def _l0_inproj(xb, w_in_ref, ch_ref, bg_ref, row0, rows):
    col_chunk = 512
    for j in range(D_MODEL // col_chunk):
        lo, hi = j * col_chunk, (j + 1) * col_chunk
        bg = jnp.dot(xb, w_in_ref[:, lo:hi], preferred_element_type=F32)
        c = jnp.dot(xb, w_in_ref[:, D_MODEL + lo:D_MODEL + hi], preferred_element_type=F32)
        h = jnp.dot(xb, w_in_ref[:, 2 * D_MODEL + lo:2 * D_MODEL + hi], preferred_element_type=F32)
        ch_ref[row0:row0 + rows, lo:hi] = c * h
        bg_ref[:, lo:hi] = bg


def _l0_prompt_kernel(x_ref, w_in_ref, w_out_ref, cw_ref, g_ref, b_ref, carry_ref,
                      o_ref, tail_ref, buf_ref, bg_ref):
    tm = x_ref.shape[0]
    t = pl.program_id(1)

    @pl.when(t == 0)
    def _():
        buf_ref[0:HALO, :] = carry_ref[...]

    x = x_ref[...]
    _l0_inproj(x.astype(BF16), w_in_ref, buf_ref, bg_ref, HALO, tm)
    y = (cw_ref[0:1, :] * buf_ref[HALO - 2:HALO - 2 + tm, :]
         + cw_ref[1:2, :] * buf_ref[HALO - 1:HALO - 1 + tm, :]
         + cw_ref[2:3, :] * buf_ref[HALO:HALO + tm, :])
    u = (bg_ref[...] * y).astype(BF16)
    m = jnp.dot(u, w_out_ref[...], preferred_element_type=F32)
    o_ref[...] = _layer_norm(DEEPNORM_ALPHA * x + m, g_ref[...], b_ref[...])
    tail = buf_ref[tm:tm + HALO, :]
    tail_ref[...] = tail
    buf_ref[0:HALO, :] = tail


def _l0_small_kernel(x_ref, st0_ref, st1_ref, w_in_ref, w_out_ref, cw_ref, g_ref, b_ref,
                     o_ref, ch_out_ref, buf_ref, bg_ref):
    n = x_ref.shape[0]
    x = x_ref[...]
    buf_ref[0:HALO, :] = jnp.zeros((HALO, D_MODEL), F32)
    _l0_inproj(x.astype(BF16), w_in_ref, buf_ref, bg_ref, HALO, n)
    y_meta = (cw_ref[0:1, :] * buf_ref[HALO - 2:HALO - 2 + N_META, :]
              + cw_ref[1:2, :] * buf_ref[HALO - 1:HALO - 1 + N_META, :]
              + cw_ref[2:3, :] * buf_ref[HALO:HALO + N_META, :])
    ch = buf_ref[HALO:HALO + n, :]
    y_s = (cw_ref[0:1, :] * st0_ref[...] + cw_ref[1:2, :] * st1_ref[...]
           + cw_ref[2:3, :] * ch[N_META:, :])
    y = jnp.concatenate([y_meta, y_s], axis=0)
    u = (bg_ref[...] * y).astype(BF16)
    m = jnp.dot(u, w_out_ref[...], preferred_element_type=F32)
    o_ref[...] = _layer_norm(DEEPNORM_ALPHA * x + m, g_ref[...], b_ref[...])
    ch_out_ref[...] = ch


def _full(shape):
    nd = len(shape)
    return pl.BlockSpec(shape, lambda *_: (0,) * nd)


def l0_mix_prompt(x, w_in, w_out, cw, g, b, carry, tm=512):
    bsz, seq, d = x.shape
    return pl.pallas_call(
        _l0_prompt_kernel,
        grid=(bsz, seq // tm),
        in_specs=[pl.BlockSpec((None, tm, d), lambda i, t: (i, t, 0)),
                  _full(w_in.shape), _full(w_out.shape), _full(cw.shape), _full(g.shape), _full(b.shape),
                  _full(carry.shape)],
        out_specs=[pl.BlockSpec((None, tm, d), lambda i, t: (i, t, 0)),
                   pl.BlockSpec((None, HALO, d), lambda i, t: (i, 0, 0))],
        out_shape=[jax.ShapeDtypeStruct((bsz, seq, d), F32),
                   jax.ShapeDtypeStruct((bsz, HALO, d), F32)],
        scratch_shapes=[pltpu.VMEM((tm + HALO, d), F32), pltpu.VMEM((tm, d), F32)],
        compiler_params=_cparams(("arbitrary", "arbitrary")),
        name="l0_mix_prompt",
    )(x, w_in, w_out, cw, g, b, carry)


def l0_mix_small(x, st0, st1, w_in, w_out, cw, g, b):
    n, d = x.shape
    return pl.pallas_call(
        _l0_small_kernel,
        grid=(1,),
        in_specs=[_full(a.shape) for a in (x, st0, st1, w_in, w_out, cw, g, b)],
        out_specs=[_full((n, d)), _full((n, d))],
        out_shape=[jax.ShapeDtypeStruct((n, d), F32), jax.ShapeDtypeStruct((n, d), F32)],
        scratch_shapes=[pltpu.VMEM((n + HALO, d), F32), pltpu.VMEM((n, d), F32)],
        compiler_params=_cparams(("arbitrary",)),
        name="l0_mix_small",
    )(x, st0, st1, w_in, w_out, cw, g, b)


def _slab(ref, s, rows):
    return ref[pl.ds(s, rows, stride=ROW_SLABS), :]


def _to_slabs(o_ref, val):
    rows = val.shape[0]
    for s in range(ROW_SLABS):
        o_ref[pl.ds(s, rows, stride=ROW_SLABS), :] = val[:, s * LANES:(s + 1) * LANES]


def _swiglu_partial(xb, w1_ref, w3_ref, w2_ref, acc_ref):
    a = jnp.dot(xb, w1_ref[...], preferred_element_type=F32)
    b = jnp.dot(xb, w3_ref[...], preferred_element_type=F32)
    h = (_silu(a) * b).astype(BF16)
    part = jnp.dot(h, w2_ref[...], preferred_element_type=F32)
    acc_ref[...] = jnp.where(pl.program_id(1) > 0, acc_ref[...], 0.0) + part


def _zero_acc_once(acc_ref):
    @pl.when((pl.program_id(0) == 0) & (pl.program_id(1) == 0))
    def _():
        acc_ref[...] = jnp.zeros(acc_ref.shape, acc_ref.dtype)


def _swiglu_ln_kernel(x_ref, w1_ref, w3_ref, w2_ref, g_ref, b_ref, o_ref, acc_ref):
    _zero_acc_once(acc_ref)
    _swiglu_partial(x_ref[...].astype(BF16), w1_ref, w3_ref, w2_ref, acc_ref)

    @pl.when(pl.program_id(1) == pl.num_programs(1) - 1)
    def _():
        o_ref[...] = _layer_norm(DEEPNORM_ALPHA * x_ref[...] + acc_ref[...], g_ref[...], b_ref[...])


def swiglu_ln(x, w1, w3, w2, g, b, tm, tf=1792):
    n, d = x.shape
    return pl.pallas_call(
        _swiglu_ln_kernel,
        grid=(n // tm, D_FF // tf),
        in_specs=[pl.BlockSpec((tm, d), lambda i, f: (i, 0)),
                  pl.BlockSpec((d, tf), lambda i, f: (0, f)),
                  pl.BlockSpec((d, tf), lambda i, f: (0, f)),
                  pl.BlockSpec((tf, d), lambda i, f: (f, 0)),
                  _full(g.shape), _full(b.shape)],
        out_specs=pl.BlockSpec((tm, d), lambda i, f: (i, 0)),
        out_shape=jax.ShapeDtypeStruct((n, d), F32),
        scratch_shapes=[pltpu.VMEM((tm, d), F32)],
        compiler_params=_cparams(("arbitrary", "arbitrary")),
        name="swiglu_ln",
    )(x, w1, w3, w2, g, b)


def _token_chunk_copy(x_hbm, chunk, buf_ref, slot, sem):
    start = pl.multiple_of(chunk * TOKEN_CHUNK, TOKEN_CHUNK)
    return pltpu.make_async_copy(x_hbm.at[pl.ds(start, TOKEN_CHUNK)], buf_ref.at[slot], sem.at[slot])


def _swiglu_experts_kernel(be_ref, nu_ref, c0_ref, cn_ref, e0_ref, cend_ref, x_hbm, dest_ref, w1_ref, w3_ref,
                           w2_ref, o_ref, acc_ref, xsel_ref, xb_ref, chunk_ref, sem):
    tm = xb_ref.shape[0]
    i = pl.program_id(0)
    f = pl.program_id(1)
    nf = pl.num_programs(1)
    n_used = nu_ref[0]
    _zero_acc_once(acc_ref)

    def start_ring(blk):
        for slot in range(CHUNK_RING):
            @pl.when(slot < cn_ref[blk])
            def _():
                _token_chunk_copy(x_hbm, c0_ref[blk] + slot, chunk_ref, slot, sem).start()

    @pl.when(i < n_used)
    def _():
        @pl.when(f == 0)
        def _():
            first = c0_ref[i]
            count = cn_ref[i]
            expert = be_ref[i]
            row_iota = lax.broadcasted_iota(jnp.int32, (SEL_ROWS, TOKEN_CHUNK), 0)

            @pl.when(i == 0)
            def _():
                start_ring(0)

            xsel_ref[...] = jnp.zeros(xsel_ref.shape, xsel_ref.dtype)

            def pick(j, carry):
                slot = jnp.where(j < CHUNK_RING, j, 0)

                @pl.when(j >= CHUNK_RING)
                def _():
                    _token_chunk_copy(x_hbm, first + j, chunk_ref, 0, sem).start()

                chunk = first + j
                before = jnp.where(chunk > 0, cend_ref[jnp.maximum(chunk - 1, 0) * N_EXPERTS + expert], 0)
                row0 = jnp.clip(before - e0_ref[i], 0, tm - SEL_ROWS) // SEL_ALIGN * SEL_ALIGN
                row0 = pl.multiple_of(row0, SEL_ALIGN)
                row_id = row_iota + (i * tm + row0)
                _token_chunk_copy(x_hbm, chunk, chunk_ref, slot, sem).wait()
                lane0 = pl.multiple_of(chunk * TOKEN_CHUNK, TOKEN_CHUNK)
                hit = row_id == dest_ref[0:1, pl.ds(lane0, TOKEN_CHUNK)]
                for k in range(1, TOP_K):
                    hit = hit | (row_id == dest_ref[k:k + 1, pl.ds(lane0, TOKEN_CHUNK)])
                sel = jnp.where(hit, 1.0, 0.0).astype(BF16)
                xsel_ref[pl.ds(row0, SEL_ROWS), :] += jnp.dot(sel, chunk_ref[slot], preferred_element_type=F32)
                return carry

            lax.fori_loop(0, count, pick, 0)
            xb_ref[...] = xsel_ref[...].astype(BF16)

        @pl.when((f == nf - 1) & (i + 1 < n_used))
        def _():
            start_ring(i + 1)

        _swiglu_partial(xb_ref[...], w1_ref, w3_ref, w2_ref, acc_ref)

        @pl.when(f == nf - 1)
        def _():
            _to_slabs(o_ref, acc_ref[...])

    @pl.when(i >= nu_ref[0])
    def _():
        o_ref[...] = jnp.zeros(o_ref.shape, o_ref.dtype)


def swiglu_experts(block_e, n_used, chunk0, n_chunks, entry0, chunk_end, xb, dest_t, w1, w3, w2, n_rows, tm):
    d = D_MODEL
    nf = EXPERT_F_STEPS
    tf = D_FF // nf

    def f_idx(i, f, nu):
        return jnp.where(i < nu[0], f, nf - 1)

    grid_spec = pltpu.PrefetchScalarGridSpec(
        num_scalar_prefetch=6,
        grid=(n_rows // tm, nf),
        in_specs=[pl.BlockSpec(memory_space=pl.ANY),
                  pl.BlockSpec(dest_t.shape, lambda i, f, be, nu, c0, cn, e0, ce: (0, 0)),
                  pl.BlockSpec((None, d, tf), lambda i, f, be, nu, c0, cn, e0, ce: (be[i], 0, f_idx(i, f, nu))),
                  pl.BlockSpec((None, d, tf), lambda i, f, be, nu, c0, cn, e0, ce: (be[i], 0, f_idx(i, f, nu))),
                  pl.BlockSpec((None, tf, d), lambda i, f, be, nu, c0, cn, e0, ce: (be[i], f_idx(i, f, nu), 0))],
        out_specs=pl.BlockSpec((tm * ROW_SLABS, LANES), lambda i, f, be, nu, c0, cn, e0, ce: (i, 0)),
        scratch_shapes=[pltpu.VMEM((tm, d), F32),
                        pltpu.VMEM((tm, d), F32),
                        pltpu.VMEM((tm, d), BF16),
                        pltpu.VMEM((CHUNK_RING, TOKEN_CHUNK, d), BF16),
                        pltpu.SemaphoreType.DMA((CHUNK_RING,))],
    )
    return pl.pallas_call(
        _swiglu_experts_kernel,
        grid_spec=grid_spec,
        out_shape=jax.ShapeDtypeStruct((n_rows * ROW_SLABS, LANES), F32),
        compiler_params=_cparams(("arbitrary", "arbitrary")),
        name="swiglu_experts",
    )(block_e, n_used, chunk0, n_chunks, entry0, chunk_end, xb, dest_t, w1, w3, w2)


QKV_GROUPS = 2


def _gdn_inproj_kernel(x_ref, wqkv_ref, wz_ref, wba_ref, qkv_ref, z_ref, ba_ref):
    j = pl.program_id(1)
    xb = x_ref[...].astype(BF16)

    @pl.when(j < QKV_GROUPS)
    def _():
        qkv_ref[...] = jnp.dot(xb, wqkv_ref[...], preferred_element_type=F32)

    @pl.when(j == QKV_GROUPS)
    def _():
        z_ref[...] = jnp.dot(xb, wz_ref[...], preferred_element_type=F32).astype(z_ref.dtype)
        ba_ref[...] = jnp.dot(xb, wba_ref[...], preferred_element_type=F32)


def gdn_inproj(x, wqkv, wz, wba, tm):
    n, k = x.shape
    tn = wqkv.shape[1] // QKV_GROUPS

    def group(i, j):
        return (i, jnp.minimum(j, QKV_GROUPS - 1))

    return pl.pallas_call(
        _gdn_inproj_kernel,
        grid=(n // tm, QKV_GROUPS + 1),
        in_specs=[pl.BlockSpec((tm, k), lambda i, j: (i, 0)),
                  pl.BlockSpec((k, tn), lambda i, j: (0, jnp.minimum(j, QKV_GROUPS - 1))),
                  _full(wz.shape), _full(wba.shape)],
        out_specs=[pl.BlockSpec((tm, tn), group),
                   pl.BlockSpec((tm, wz.shape[1]), lambda i, j: (i, 0)),
                   pl.BlockSpec((tm, wba.shape[1]), lambda i, j: (i, 0))],
        out_shape=[jax.ShapeDtypeStruct((n, wqkv.shape[1]), F32),
                   jax.ShapeDtypeStruct((n, wz.shape[1]), BF16),
                   jax.ShapeDtypeStruct((n, wba.shape[1]), F32)],
        compiler_params=_cparams(("arbitrary", "arbitrary")),
        name="gdn_inproj",
    )(x, wqkv, wz, wba)


def _top2(logits):
    mx = jnp.max(logits, axis=-1, keepdims=True)
    ex = jnp.exp(logits - mx)
    p = ex / jnp.sum(ex, axis=-1, keepdims=True)
    lane = lax.broadcasted_iota(jnp.int32, p.shape, 1)
    p1 = jnp.max(p, axis=-1, keepdims=True)
    i1 = jnp.min(jnp.where(p == p1, lane, N_EXPERTS), axis=-1, keepdims=True)
    rest = jnp.where(lane == i1, -1.0, p)
    p2 = jnp.max(rest, axis=-1, keepdims=True)
    i2 = jnp.min(jnp.where(rest == p2, lane, N_EXPERTS), axis=-1, keepdims=True)
    tot = p1 + p2
    return jnp.concatenate([i1, i2], axis=-1), jnp.concatenate([p1 / tot, p2 / tot], axis=-1)


def _mm_ln_router_kernel(a_ref, x_ref, a_tail_ref, x_tail_ref, w_ref, g_ref, b_ref, wr_ref,
                         o_ref, ob_ref, idx_ref, val_ref, *, n_main):
    def block(a_blk, x_blk):
        m = jnp.dot(a_blk[...], w_ref[...], preferred_element_type=F32)
        y = _layer_norm(DEEPNORM_ALPHA * x_blk[...] + m, g_ref[...], b_ref[...])
        o_ref[...] = y
        ob_ref[...] = y.astype(BF16)
        logits = _dot_f32x3(y, wr_ref[...])
        idx_ref[...], val_ref[...] = _top2(logits)

    @pl.when(pl.program_id(0) < n_main)
    def _():
        block(a_ref, x_ref)

    @pl.when(pl.program_id(0) >= n_main)
    def _():
        block(a_tail_ref, x_tail_ref)


def matmul_ln_router(a, x, a_tail, x_tail, w, g, b, wr):
    tm, k = a_tail.shape
    n = a.shape[0]
    d = w.shape[1]
    n_main = n // tm
    n_all = n + tm

    def main(i):
        return (jnp.minimum(i, n_main - 1), 0)

    return pl.pallas_call(
        functools.partial(_mm_ln_router_kernel, n_main=n_main),
        grid=(n_main + 1,),
        in_specs=[pl.BlockSpec((tm, k), main), pl.BlockSpec((tm, d), main),
                  _full(a_tail.shape), _full(x_tail.shape),
                  _full(w.shape), _full(g.shape), _full(b.shape), _full(wr.shape)],
        out_specs=[pl.BlockSpec((tm, d), lambda i: (i, 0)), pl.BlockSpec((tm, d), lambda i: (i, 0)),
                   pl.BlockSpec((tm, TOP_K), lambda i: (i, 0)), pl.BlockSpec((tm, TOP_K), lambda i: (i, 0))],
        out_shape=[jax.ShapeDtypeStruct((n_all, d), F32), jax.ShapeDtypeStruct((n_all, d), BF16),
                   jax.ShapeDtypeStruct((n_all, TOP_K), jnp.int32), jax.ShapeDtypeStruct((n_all, TOP_K), F32)],
        compiler_params=_cparams(("arbitrary",)),
        name="matmul_ln_router",
    )(a, x, a_tail, x_tail, w, g, b, wr)


def _l2norm_heads(x, n_heads, scale):
    outs = []
    for h in range(n_heads):
        xh = x[:, h * GD_DH:(h + 1) * GD_DH]
        ss = jnp.sum(xh * xh, axis=-1, keepdims=True)
        outs.append(xh * (lax.rsqrt(ss + RMS_EPS) * scale))
    return outs


def _softplus(x):
    return jnp.maximum(x, 0.0) + jnp.log1p(jnp.exp(-jnp.abs(x)))


def _gated_rmsnorm(o, z, norm_g):
    ms = jnp.mean(o * o, axis=-1, keepdims=True)
    return o * lax.rsqrt(ms + RMS_EPS) * norm_g * _silu(z)


PAIR = GD_HV // GD_HK
PAIR_SQ = PAIR * CHUNK
PAIR_DV = PAIR * GD_DH


def _sq_cols(x, p):
    lane = lax.broadcasted_iota(jnp.int32, (CHUNK, PAIR_SQ), 1)
    a = jnp.broadcast_to(x[:, PAIR * p:PAIR * p + 1], (CHUNK, PAIR_SQ))
    b = jnp.broadcast_to(x[:, PAIR * p + 1:PAIR * p + 2], (CHUNK, PAIR_SQ))
    return jnp.where(lane < CHUNK, a, b)


def _block_diag_sq(m):
    lane = lax.broadcasted_iota(jnp.int32, m.shape, 1)
    top = jnp.where(lane < CHUNK, m, 0.0).astype(BF16)
    bot = jnp.where(lane < CHUNK, 0.0, m).astype(BF16)
    return jnp.concatenate([top, bot], axis=0)


def _block_diag_dv(m):
    mb = m.astype(BF16)
    zero = jnp.zeros((CHUNK, GD_DH), BF16)
    top = jnp.concatenate([mb[:, :GD_DH], zero], axis=1)
    bot = jnp.concatenate([zero, mb[:, GD_DH:]], axis=1)
    return jnp.concatenate([top, bot], axis=0)


def _tri_inverse_pairs(a_list):
    row = lax.broadcasted_iota(jnp.int32, (CHUNK, PAIR_SQ), 0)
    col = lax.broadcasted_iota(jnp.int32, (CHUNK, PAIR_SQ), 1) % CHUNK
    eye = jnp.where(row == col, 1.0, 0.0).astype(F32)
    s = 1
    inv = None
    while s < CHUNK:
        sub = ((row // (2 * s)) == (col // (2 * s))) & ((row // s) % 2 == 1) & ((col // s) % 2 == 0)
        e_list = [jnp.where(sub, a, 0.0) for a in a_list]
        if s == 1:
            inv = [eye - e for e in e_list]
        else:
            x_list = [jnp.dot(e.astype(BF16), _block_diag_sq(d), preferred_element_type=F32)
                      for e, d in zip(e_list, inv)]
            y_list = [jnp.dot(d.astype(BF16), _block_diag_sq(x), preferred_element_type=F32)
                      for d, x in zip(inv, x_list)]
            inv = [d - y for d, y in zip(inv, y_list)]
        s *= 2
    return inv


def _gdn_chunk_kernel(qkv_ref, z_ref, ba_ref, bat_ref, cw_ref, alog_ref, dtb_ref, alogt_ref, dtbt_ref,
                      ng_ref, carry_ref, s0_ref, o_ref, tail_ref, sout_ref,
                      buf_ref, s_ref, q_ref, k_ref, v_ref, lhs_w_ref, lhs_o_ref, glast_ref,
                      *, valid_rows):
    tc = qkv_ref.shape[0]
    t = pl.program_id(1)

    @pl.when(t == 0)
    def _():
        for c in range(GD_CONV_CH // LANES):
            buf_ref[c, 0:HALO, :] = carry_ref[:, c * LANES:(c + 1) * LANES]
        for h in range(GD_HV):
            j = h % PAIR
            s_ref[h // PAIR, :, j * GD_DH:(j + 1) * GD_DH] = s0_ref[h]

    for c in range(GD_CONV_CH // LANES):
        lo, hi = c * LANES, (c + 1) * LANES
        buf_ref[c, HALO:HALO + tc, :] = qkv_ref[:, lo:hi]
        y = cw_ref[GD_CONV - 1:GD_CONV, lo:hi] * qkv_ref[:, lo:hi]
        for i in range(GD_CONV - 1):
            off = HALO - (GD_CONV - 1) + i
            y = y + cw_ref[i:i + 1, lo:hi] * buf_ref[c, off:off + tc, :]
        act = _silu(y)
        if lo < GD_K_DIM:
            q_ref[:, lo:hi] = _l2norm_heads(act, 1, GD_DH ** -0.5)[0].astype(BF16)
        elif lo < 2 * GD_K_DIM:
            k_ref[:, lo - GD_K_DIM:hi - GD_K_DIM] = _l2norm_heads(act, 1, 1.0)[0].astype(BF16)
        else:
            v_ref[:, lo - 2 * GD_K_DIM:hi - 2 * GD_K_DIM] = act
        tail = buf_ref[c, tc:tc + HALO, :]
        tail_ref[:, lo:hi] = tail
        buf_ref[c, 0:HALO, :] = tail

    row = lax.broadcasted_iota(jnp.int32, (CHUNK, CHUNK), 0)
    col = lax.broadcasted_iota(jnp.int32, (CHUNK, CHUNK), 1)
    tril_ones = jnp.where(row >= col, 1.0, 0.0).astype(BF16)
    triu_ones = jnp.where(row <= col, 1.0, 0.0).astype(BF16)
    row2 = lax.broadcasted_iota(jnp.int32, (CHUNK, PAIR_SQ), 0)
    col2 = lax.broadcasted_iota(jnp.int32, (CHUNK, PAIR_SQ), 1) % CHUNK
    incl2 = row2 >= col2
    strict2 = row2 > col2
    n_pairs = GD_HK

    eye2 = jnp.where(row2 == col2, 1.0, 0.0).astype(F32)
    n_chunks = tc // CHUNK

    def pair_rows(xt, p):
        return jnp.concatenate([xt[PAIR * p + j:PAIR * p + j + 1, :] for j in range(PAIR)], axis=1)

    items = [(c, p) for c in range(n_chunks) for p in range(n_pairs)]
    cums, cumts, betas, betats = [], [], [], []
    for c in range(n_chunks):
        ba = ba_ref[c]
        bat = bat_ref[c]
        beta = jax.nn.sigmoid(ba[:, :GD_HV])
        betat = jax.nn.sigmoid(bat[:GD_HV, :])
        g = -jnp.exp(alog_ref[...]) * _softplus(ba[:, GD_HV:] + dtb_ref[...])
        gt = -jnp.exp(alogt_ref[...]) * _softplus(bat[GD_HV:, :] + dtbt_ref[...])
        if valid_rows is not None:
            rid = lax.broadcasted_iota(jnp.int32, (CHUNK, GD_HV), 0) + c * CHUNK + t * tc
            beta = jnp.where(rid < valid_rows, beta, 0.0)
            g = jnp.where(rid < valid_rows, g, 0.0)
            cid = lax.broadcasted_iota(jnp.int32, (GD_HV, CHUNK), 1) + c * CHUNK + t * tc
            betat = jnp.where(cid < valid_rows, betat, 0.0)
            gt = jnp.where(cid < valid_rows, gt, 0.0)
        cum = sum(jnp.dot(tril_ones, piece, preferred_element_type=F32) for piece in _bf16_pieces(g))
        cumt = sum(jnp.dot(piece, triu_ones, preferred_element_type=F32) for piece in _bf16_pieces(gt))
        glast_ref[c] = jnp.concatenate(
            [jnp.broadcast_to(jnp.exp(cum[CHUNK - 1:CHUNK, h:h + 1]), (1, GD_DH)) for h in range(GD_HV)], axis=1)
        cums.append(cum)
        cumts.append(cumt)
        betas.append(beta)
        betats.append(betat)

    grams = []
    for c, p in items:
        kb = k_ref[c * CHUNK:(c + 1) * CHUNK, p * GD_DH:(p + 1) * GD_DH]
        qb = q_ref[c * CHUNK:(c + 1) * CHUNK, p * GD_DH:(p + 1) * GD_DH]
        grams.append(lax.dot_general(jnp.concatenate([kb, qb], axis=0), jnp.concatenate([kb, kb], axis=0),
                                     (((1,), (1,)), ((), ())), preferred_element_type=F32))
    a_list, attn_list = [], []
    for (c, p), gram in zip(items, grams):
        diff = _sq_cols(cums[c], p) - pair_rows(cumts[c], p)
        dec = jnp.where(incl2, jnp.exp(jnp.where(incl2, diff, 0.0)), 0.0)
        attn_list.append(gram[CHUNK:] * dec)
        a_list.append(jnp.where(strict2, dec * gram[:CHUNK], 0.0) * _sq_cols(betas[c], p))
    inv_list = _tri_inverse_pairs(a_list)
    for (c, p), inv, attn in zip(items, inv_list, attn_list):
        cum_row = pair_rows(cumts[c], p)
        last = jnp.concatenate([jnp.broadcast_to(cumts[c][PAIR * p + j:PAIR * p + j + 1, CHUNK - 1:CHUNK], (1, CHUNK))
                                for j in range(PAIR)], axis=1)
        ecum_row = jnp.exp(cum_row)
        t_beta = inv * pair_rows(betats[c], p)
        lhs_w_ref[c, p] = jnp.concatenate([t_beta, -(t_beta * ecum_row)], axis=1).astype(BF16)
        top = jnp.concatenate([attn, eye2 * ecum_row], axis=1)
        bot = jnp.concatenate([eye2 * jnp.exp(last - cum_row), jnp.zeros((CHUNK, PAIR_SQ), F32)], axis=1)
        lhs_o_ref[c, p] = jnp.concatenate([top, bot], axis=0).astype(BF16)

    def state_body(c, carry):
        r0 = pl.multiple_of(c * CHUNK, CHUNK)
        kbs, s_old, kqs = [], [], []
        for p in range(n_pairs):
            kb = k_ref[pl.ds(r0, CHUNK), p * GD_DH:(p + 1) * GD_DH]
            qb = q_ref[pl.ds(r0, CHUNK), p * GD_DH:(p + 1) * GD_DH]
            s2 = s_ref[p]
            kbs.append(kb)
            s_old.append(s2)
            kqs.append(jnp.dot(jnp.concatenate([kb, qb], axis=0), s2.astype(BF16), preferred_element_type=F32))
        ws = []
        for p in range(n_pairs):
            v2 = v_ref[pl.ds(r0, CHUNK), p * PAIR_DV:(p + 1) * PAIR_DV]
            rhs = jnp.concatenate([_block_diag_dv(v2), _block_diag_dv(kqs[p][:CHUNK])], axis=0)
            ws.append(jnp.dot(lhs_w_ref[c, p], rhs, preferred_element_type=F32))
        for p in range(n_pairs):
            rhs = jnp.concatenate([_block_diag_dv(ws[p]), _block_diag_dv(kqs[p][CHUNK:])], axis=0)
            ow = jnp.dot(lhs_o_ref[c, p], rhs, preferred_element_type=F32)
            o2 = ow[:CHUNK]
            upd = lax.dot_general(kbs[p], ow[CHUNK:].astype(BF16), (((0,), (0,)), ((), ())),
                                  preferred_element_type=F32)
            s_ref[p] = glast_ref[c, :, p * PAIR_DV:(p + 1) * PAIR_DV] * s_old[p] + upd
            for j in range(PAIR):
                h = PAIR * p + j
                z = z_ref[pl.ds(r0, CHUNK), h * GD_DH:(h + 1) * GD_DH].astype(F32)
                o_ref[pl.ds(r0, CHUNK), h * GD_DH:(h + 1) * GD_DH] = _gated_rmsnorm(
                    o2[:, j * GD_DH:(j + 1) * GD_DH], z, ng_ref[...]).astype(o_ref.dtype)
        return carry

    lax.fori_loop(0, tc // CHUNK, state_body, 0)
    for h in range(GD_HV):
        j = h % PAIR
        sout_ref[h] = s_ref[h // PAIR, :, j * GD_DH:(j + 1) * GD_DH]


def gdn_chunked(qkv, z, ba, bat, cw, a_log, dt_bias, norm_g, carry, s0, tc, valid_rows=None):
    bsz, seq, _ = qkv.shape
    nchunk = tc // CHUNK
    ba4 = ba.reshape(bsz, seq // CHUNK, CHUNK, 2 * GD_HV)
    bat4 = bat.reshape(bsz, 2 * GD_HV, seq // CHUNK, CHUNK).transpose(0, 2, 1, 3)
    alog = a_log.reshape(1, GD_HV)
    dtb = dt_bias.reshape(1, GD_HV)
    alogt = a_log.reshape(GD_HV, 1)
    dtbt = dt_bias.reshape(GD_HV, 1)
    ng = norm_g.reshape(1, GD_DH)
    return pl.pallas_call(
        functools.partial(_gdn_chunk_kernel, valid_rows=valid_rows),
        grid=(bsz, seq // tc),
        in_specs=[pl.BlockSpec((None, tc, GD_CONV_CH), lambda i, t: (i, t, 0)),
                  pl.BlockSpec((None, tc, GD_V_DIM), lambda i, t: (i, t, 0)),
                  pl.BlockSpec((None, nchunk, CHUNK, 2 * GD_HV), lambda i, t: (i, t, 0, 0)),
                  pl.BlockSpec((None, nchunk, 2 * GD_HV, CHUNK), lambda i, t: (i, t, 0, 0)),
                  _full(cw.shape), _full(alog.shape), _full(dtb.shape), _full(alogt.shape), _full(dtbt.shape),
                  _full(ng.shape), _full(carry.shape), _full(s0.shape)],
        out_specs=[pl.BlockSpec((None, tc, GD_V_DIM), lambda i, t: (i, t, 0)),
                   pl.BlockSpec((None, HALO, GD_CONV_CH), lambda i, t: (i, 0, 0)),
                   pl.BlockSpec((None, GD_HV, GD_DH, GD_DH), lambda i, t: (i, 0, 0, 0))],
        out_shape=[jax.ShapeDtypeStruct((bsz, seq, GD_V_DIM), BF16),
                   jax.ShapeDtypeStruct((bsz, HALO, GD_CONV_CH), F32),
                   jax.ShapeDtypeStruct((bsz, GD_HV, GD_DH, GD_DH), F32)],
        scratch_shapes=[pltpu.VMEM((GD_CONV_CH // LANES, tc + HALO, LANES), F32),
                        pltpu.VMEM((GD_HK, GD_DH, PAIR_DV), F32),
                        pltpu.VMEM((tc, GD_K_DIM), BF16), pltpu.VMEM((tc, GD_K_DIM), BF16),
                        pltpu.VMEM((tc, GD_V_DIM), F32),
                        pltpu.VMEM((nchunk, GD_HK, CHUNK, 2 * PAIR_SQ), BF16),
                        pltpu.VMEM((nchunk, GD_HK, 2 * CHUNK, 2 * PAIR_SQ), BF16),
                        pltpu.VMEM((nchunk, 1, GD_V_DIM), F32)],
        compiler_params=_cparams(("arbitrary", "arbitrary")),
        name="gdn_chunked",
    )(qkv, z, ba4, bat4, cw, alog, dtb, alogt, dtbt, ng, carry, s0)


def _gdn_step_kernel(cur_ref, st_ref, z_ref, ba_ref, cw_ref, alog_ref, dtb_ref, ng_ref, s_ref,
                     o_ref, sout_ref, oacc_ref):
    ns = cur_ref.shape[0]
    y = cw_ref[GD_CONV - 1:GD_CONV, :] * cur_ref[...]
    for i in range(GD_CONV - 1):
        y = y + cw_ref[i:i + 1, :] * st_ref[:, i, :]
    act = _silu(y)
    qh = _l2norm_heads(act[:, :GD_K_DIM], GD_HK, GD_DH ** -0.5)
    kh = _l2norm_heads(act[:, GD_K_DIM:2 * GD_K_DIM], GD_HK, 1.0)
    qk_t = jnp.concatenate(qh + kh, axis=0).T
    ba = ba_ref[...]
    beta = jax.nn.sigmoid(ba[:, :GD_HV])
    eg = jnp.exp(-jnp.exp(alog_ref[...]) * _softplus(ba[:, GD_HV:] + dtb_ref[...]))
    for s in range(ns):
        for h in range(GD_HV):
            g = h // (GD_HV // GD_HK)
            qcol = qk_t[:, g * ns + s:g * ns + s + 1]
            kcol = qk_t[:, (GD_HK + g) * ns + s:(GD_HK + g) * ns + s + 1]
            sd = s_ref[s, h] * eg[s:s + 1, h:h + 1]
            ks = jnp.sum(sd * kcol, axis=0, keepdims=True)
            v = act[s:s + 1, 2 * GD_K_DIM + h * GD_DH:2 * GD_K_DIM + (h + 1) * GD_DH]
            w = beta[s:s + 1, h:h + 1] * (v - ks)
            sn = sd + kcol * w
            sout_ref[s, h] = sn
            oacc_ref[s:s + 1, h * GD_DH:(h + 1) * GD_DH] = jnp.sum(sn * qcol, axis=0, keepdims=True)
    for h in range(GD_HV):
        lo, hi = h * GD_DH, (h + 1) * GD_DH
        o_ref[:, lo:hi] = _gated_rmsnorm(oacc_ref[:, lo:hi], z_ref[:, lo:hi].astype(F32),
                                         ng_ref[...]).astype(o_ref.dtype)


def gdn_step(cur, st, z, ba, cw, a_log, dt_bias, norm_g, s0, ns=8):
    n = cur.shape[0]
    alog = a_log.reshape(1, GD_HV)
    dtb = dt_bias.reshape(1, GD_HV)
    ng = norm_g.reshape(1, GD_DH)
    return pl.pallas_call(
        _gdn_step_kernel,
        grid=(n // ns,),
        in_specs=[pl.BlockSpec((ns, GD_CONV_CH), lambda i: (i, 0)),
                  pl.BlockSpec((ns, GD_CONV - 1, GD_CONV_CH), lambda i: (i, 0, 0)),
                  pl.BlockSpec((ns, GD_V_DIM), lambda i: (i, 0)),
                  pl.BlockSpec((ns, 2 * GD_HV), lambda i: (i, 0)),
                  _full(cw.shape), _full(alog.shape), _full(dtb.shape), _full(ng.shape),
                  pl.BlockSpec((ns, GD_HV, GD_DH, GD_DH), lambda i: (i, 0, 0, 0))],
        out_specs=[pl.BlockSpec((ns, GD_V_DIM), lambda i: (i, 0)),
                   pl.BlockSpec((ns, GD_HV, GD_DH, GD_DH), lambda i: (i, 0, 0, 0))],
        out_shape=[jax.ShapeDtypeStruct((n, GD_V_DIM), BF16),
                   jax.ShapeDtypeStruct((n, GD_HV, GD_DH, GD_DH), F32)],
        scratch_shapes=[pltpu.VMEM((ns, GD_V_DIM), F32)],
        compiler_params=_cparams(("arbitrary",)),
        name="gdn_step",
    )(cur, st, z, ba, cw, alog, dtb, ng, s0)


def _row_copy(src_hbm, row, dst, slot, sem):
    return pltpu.make_async_copy(src_hbm.at[pl.ds(pl.multiple_of(row * ROW_SLABS, ROW_SLABS), ROW_SLABS)],
                                 dst.at[pl.ds(pl.multiple_of(slot * ROW_SLABS, ROW_SLABS), ROW_SLABS)], sem)


def _start_row_gather(src_hbm, idx_ref, base, stride, n, dst, sem, both_queues=False):
    def issue(blk, c):
        r0 = blk * DMA_UNROLL
        for u in range(DMA_UNROLL):
            _row_copy(src_hbm, idx_ref[base + stride * (r0 + u)], dst, r0 + u, sem).start(
                priority=u % 2 if both_queues else 0)
        return c

    lax.fori_loop(0, n // DMA_UNROLL, issue, 0)


def _wait_row_gather(src_hbm, n, dst, sem):
    pltpu.make_async_copy(src_hbm.at[pl.ds(0, n * ROW_SLABS)], dst, sem).wait()


def _combine_kernel(dest_ref, x_ref, pv_ref, g_ref, b_ref, y_hbm, o_ref, buf_ref, sem, *, first_block):
    tt = o_ref.shape[0]
    i = pl.program_id(0)
    slot = i % 2

    def start(blk, into):
        base = (blk + first_block) * tt * TOP_K
        for k in range(TOP_K):
            _start_row_gather(y_hbm, dest_ref, base + k, TOP_K, tt, buf_ref.at[into, k], sem.at[into, k],
                              both_queues=True)

    @pl.when(i == 0)
    def _():
        start(0, 0)

    @pl.when(i + 1 < pl.num_programs(0))
    def _():
        start(i + 1, 1 - slot)

    for k in range(TOP_K):
        _wait_row_gather(y_hbm, tt, buf_ref.at[slot, k], sem.at[slot, k])
    pv = pv_ref[...]
    slabs = []
    for s in range(ROW_SLABS):
        y = pv[:, 0:1] * _slab(buf_ref.at[slot, 0], s, tt) + pv[:, 1:2] * _slab(buf_ref.at[slot, 1], s, tt)
        slabs.append(DEEPNORM_ALPHA * x_ref[:, s * LANES:(s + 1) * LANES] + y)
    mu = sum(jnp.sum(r, axis=-1, keepdims=True) for r in slabs) * (1.0 / D_MODEL)
    var = sum(jnp.sum((r - mu) * (r - mu), axis=-1, keepdims=True) for r in slabs) * (1.0 / D_MODEL)
    inv = lax.rsqrt(var + LN_EPS)
    for s in range(ROW_SLABS):
        cs = slice(s * LANES, (s + 1) * LANES)
        o_ref[:, cs] = (slabs[s] - mu) * inv * g_ref[:, cs] + b_ref[:, cs]


def combine(dest, x, pv, g, b, yb3, tt, first_block, n_blocks):
    grid_spec = pltpu.PrefetchScalarGridSpec(
        num_scalar_prefetch=1,
        grid=(n_blocks,),
        in_specs=[pl.BlockSpec((tt, D_MODEL), lambda i, dst: (i + first_block, 0)),
                  pl.BlockSpec((tt, TOP_K), lambda i, dst: (i + first_block, 0)),
                  pl.BlockSpec(g.shape, lambda i, dst: (0, 0)),
                  pl.BlockSpec(b.shape, lambda i, dst: (0, 0)),
                  pl.BlockSpec(memory_space=pl.ANY)],
        out_specs=pl.BlockSpec((tt, D_MODEL), lambda i, dst: (i, 0)),
        scratch_shapes=[pltpu.VMEM((2, TOP_K, tt * ROW_SLABS, LANES), F32), pltpu.SemaphoreType.DMA((2, TOP_K))],
    )
    return pl.pallas_call(
        functools.partial(_combine_kernel, first_block=first_block),
        grid_spec=grid_spec,
        out_shape=jax.ShapeDtypeStruct((n_blocks * tt, D_MODEL), F32),
        compiler_params=_cparams(("arbitrary",)),
        name="moe_combine",
    )(dest, x, pv, g, b, yb3)


def moe_experts(xb, topi, w1, w3, w2, tm):
    n = topi.shape[0]
    eid = topi.reshape(-1)
    onehot = (eid[:, None] == jnp.arange(N_EXPERTS, dtype=jnp.int32)[None, :]).astype(jnp.int32)
    csum = jnp.cumsum(onehot, axis=0)
    rank = jnp.sum((csum - onehot) * onehot, axis=1)
    counts = csum[-1]
    padded = (counts + tm - 1) // tm * tm
    pends = jnp.cumsum(padded)
    pstarts = pends - padded
    dest = (pstarts[eid] + rank).astype(jnp.int32)
    nb = (n * TOP_K + tm - 1) // tm + N_EXPERTS
    n_rows = nb * tm
    block_start = jnp.arange(nb, dtype=jnp.int32) * tm
    block_e = jnp.minimum(jnp.searchsorted(pends, block_start, side="right"), N_EXPERTS - 1).astype(jnp.int32)
    n_used = (pends[-1] // tm).astype(jnp.int32).reshape(1)
    n_chunks = n // TOKEN_CHUNK
    chunk_end = csum[TOKEN_CHUNK * TOP_K - 1::TOKEN_CHUNK * TOP_K]
    ends = chunk_end[:, block_e]
    first_entry = block_start - pstarts[block_e]
    last_entry = jnp.minimum(first_entry + tm, counts[block_e]) - 1
    chunk0 = jnp.minimum(jnp.sum(ends <= first_entry[None, :], axis=0), n_chunks - 1).astype(jnp.int32)
    chunk1 = jnp.minimum(jnp.sum(ends <= last_entry[None, :], axis=0), n_chunks - 1).astype(jnp.int32)
    chunk_cnt = jnp.maximum(chunk1 - chunk0 + 1, 1).astype(jnp.int32)
    dest_t = dest.reshape(n, TOP_K).T

    yb3 = swiglu_experts(block_e, n_used, chunk0, chunk_cnt, first_entry.astype(jnp.int32),
                         chunk_end.reshape(-1).astype(jnp.int32), xb, dest_t, w1, w3, w2, n_rows, tm)
    return yb3, dest


def kernel(x_prompt, x_sample, state_conv_a, state_conv_b, state_delta, meta_tokens, ln_g, ln_b, sc_w_in, sc_conv, sc_w_out, ffn_w1, ffn_w3, ffn_w2, gd_w_in, gd_conv, gd_a_log, gd_dt_bias, gd_norm_g, gd_w_out, moe_router, moe_w1, moe_w3, moe_w2):
    bsz, seq, d = x_prompt.shape
    n_s = x_sample.shape[0]
    n_small = N_META + n_s
    n_p = bsz * seq

    def row(v):
        return v.reshape(1, -1)

    sc_w_in_b = sc_w_in[0].astype(BF16)
    sc_w_out_b = sc_w_out[0].astype(BF16)
    ffn_w1_b, ffn_w3_b, ffn_w2_b = ffn_w1.astype(BF16), ffn_w3.astype(BF16), ffn_w2.astype(BF16)
    gd_w_qkv_b = gd_w_in[0][:, :GD_CONV_CH].astype(BF16)
    gd_w_z_b = gd_w_in[0][:, GD_CONV_CH:GD_CONV_CH + GD_V_DIM].astype(BF16)
    gd_w_ba_b = gd_w_in[0][:, GD_CONV_CH + GD_V_DIM:].astype(BF16)
    gd_w_out_b = gd_w_out[0].astype(BF16)
    moe_w1_b, moe_w3_b, moe_w2_b = moe_w1[0].astype(BF16), moe_w3[0].astype(BF16), moe_w2[0].astype(BF16)

    x_small = jnp.concatenate([meta_tokens.astype(F32), x_sample.reshape(n_s, d)], axis=0)

    xa_s, ch_s = l0_mix_small(x_small, state_conv_a[0, :, 0], state_conv_a[0, :, 1], sc_w_in_b, sc_w_out_b,
                              sc_conv[0], row(ln_g[0, 0]), row(ln_b[0, 0]))
    xa_p, tail_a = l0_mix_prompt(x_prompt, sc_w_in_b, sc_w_out_b, sc_conv[0], row(ln_g[0, 0]), row(ln_b[0, 0]),
                                 ch_s[N_META - HALO:N_META])
    def dense_ffn(x, tm):
        return swiglu_ln(x, ffn_w1_b[0], ffn_w3_b[0], ffn_w2_b[0], row(ln_g[0, 1]), row(ln_b[0, 1]), tm)

    xb_s = dense_ffn(xa_s, n_small)
    xb_p = dense_ffn(xa_p.reshape(n_p, d), 512)

    qkv_s, z_s, ba_s = gdn_inproj(xb_s, gd_w_qkv_b, gd_w_z_b, gd_w_ba_b, n_small)
    qkv_p, z_p, ba_p = gdn_inproj(xb_p, gd_w_qkv_b, gd_w_z_b, gd_w_ba_b, 1024)

    pad = CHUNK - N_META

    def meta_pad(a):
        return jnp.pad(a[:N_META], ((0, pad), (0, 0)))[None]

    ba_m = meta_pad(ba_s)
    o_m, _, s_meta = gdn_chunked(meta_pad(qkv_s), meta_pad(z_s), ba_m, jnp.swapaxes(ba_m, 1, 2), gd_conv[0],
                                 gd_a_log[0], gd_dt_bias[0], gd_norm_g[0],
                                 jnp.zeros((HALO, GD_CONV_CH), F32), jnp.zeros((GD_HV, GD_DH, GD_DH), F32),
                                 CHUNK, valid_rows=N_META)
    o_smp, s_smp = gdn_step(qkv_s[N_META:], state_conv_b[0], z_s[N_META:], ba_s[N_META:], gd_conv[0],
                            gd_a_log[0], gd_dt_bias[0], gd_norm_g[0], state_delta[0])
    ba_p3 = ba_p.reshape(bsz, seq, 2 * GD_HV)
    o_p, tail_b, s_p = gdn_chunked(qkv_p.reshape(bsz, seq, GD_CONV_CH), z_p.reshape(bsz, seq, GD_V_DIM), ba_p3,
                                   jnp.swapaxes(ba_p3, 1, 2), gd_conv[0], gd_a_log[0], gd_dt_bias[0],
                                   gd_norm_g[0], qkv_s[N_META - HALO:N_META], s_meta[0], 256)
    tt = TOKEN_BLOCK
    small_pad = ((0, tt - n_small), (0, 0))
    o_s = jnp.pad(jnp.concatenate([o_m[0, :N_META], o_smp], axis=0), small_pad)
    xc, xc_b, topi, topv = matmul_ln_router(o_p.reshape(n_p, GD_V_DIM), xb_p, o_s, jnp.pad(xb_s, small_pad),
                                            gd_w_out_b, row(ln_g[1, 0]), row(ln_b[1, 0]), moe_router[0])
    yb3, dest = moe_experts(xc_b, topi, moe_w1_b, moe_w3_b, moe_w2_b, EXPERT_BLOCK)
    xd_p = combine(dest, xc, topv, row(ln_g[1, 1]), row(ln_b[1, 1]), yb3, tt, 0, n_p // tt)
    xd_s = combine(dest, xc, topv, row(ln_g[1, 1]), row(ln_b[1, 1]), yb3, tt, n_p // tt, 1)

    y_prompt = xd_p.reshape(bsz, seq, d)
    y_sample = xd_s[N_META:n_small].reshape(n_s, 1, d)
    new_conv_a_prompt = tail_a[None, :, HALO - (SC_WIDTH - 1):]
    new_conv_b_prompt = tail_b[None, :, HALO - (GD_CONV - 1):]
    new_delta_prompt = s_p[None]
    new_conv_a_sample = jnp.stack([state_conv_a[0, :, 1], ch_s[N_META:]], axis=1)[None]
    new_conv_b_sample = jnp.concatenate([state_conv_b[0, :, 1:], qkv_s[N_META:, None]], axis=1)[None]
    new_delta_sample = s_smp[None]
    return (y_prompt, y_sample, new_conv_a_prompt, new_conv_b_prompt, new_delta_prompt,
            new_conv_a_sample, new_conv_b_sample, new_delta_sample)
```

```python
import functools

import jax
import jax.numpy as jnp
from jax import lax
from jax.experimental import pallas as pl
from jax.experimental.pallas import tpu as pltpu

F32 = jnp.float32
BF16 = jnp.bfloat16

D_MODEL = 1024
N_META = 16
SC_WIDTH = 3
GD_HK = 8
GD_HV = 16
GD_DH = 128
GD_K_DIM = GD_HK * GD_DH
GD_V_DIM = GD_HV * GD_DH
GD_CONV = 4
GD_CONV_CH = 2 * GD_K_DIM + GD_V_DIM
D_FF = 3584
N_EXPERTS = 8
TOP_K = 2
LN_EPS = 1e-5
RMS_EPS = 1e-6
DEPTH = 2
DEEPNORM_ALPHA = (2 * DEPTH) ** 0.25

LANES = 128
ROW_SLABS = D_MODEL // LANES
DMA_UNROLL = 8
EXPERT_F_STEPS = 2
TOKEN_CHUNK = 256
CHUNK_RING = 12
SEL_ALIGN = 16
SEL_ROWS = TOKEN_CHUNK + SEL_ALIGN
TOKEN_BLOCK = 512
EXPERT_BLOCK = 512
CHUNK = 64
HALO = 8
V7X_VMEM_LIMIT = 56 * 1024 * 1024


def _cparams(sem, vmem=V7X_VMEM_LIMIT):
    return pltpu.CompilerParams(dimension_semantics=sem, vmem_limit_bytes=vmem)


def _bdot(a, b):
    return jnp.dot(a.astype(BF16), b.astype(BF16), preferred_element_type=F32)


def _bdot_nt(a, b):
    return lax.dot_general(a.astype(BF16), b.astype(BF16), (((1,), (1,)), ((), ())),
                           preferred_element_type=F32)


def _bdot_tn(a, b):
    return lax.dot_general(a.astype(BF16), b.astype(BF16), (((0,), (0,)), ((), ())),
                           preferred_element_type=F32)


def _bf16_pieces(x):
    p0 = x.astype(BF16)
    r1 = x - p0.astype(F32)
    p1 = r1.astype(BF16)
    p2 = (r1 - p1.astype(F32)).astype(BF16)
    return p0, p1, p2


def _dot_f32x3(a, b):
    a_hi = a.astype(BF16)
    b_hi = b.astype(BF16)
    a_lo = (a - a_hi.astype(F32)).astype(BF16)
    b_lo = (b - b_hi.astype(F32)).astype(BF16)
    return (jnp.dot(a_hi, b_hi, preferred_element_type=F32) + jnp.dot(a_lo, b_hi, preferred_element_type=F32)
            + jnp.dot(a_hi, b_lo, preferred_element_type=F32))


def _layer_norm(r, g, b):
    mu = jnp.mean(r, axis=-1, keepdims=True)
    d = r - mu
    var = jnp.mean(d * d, axis=-1, keepdims=True)
    return d * lax.rsqrt(var + LN_EPS) * g + b


def _silu(x):
    return x * jax.nn.sigmoid(x)


def _l0_inproj(xb, w_in_ref, ch_ref, bg_ref, row0, rows):
    col_chunk = 512
    for j in range(D_MODEL // col_chunk):
        lo, hi = j * col_chunk, (j + 1) * col_chunk
        bg = jnp.dot(xb, w_in_ref[:, lo:hi], preferred_element_type=F32)
        c = jnp.dot(xb, w_in_ref[:, D_MODEL + lo:D_MODEL + hi], preferred_element_type=F32)
        h = jnp.dot(xb, w_in_ref[:, 2 * D_MODEL + lo:2 * D_MODEL + hi], preferred_element_type=F32)
        ch_ref[row0:row0 + rows, lo:hi] = c * h
        bg_ref[:, lo:hi] = bg


def _l0_prompt_kernel(x_ref, w_in_ref, w_out_ref, cw_ref, g_ref, b_ref, carry_ref,
                      o_ref, tail_ref, buf_ref, bg_ref):
    tm = x_ref.shape[0]
    t = pl.program_id(1)

    @pl.when(t == 0)
    def _():
        buf_ref[0:HALO, :] = carry_ref[...]

    x = x_ref[...]
    _l0_inproj(x.astype(BF16), w_in_ref, buf_ref, bg_ref, HALO, tm)
    y = (cw_ref[0:1, :] * buf_ref[HALO - 2:HALO - 2 + tm, :]
         + cw_ref[1:2, :] * buf_ref[HALO - 1:HALO - 1 + tm, :]
         + cw_ref[2:3, :] * buf_ref[HALO:HALO + tm, :])
    u = (bg_ref[...] * y).astype(BF16)
    m = jnp.dot(u, w_out_ref[...], preferred_element_type=F32)
    o_ref[...] = _layer_norm(DEEPNORM_ALPHA * x + m, g_ref[...], b_ref[...])
    tail = buf_ref[tm:tm + HALO, :]
    tail_ref[...] = tail
    buf_ref[0:HALO, :] = tail


def _l0_small_kernel(x_ref, st0_ref, st1_ref, w_in_ref, w_out_ref, cw_ref, g_ref, b_ref,
                     o_ref, ch_out_ref, buf_ref, bg_ref):
    n = x_ref.shape[0]
    x = x_ref[...]
    buf_ref[0:HALO, :] = jnp.zeros((HALO, D_MODEL), F32)
    _l0_inproj(x.astype(BF16), w_in_ref, buf_ref, bg_ref, HALO, n)
    y_meta = (cw_ref[0:1, :] * buf_ref[HALO - 2:HALO - 2 + N_META, :]
              + cw_ref[1:2, :] * buf_ref[HALO - 1:HALO - 1 + N_META, :]
              + cw_ref[2:3, :] * buf_ref[HALO:HALO + N_META, :])
    ch = buf_ref[HALO:HALO + n, :]
    y_s = (cw_ref[0:1, :] * st0_ref[...] + cw_ref[1:2, :] * st1_ref[...]
           + cw_ref[2:3, :] * ch[N_META:, :])
    y = jnp.concatenate([y_meta, y_s], axis=0)
    u = (bg_ref[...] * y).astype(BF16)
    m = jnp.dot(u, w_out_ref[...], preferred_element_type=F32)
    o_ref[...] = _layer_norm(DEEPNORM_ALPHA * x + m, g_ref[...], b_ref[...])
    ch_out_ref[...] = ch


def _full(shape):
    nd = len(shape)
    return pl.BlockSpec(shape, lambda *_: (0,) * nd)


def l0_mix_prompt(x, w_in, w_out, cw, g, b, carry, tm=512):
    bsz, seq, d = x.shape
    return pl.pallas_call(
        _l0_prompt_kernel,
        grid=(bsz, seq // tm),
        in_specs=[pl.BlockSpec((None, tm, d), lambda i, t: (i, t, 0)),
                  _full(w_in.shape), _full(w_out.shape), _full(cw.shape), _full(g.shape), _full(b.shape),
                  _full(carry.shape)],
        out_specs=[pl.BlockSpec((None, tm, d), lambda i, t: (i, t, 0)),
                   pl.BlockSpec((None, HALO, d), lambda i, t: (i, 0, 0))],
        out_shape=[jax.ShapeDtypeStruct((bsz, seq, d), F32),
                   jax.ShapeDtypeStruct((bsz, HALO, d), F32)],
        scratch_shapes=[pltpu.VMEM((tm + HALO, d), F32), pltpu.VMEM((tm, d), F32)],
        compiler_params=_cparams(("arbitrary", "arbitrary")),
        name="l0_mix_prompt",
    )(x, w_in, w_out, cw, g, b, carry)


def l0_mix_small(x, st0, st1, w_in, w_out, cw, g, b):
    n, d = x.shape
    return pl.pallas_call(
        _l0_small_kernel,
        grid=(1,),
        in_specs=[_full(a.shape) for a in (x, st0, st1, w_in, w_out, cw, g, b)],
        out_specs=[_full((n, d)), _full((n, d))],
        out_shape=[jax.ShapeDtypeStruct((n, d), F32), jax.ShapeDtypeStruct((n, d), F32)],
        scratch_shapes=[pltpu.VMEM((n + HALO, d), F32), pltpu.VMEM((n, d), F32)],
        compiler_params=_cparams(("arbitrary",)),
        name="l0_mix_small",
    )(x, st0, st1, w_in, w_out, cw, g, b)


def _slab(ref, s, rows):
    return ref[pl.ds(s, rows, stride=ROW_SLABS), :]


def _to_slabs(o_ref, val):
    rows = val.shape[0]
    for s in range(ROW_SLABS):
        o_ref[pl.ds(s, rows, stride=ROW_SLABS), :] = val[:, s * LANES:(s + 1) * LANES]


def _swiglu_partial(xb, w1_ref, w3_ref, w2_ref, acc_ref):
    a = jnp.dot(xb, w1_ref[...], preferred_element_type=F32)
    b = jnp.dot(xb, w3_ref[...], preferred_element_type=F32)
    h = (_silu(a) * b).astype(BF16)
    part = jnp.dot(h, w2_ref[...], preferred_element_type=F32)
    acc_ref[...] = jnp.where(pl.program_id(1) > 0, acc_ref[...], 0.0) + part


def _zero_acc_once(acc_ref):
    @pl.when((pl.program_id(0) == 0) & (pl.program_id(1) == 0))
    def _():
        acc_ref[...] = jnp.zeros(acc_ref.shape, acc_ref.dtype)


def _swiglu_ln_kernel(x_ref, w1_ref, w3_ref, w2_ref, g_ref, b_ref, o_ref, acc_ref):
    _zero_acc_once(acc_ref)
    _swiglu_partial(x_ref[...].astype(BF16), w1_ref, w3_ref, w2_ref, acc_ref)

    @pl.when(pl.program_id(1) == pl.num_programs(1) - 1)
    def _():
        o_ref[...] = _layer_norm(DEEPNORM_ALPHA * x_ref[...] + acc_ref[...], g_ref[...], b_ref[...])


def swiglu_ln(x, w1, w3, w2, g, b, tm, tf=1792):
    n, d = x.shape
    return pl.pallas_call(
        _swiglu_ln_kernel,
        grid=(n // tm, D_FF // tf),
        in_specs=[pl.BlockSpec((tm, d), lambda i, f: (i, 0)),
                  pl.BlockSpec((d, tf), lambda i, f: (0, f)),
                  pl.BlockSpec((d, tf), lambda i, f: (0, f)),
                  pl.BlockSpec((tf, d), lambda i, f: (f, 0)),
                  _full(g.shape), _full(b.shape)],
        out_specs=pl.BlockSpec((tm, d), lambda i, f: (i, 0)),
        out_shape=jax.ShapeDtypeStruct((n, d), F32),
        scratch_shapes=[pltpu.VMEM((tm, d), F32)],
        compiler_params=_cparams(("arbitrary", "arbitrary")),
        name="swiglu_ln",
    )(x, w1, w3, w2, g, b)


def _token_chunk_copy(x_hbm, chunk, buf_ref, slot, sem):
    start = pl.multiple_of(chunk * TOKEN_CHUNK, TOKEN_CHUNK)
    return pltpu.make_async_copy(x_hbm.at[pl.ds(start, TOKEN_CHUNK)], buf_ref.at[slot], sem.at[slot])


def _swiglu_experts_kernel(be_ref, nu_ref, c0_ref, cn_ref, e0_ref, cend_ref, x_hbm, dest_ref, w1_ref, w3_ref,
                           w2_ref, o_ref, acc_ref, xsel_ref, xb_ref, chunk_ref, sem):
    tm = xb_ref.shape[0]
    i = pl.program_id(0)
    f = pl.program_id(1)
    nf = pl.num_programs(1)
    n_used = nu_ref[0]
    _zero_acc_once(acc_ref)

    def start_ring(blk):
        for slot in range(CHUNK_RING):
            @pl.when(slot < cn_ref[blk])
            def _():
                _token_chunk_copy(x_hbm, c0_ref[blk] + slot, chunk_ref, slot, sem).start()

    @pl.when(i < n_used)
    def _():
        @pl.when(f == 0)
        def _():
            first = c0_ref[i]
            count = cn_ref[i]
            expert = be_ref[i]
            row_iota = lax.broadcasted_iota(jnp.int32, (SEL_ROWS, TOKEN_CHUNK), 0)

            @pl.when(i == 0)
            def _():
                start_ring(0)

            xsel_ref[...] = jnp.zeros(xsel_ref.shape, xsel_ref.dtype)

            def pick(j, carry):
                slot = jnp.where(j < CHUNK_RING, j, 0)

                @pl.when(j >= CHUNK_RING)
                def _():
                    _token_chunk_copy(x_hbm, first + j, chunk_ref, 0, sem).start()

                chunk = first + j
                before = jnp.where(chunk > 0, cend_ref[jnp.maximum(chunk - 1, 0) * N_EXPERTS + expert], 0)
                row0 = jnp.clip(before - e0_ref[i], 0, tm - SEL_ROWS) // SEL_ALIGN * SEL_ALIGN
                row0 = pl.multiple_of(row0, SEL_ALIGN)
                row_id = row_iota + (i * tm + row0)
                _token_chunk_copy(x_hbm, chunk, chunk_ref, slot, sem).wait()
                lane0 = pl.multiple_of(chunk * TOKEN_CHUNK, TOKEN_CHUNK)
                hit = row_id == dest_ref[0:1, pl.ds(lane0, TOKEN_CHUNK)]
                for k in range(1, TOP_K):
                    hit = hit | (row_id == dest_ref[k:k + 1, pl.ds(lane0, TOKEN_CHUNK)])
                sel = jnp.where(hit, 1.0, 0.0).astype(BF16)
                xsel_ref[pl.ds(row0, SEL_ROWS), :] += jnp.dot(sel, chunk_ref[slot], preferred_element_type=F32)
                return carry

            lax.fori_loop(0, count, pick, 0)
            xb_ref[...] = xsel_ref[...].astype(BF16)

        @pl.when((f == nf - 1) & (i + 1 < n_used))
        def _():
            start_ring(i + 1)

        _swiglu_partial(xb_ref[...], w1_ref, w3_ref, w2_ref, acc_ref)

        @pl.when(f == nf - 1)
        def _():
            _to_slabs(o_ref, acc_ref[...])

    @pl.when(i >= nu_ref[0])
    def _():
        o_ref[...] = jnp.zeros(o_ref.shape, o_ref.dtype)


def swiglu_experts(block_e, n_used, chunk0, n_chunks, entry0, chunk_end, xb, dest_t, w1, w3, w2, n_rows, tm):
    d = D_MODEL
    nf = EXPERT_F_STEPS
    tf = D_FF // nf

    def f_idx(i, f, nu):
        return jnp.where(i < nu[0], f, nf - 1)

    grid_spec = pltpu.PrefetchScalarGridSpec(
        num_scalar_prefetch=6,
        grid=(n_rows // tm, nf),
        in_specs=[pl.BlockSpec(memory_space=pl.ANY),
                  pl.BlockSpec(dest_t.shape, lambda i, f, be, nu, c0, cn, e0, ce: (0, 0)),
                  pl.BlockSpec((None, d, tf), lambda i, f, be, nu, c0, cn, e0, ce: (be[i], 0, f_idx(i, f, nu))),
                  pl.BlockSpec((None, d, tf), lambda i, f, be, nu, c0, cn, e0, ce: (be[i], 0, f_idx(i, f, nu))),
                  pl.BlockSpec((None, tf, d), lambda i, f, be, nu, c0, cn, e0, ce: (be[i], f_idx(i, f, nu), 0))],
        out_specs=pl.BlockSpec((tm * ROW_SLABS, LANES), lambda i, f, be, nu, c0, cn, e0, ce: (i, 0)),
        scratch_shapes=[pltpu.VMEM((tm, d), F32),
                        pltpu.VMEM((tm, d), F32),
                        pltpu.VMEM((tm, d), BF16),
                        pltpu.VMEM((CHUNK_RING, TOKEN_CHUNK, d), BF16),
                        pltpu.SemaphoreType.DMA((CHUNK_RING,))],
    )
    return pl.pallas_call(
        _swiglu_experts_kernel,
        grid_spec=grid_spec,
        out_shape=jax.ShapeDtypeStruct((n_rows * ROW_SLABS, LANES), F32),
        compiler_params=_cparams(("arbitrary", "arbitrary")),
        name="swiglu_experts",
    )(block_e, n_used, chunk0, n_chunks, entry0, chunk_end, xb, dest_t, w1, w3, w2)


QKV_GROUPS = 2


def _gdn_inproj_kernel(x_ref, wqkv_ref, wz_ref, wba_ref, qkv_ref, z_ref, ba_ref):
    j = pl.program_id(1)
    xb = x_ref[...].astype(BF16)

    @pl.when(j < QKV_GROUPS)
    def _():
        qkv_ref[...] = jnp.dot(xb, wqkv_ref[...], preferred_element_type=F32)

    @pl.when(j == QKV_GROUPS)
    def _():
        z_ref[...] = jnp.dot(xb, wz_ref[...], preferred_element_type=F32).astype(z_ref.dtype)
        ba_ref[...] = jnp.dot(xb, wba_ref[...], preferred_element_type=F32)


def gdn_inproj(x, wqkv, wz, wba, tm):
    n, k = x.shape
    tn = wqkv.shape[1] // QKV_GROUPS

    def group(i, j):
        return (i, jnp.minimum(j, QKV_GROUPS - 1))

    return pl.pallas_call(
        _gdn_inproj_kernel,
        grid=(n // tm, QKV_GROUPS + 1),
        in_specs=[pl.BlockSpec((tm, k), lambda i, j: (i, 0)),
                  pl.BlockSpec((k, tn), lambda i, j: (0, jnp.minimum(j, QKV_GROUPS - 1))),
                  _full(wz.shape), _full(wba.shape)],
        out_specs=[pl.BlockSpec((tm, tn), group),
                   pl.BlockSpec((tm, wz.shape[1]), lambda i, j: (i, 0)),
                   pl.BlockSpec((tm, wba.shape[1]), lambda i, j: (i, 0))],
        out_shape=[jax.ShapeDtypeStruct((n, wqkv.shape[1]), F32),
                   jax.ShapeDtypeStruct((n, wz.shape[1]), BF16),
                   jax.ShapeDtypeStruct((n, wba.shape[1]), F32)],
        compiler_params=_cparams(("arbitrary", "arbitrary")),
        name="gdn_inproj",
    )(x, wqkv, wz, wba)


def _top2(logits):
    mx = jnp.max(logits, axis=-1, keepdims=True)
    ex = jnp.exp(logits - mx)
    p = ex / jnp.sum(ex, axis=-1, keepdims=True)
    lane = lax.broadcasted_iota(jnp.int32, p.shape, 1)
    p1 = jnp.max(p, axis=-1, keepdims=True)
    i1 = jnp.min(jnp.where(p == p1, lane, N_EXPERTS), axis=-1, keepdims=True)
    rest = jnp.where(lane == i1, -1.0, p)
    p2 = jnp.max(rest, axis=-1, keepdims=True)
    i2 = jnp.min(jnp.where(rest == p2, lane, N_EXPERTS), axis=-1, keepdims=True)
    tot = p1 + p2
    return jnp.concatenate([i1, i2], axis=-1), jnp.concatenate([p1 / tot, p2 / tot], axis=-1)


def _mm_ln_router_kernel(a_ref, x_ref, a_tail_ref, x_tail_ref, w_ref, g_ref, b_ref, wr_ref,
                         o_ref, ob_ref, idx_ref, val_ref, *, n_main):
    def block(a_blk, x_blk):
        m = jnp.dot(a_blk[...], w_ref[...], preferred_element_type=F32)
        y = _layer_norm(DEEPNORM_ALPHA * x_blk[...] + m, g_ref[...], b_ref[...])
        o_ref[...] = y
        ob_ref[...] = y.astype(BF16)
        logits = _dot_f32x3(y, wr_ref[...])
        idx_ref[...], val_ref[...] = _top2(logits)

    @pl.when(pl.program_id(0) < n_main)
    def _():
        block(a_ref, x_ref)

    @pl.when(pl.program_id(0) >= n_main)
    def _():
        block(a_tail_ref, x_tail_ref)


def matmul_ln_router(a, x, a_tail, x_tail, w, g, b, wr):
    tm, k = a_tail.shape
    n = a.shape[0]
    d = w.shape[1]
    n_main = n // tm
    n_all = n + tm

    def main(i):
        return (jnp.minimum(i, n_main - 1), 0)

    return pl.pallas_call(
        functools.partial(_mm_ln_router_kernel, n_main=n_main),
        grid=(n_main + 1,),
        in_specs=[pl.BlockSpec((tm, k), main), pl.BlockSpec((tm, d), main),
                  _full(a_tail.shape), _full(x_tail.shape),
                  _full(w.shape), _full(g.shape), _full(b.shape), _full(wr.shape)],
        out_specs=[pl.BlockSpec((tm, d), lambda i: (i, 0)), pl.BlockSpec((tm, d), lambda i: (i, 0)),
                   pl.BlockSpec((tm, TOP_K), lambda i: (i, 0)), pl.BlockSpec((tm, TOP_K), lambda i: (i, 0))],
        out_shape=[jax.ShapeDtypeStruct((n_all, d), F32), jax.ShapeDtypeStruct((n_all, d), BF16),
                   jax.ShapeDtypeStruct((n_all, TOP_K), jnp.int32), jax.ShapeDtypeStruct((n_all, TOP_K), F32)],
        compiler_params=_cparams(("arbitrary",)),
        name="matmul_ln_router",
    )(a, x, a_tail, x_tail, w, g, b, wr)


def _l2norm_heads(x, n_heads, scale):
    outs = []
    for h in range(n_heads):
        xh = x[:, h * GD_DH:(h + 1) * GD_DH]
        ss = jnp.sum(xh * xh, axis=-1, keepdims=True)
        outs.append(xh * (lax.rsqrt(ss + RMS_EPS) * scale))
    return outs


def _softplus(x):
    return jnp.maximum(x, 0.0) + jnp.log1p(jnp.exp(-jnp.abs(x)))


def _gated_rmsnorm(o, z, norm_g):
    ms = jnp.mean(o * o, axis=-1, keepdims=True)
    return o * lax.rsqrt(ms + RMS_EPS) * norm_g * _silu(z)


PAIR = GD_HV // GD_HK
PAIR_SQ = PAIR * CHUNK
PAIR_DV = PAIR * GD_DH


def _sq_cols(x, p):
    lane = lax.broadcasted_iota(jnp.int32, (CHUNK, PAIR_SQ), 1)
    a = jnp.broadcast_to(x[:, PAIR * p:PAIR * p + 1], (CHUNK, PAIR_SQ))
    b = jnp.broadcast_to(x[:, PAIR * p + 1:PAIR * p + 2], (CHUNK, PAIR_SQ))
    return jnp.where(lane < CHUNK, a, b)


def _block_diag_sq(m):
    lane = lax.broadcasted_iota(jnp.int32, m.shape, 1)
    top = jnp.where(lane < CHUNK, m, 0.0).astype(BF16)
    bot = jnp.where(lane < CHUNK, 0.0, m).astype(BF16)
    return jnp.concatenate([top, bot], axis=0)


def _block_diag_dv(m):
    mb = m.astype(BF16)
    zero = jnp.zeros((CHUNK, GD_DH), BF16)
    top = jnp.concatenate([mb[:, :GD_DH], zero], axis=1)
    bot = jnp.concatenate([zero, mb[:, GD_DH:]], axis=1)
    return jnp.concatenate([top, bot], axis=0)


def _tri_inverse_pairs(a_list):
    row = lax.broadcasted_iota(jnp.int32, (CHUNK, PAIR_SQ), 0)
    col = lax.broadcasted_iota(jnp.int32, (CHUNK, PAIR_SQ), 1) % CHUNK
    eye = jnp.where(row == col, 1.0, 0.0).astype(F32)
    s = 1
    inv = None
    while s < CHUNK:
        sub = ((row // (2 * s)) == (col // (2 * s))) & ((row // s) % 2 == 1) & ((col // s) % 2 == 0)
        e_list = [jnp.where(sub, a, 0.0) for a in a_list]
        if s == 1:
            inv = [eye - e for e in e_list]
        else:
            x_list = [jnp.dot(e.astype(BF16), _block_diag_sq(d), preferred_element_type=F32)
                      for e, d in zip(e_list, inv)]
            y_list = [jnp.dot(d.astype(BF16), _block_diag_sq(x), preferred_element_type=F32)
                      for d, x in zip(inv, x_list)]
            inv = [d - y for d, y in zip(inv, y_list)]
        s *= 2
    return inv


def _gdn_chunk_kernel(qkv_ref, z_ref, ba_ref, bat_ref, cw_ref, alog_ref, dtb_ref, alogt_ref, dtbt_ref,
                      ng_ref, carry_ref, s0_ref, o_ref, tail_ref, sout_ref,
                      buf_ref, s_ref, q_ref, k_ref, v_ref, lhs_w_ref, lhs_o_ref, glast_ref,
                      *, valid_rows):
    tc = qkv_ref.shape[0]
    t = pl.program_id(1)

    @pl.when(t == 0)
    def _():
        for c in range(GD_CONV_CH // LANES):
            buf_ref[c, 0:HALO, :] = carry_ref[:, c * LANES:(c + 1) * LANES]
        for h in range(GD_HV):
            j = h % PAIR
            s_ref[h // PAIR, :, j * GD_DH:(j + 1) * GD_DH] = s0_ref[h]

    for c in range(GD_CONV_CH // LANES):
        lo, hi = c * LANES, (c + 1) * LANES
        buf_ref[c, HALO:HALO + tc, :] = qkv_ref[:, lo:hi]
        y = cw_ref[GD_CONV - 1:GD_CONV, lo:hi] * qkv_ref[:, lo:hi]
        for i in range(GD_CONV - 1):
            off = HALO - (GD_CONV - 1) + i
            y = y + cw_ref[i:i + 1, lo:hi] * buf_ref[c, off:off + tc, :]
        act = _silu(y)
        if lo < GD_K_DIM:
            q_ref[:, lo:hi] = _l2norm_heads(act, 1, GD_DH ** -0.5)[0].astype(BF16)
        elif lo < 2 * GD_K_DIM:
            k_ref[:, lo - GD_K_DIM:hi - GD_K_DIM] = _l2norm_heads(act, 1, 1.0)[0].astype(BF16)
        else:
            v_ref[:, lo - 2 * GD_K_DIM:hi - 2 * GD_K_DIM] = act
        tail = buf_ref[c, tc:tc + HALO, :]
        tail_ref[:, lo:hi] = tail
        buf_ref[c, 0:HALO, :] = tail

    row = lax.broadcasted_iota(jnp.int32, (CHUNK, CHUNK), 0)
    col = lax.broadcasted_iota(jnp.int32, (CHUNK, CHUNK), 1)
    tril_ones = jnp.where(row >= col, 1.0, 0.0).astype(BF16)
    triu_ones = jnp.where(row <= col, 1.0, 0.0).astype(BF16)
    row2 = lax.broadcasted_iota(jnp.int32, (CHUNK, PAIR_SQ), 0)
    col2 = lax.broadcasted_iota(jnp.int32, (CHUNK, PAIR_SQ), 1) % CHUNK
    incl2 = row2 >= col2
    strict2 = row2 > col2
    n_pairs = GD_HK

    eye2 = jnp.where(row2 == col2, 1.0, 0.0).astype(F32)
    n_chunks = tc // CHUNK

    def pair_rows(xt, p):
        return jnp.concatenate([xt[PAIR * p + j:PAIR * p + j + 1, :] for j in range(PAIR)], axis=1)

    items = [(c, p) for c in range(n_chunks) for p in range(n_pairs)]
    cums, cumts, betas, betats = [], [], [], []
    for c in range(n_chunks):
        ba = ba_ref[c]
        bat = bat_ref[c]
        beta = jax.nn.sigmoid(ba[:, :GD_HV])
        betat = jax.nn.sigmoid(bat[:GD_HV, :])
        g = -jnp.exp(alog_ref[...]) * _softplus(ba[:, GD_HV:] + dtb_ref[...])
        gt = -jnp.exp(alogt_ref[...]) * _softplus(bat[GD_HV:, :] + dtbt_ref[...])
        if valid_rows is not None:
            rid = lax.broadcasted_iota(jnp.int32, (CHUNK, GD_HV), 0) + c * CHUNK + t * tc
            beta = jnp.where(rid < valid_rows, beta, 0.0)
            g = jnp.where(rid < valid_rows, g, 0.0)
            cid = lax.broadcasted_iota(jnp.int32, (GD_HV, CHUNK), 1) + c * CHUNK + t * tc
            betat = jnp.where(cid < valid_rows, betat, 0.0)
            gt = jnp.where(cid < valid_rows, gt, 0.0)
        cum = sum(jnp.dot(tril_ones, piece, preferred_element_type=F32) for piece in _bf16_pieces(g))
        cumt = sum(jnp.dot(piece, triu_ones, preferred_element_type=F32) for piece in _bf16_pieces(gt))
        glast_ref[c] = jnp.concatenate(
            [jnp.broadcast_to(jnp.exp(cum[CHUNK - 1:CHUNK, h:h + 1]), (1, GD_DH)) for h in range(GD_HV)], axis=1)
        cums.append(cum)
        cumts.append(cumt)
        betas.append(beta)
        betats.append(betat)

    grams = []
    for c, p in items:
        kb = k_ref[c * CHUNK:(c + 1) * CHUNK, p * GD_DH:(p + 1) * GD_DH]
        qb = q_ref[c * CHUNK:(c + 1) * CHUNK, p * GD_DH:(p + 1) * GD_DH]
        grams.append(lax.dot_general(jnp.concatenate([kb, qb], axis=0), jnp.concatenate([kb, kb], axis=0),
                                     (((1,), (1,)), ((), ())), preferred_element_type=F32))
    a_list, attn_list = [], []
    for (c, p), gram in zip(items, grams):
        diff = _sq_cols(cums[c], p) - pair_rows(cumts[c], p)
        dec = jnp.where(incl2, jnp.exp(jnp.where(incl2, diff, 0.0)), 0.0)
        attn_list.append(gram[CHUNK:] * dec)
        a_list.append(jnp.where(strict2, dec * gram[:CHUNK], 0.0) * _sq_cols(betas[c], p))
    inv_list = _tri_inverse_pairs(a_list)
    for (c, p), inv, attn in zip(items, inv_list, attn_list):
        cum_row = pair_rows(cumts[c], p)
        last = jnp.concatenate([jnp.broadcast_to(cumts[c][PAIR * p + j:PAIR * p + j + 1, CHUNK - 1:CHUNK], (1, CHUNK))
                                for j in range(PAIR)], axis=1)
        ecum_row = jnp.exp(cum_row)
        t_beta = inv * pair_rows(betats[c], p)
        lhs_w_ref[c, p] = jnp.concatenate([t_beta, -(t_beta * ecum_row)], axis=1).astype(BF16)
        top = jnp.concatenate([attn, eye2 * ecum_row], axis=1)
        bot = jnp.concatenate([eye2 * jnp.exp(last - cum_row), jnp.zeros((CHUNK, PAIR_SQ), F32)], axis=1)
        lhs_o_ref[c, p] = jnp.concatenate([top, bot], axis=0).astype(BF16)

    def state_body(c, carry):
        r0 = pl.multiple_of(c * CHUNK, CHUNK)
        kbs, s_old, kqs = [], [], []
        for p in range(n_pairs):
            kb = k_ref[pl.ds(r0, CHUNK), p * GD_DH:(p + 1) * GD_DH]
            qb = q_ref[pl.ds(r0, CHUNK), p * GD_DH:(p + 1) * GD_DH]
            s2 = s_ref[p]
            kbs.append(kb)
            s_old.append(s2)
            kqs.append(jnp.dot(jnp.concatenate([kb, qb], axis=0), s2.astype(BF16), preferred_element_type=F32))
        ws = []
        for p in range(n_pairs):
            v2 = v_ref[pl.ds(r0, CHUNK), p * PAIR_DV:(p + 1) * PAIR_DV]
            rhs = jnp.concatenate([_block_diag_dv(v2), _block_diag_dv(kqs[p][:CHUNK])], axis=0)
            ws.append(jnp.dot(lhs_w_ref[c, p], rhs, preferred_element_type=F32))
        for p in range(n_pairs):
            rhs = jnp.concatenate([_block_diag_dv(ws[p]), _block_diag_dv(kqs[p][CHUNK:])], axis=0)
            ow = jnp.dot(lhs_o_ref[c, p], rhs, preferred_element_type=F32)
            o2 = ow[:CHUNK]
            upd = lax.dot_general(kbs[p], ow[CHUNK:].astype(BF16), (((0,), (0,)), ((), ())),
                                  preferred_element_type=F32)
            s_ref[p] = glast_ref[c, :, p * PAIR_DV:(p + 1) * PAIR_DV] * s_old[p] + upd
            for j in range(PAIR):
                h = PAIR * p + j
                z = z_ref[pl.ds(r0, CHUNK), h * GD_DH:(h + 1) * GD_DH].astype(F32)
                o_ref[pl.ds(r0, CHUNK), h * GD_DH:(h + 1) * GD_DH] = _gated_rmsnorm(
                    o2[:, j * GD_DH:(j + 1) * GD_DH], z, ng_ref[...]).astype(o_ref.dtype)
        return carry

    lax.fori_loop(0, tc // CHUNK, state_body, 0)
    for h in range(GD_HV):
        j = h % PAIR
        sout_ref[h] = s_ref[h // PAIR, :, j * GD_DH:(j + 1) * GD_DH]


def gdn_chunked(qkv, z, ba, bat, cw, a_log, dt_bias, norm_g, carry, s0, tc, valid_rows=None):
    bsz, seq, _ = qkv.shape
    nchunk = tc // CHUNK
    ba4 = ba.reshape(bsz, seq // CHUNK, CHUNK, 2 * GD_HV)
    bat4 = bat.reshape(bsz, 2 * GD_HV, seq // CHUNK, CHUNK).transpose(0, 2, 1, 3)
    alog = a_log.reshape(1, GD_HV)
    dtb = dt_bias.reshape(1, GD_HV)
    alogt = a_log.reshape(GD_HV, 1)
    dtbt = dt_bias.reshape(GD_HV, 1)
    ng = norm_g.reshape(1, GD_DH)
    return pl.pallas_call(
        functools.partial(_gdn_chunk_kernel, valid_rows=valid_rows),
        grid=(bsz, seq // tc),
        in_specs=[pl.BlockSpec((None, tc, GD_CONV_CH), lambda i, t: (i, t, 0)),
                  pl.BlockSpec((None, tc, GD_V_DIM), lambda i, t: (i, t, 0)),
                  pl.BlockSpec((None, nchunk, CHUNK, 2 * GD_HV), lambda i, t: (i, t, 0, 0)),
                  pl.BlockSpec((None, nchunk, 2 * GD_HV, CHUNK), lambda i, t: (i, t, 0, 0)),
                  _full(cw.shape), _full(alog.shape), _full(dtb.shape), _full(alogt.shape), _full(dtbt.shape),
                  _full(ng.shape), _full(carry.shape), _full(s0.shape)],
        out_specs=[pl.BlockSpec((None, tc, GD_V_DIM), lambda i, t: (i, t, 0)),
                   pl.BlockSpec((None, HALO, GD_CONV_CH), lambda i, t: (i, 0, 0)),
                   pl.BlockSpec((None, GD_HV, GD_DH, GD_DH), lambda i, t: (i, 0, 0, 0))],
        out_shape=[jax.ShapeDtypeStruct((bsz, seq, GD_V_DIM), BF16),
                   jax.ShapeDtypeStruct((bsz, HALO, GD_CONV_CH), F32),
                   jax.ShapeDtypeStruct((bsz, GD_HV, GD_DH, GD_DH), F32)],
        scratch_shapes=[pltpu.VMEM((GD_CONV_CH // LANES, tc + HALO, LANES), F32),
                        pltpu.VMEM((GD_HK, GD_DH, PAIR_DV), F32),
                        pltpu.VMEM((tc, GD_K_DIM), BF16), pltpu.VMEM((tc, GD_K_DIM), BF16),
                        pltpu.VMEM((tc, GD_V_DIM), F32),
                        pltpu.VMEM((nchunk, GD_HK, CHUNK, 2 * PAIR_SQ), BF16),
                        pltpu.VMEM((nchunk, GD_HK, 2 * CHUNK, 2 * PAIR_SQ), BF16),
                        pltpu.VMEM((nchunk, 1, GD_V_DIM), F32)],
        compiler_params=_cparams(("arbitrary", "arbitrary")),
        name="gdn_chunked",
    )(qkv, z, ba4, bat4, cw, alog, dtb, alogt, dtbt, ng, carry, s0)


def _gdn_step_kernel(cur_ref, st_ref, z_ref, ba_ref, cw_ref, alog_ref, dtb_ref, ng_ref, s_ref,
                     o_ref, sout_ref, oacc_ref):
    ns = cur_ref.shape[0]
    y = cw_ref[GD_CONV - 1:GD_CONV, :] * cur_ref[...]
    for i in range(GD_CONV - 1):
        y = y + cw_ref[i:i + 1, :] * st_ref[:, i, :]
    act = _silu(y)
    qh = _l2norm_heads(act[:, :GD_K_DIM], GD_HK, GD_DH ** -0.5)
    kh = _l2norm_heads(act[:, GD_K_DIM:2 * GD_K_DIM], GD_HK, 1.0)
    qk_t = jnp.concatenate(qh + kh, axis=0).T
    ba = ba_ref[...]
    beta = jax.nn.sigmoid(ba[:, :GD_HV])
    eg = jnp.exp(-jnp.exp(alog_ref[...]) * _softplus(ba[:, GD_HV:] + dtb_ref[...]))
    for s in range(ns):
        for h in range(GD_HV):
            g = h // (GD_HV // GD_HK)
            qcol = qk_t[:, g * ns + s:g * ns + s + 1]
            kcol = qk_t[:, (GD_HK + g) * ns + s:(GD_HK + g) * ns + s + 1]
            sd = s_ref[s, h] * eg[s:s + 1, h:h + 1]
            ks = jnp.sum(sd * kcol, axis=0, keepdims=True)
            v = act[s:s + 1, 2 * GD_K_DIM + h * GD_DH:2 * GD_K_DIM + (h + 1) * GD_DH]
            w = beta[s:s + 1, h:h + 1] * (v - ks)
            sn = sd + kcol * w
            sout_ref[s, h] = sn
            oacc_ref[s:s + 1, h * GD_DH:(h + 1) * GD_DH] = jnp.sum(sn * qcol, axis=0, keepdims=True)
    for h in range(GD_HV):
        lo, hi = h * GD_DH, (h + 1) * GD_DH
        o_ref[:, lo:hi] = _gated_rmsnorm(oacc_ref[:, lo:hi], z_ref[:, lo:hi].astype(F32),
                                         ng_ref[...]).astype(o_ref.dtype)


def gdn_step(cur, st, z, ba, cw, a_log, dt_bias, norm_g, s0, ns=8):
    n = cur.shape[0]
    alog = a_log.reshape(1, GD_HV)
    dtb = dt_bias.reshape(1, GD_HV)
    ng = norm_g.reshape(1, GD_DH)
    return pl.pallas_call(
        _gdn_step_kernel,
        grid=(n // ns,),
        in_specs=[pl.BlockSpec((ns, GD_CONV_CH), lambda i: (i, 0)),
                  pl.BlockSpec((ns, GD_CONV - 1, GD_CONV_CH), lambda i: (i, 0, 0)),
                  pl.BlockSpec((ns, GD_V_DIM), lambda i: (i, 0)),
                  pl.BlockSpec((ns, 2 * GD_HV), lambda i: (i, 0)),
                  _full(cw.shape), _full(alog.shape), _full(dtb.shape), _full(ng.shape),
                  pl.BlockSpec((ns, GD_HV, GD_DH, GD_DH), lambda i: (i, 0, 0, 0))],
        out_specs=[pl.BlockSpec((ns, GD_V_DIM), lambda i: (i, 0)),
                   pl.BlockSpec((ns, GD_HV, GD_DH, GD_DH), lambda i: (i, 0, 0, 0))],
        out_shape=[jax.ShapeDtypeStruct((n, GD_V_DIM), BF16),
                   jax.ShapeDtypeStruct((n, GD_HV, GD_DH, GD_DH), F32)],
        scratch_shapes=[pltpu.VMEM((ns, GD_V_DIM), F32)],
        compiler_params=_cparams(("arbitrary",)),
        name="gdn_step",
    )(cur, st, z, ba, cw, alog, dtb, ng, s0)


def _row_copy(src_hbm, row, dst, slot, sem):
    return pltpu.make_async_copy(src_hbm.at[pl.ds(pl.multiple_of(row * ROW_SLABS, ROW_SLABS), ROW_SLABS)],
                                 dst.at[pl.ds(pl.multiple_of(slot * ROW_SLABS, ROW_SLABS), ROW_SLABS)], sem)


def _start_row_gather(src_hbm, idx_ref, base, stride, n, dst, sem, both_queues=False):
    def issue(blk, c):
        r0 = blk * DMA_UNROLL
        for u in range(DMA_UNROLL):
            _row_copy(src_hbm, idx_ref[base + stride * (r0 + u)], dst, r0 + u, sem).start(
                priority=u % 2 if both_queues else 0)
        return c

    lax.fori_loop(0, n // DMA_UNROLL, issue, 0)


def _wait_row_gather(src_hbm, n, dst, sem):
    pltpu.make_async_copy(src_hbm.at[pl.ds(0, n * ROW_SLABS)], dst, sem).wait()


def _combine_kernel(dest_ref, x_ref, pv_ref, g_ref, b_ref, y_hbm, o_ref, buf_ref, sem, *, first_block):
    tt = o_ref.shape[0]
    i = pl.program_id(0)
    slot = i % 2

    def start(blk, into):
        base = (blk + first_block) * tt * TOP_K
        for k in range(TOP_K):
            _start_row_gather(y_hbm, dest_ref, base + k, TOP_K, tt, buf_ref.at[into, k], sem.at[into, k],
                              both_queues=True)

    @pl.when(i == 0)
    def _():
        start(0, 0)

    @pl.when(i + 1 < pl.num_programs(0))
    def _():
        start(i + 1, 1 - slot)

    for k in range(TOP_K):
        _wait_row_gather(y_hbm, tt, buf_ref.at[slot, k], sem.at[slot, k])
    pv = pv_ref[...]
    slabs = []
    for s in range(ROW_SLABS):
        y = pv[:, 0:1] * _slab(buf_ref.at[slot, 0], s, tt) + pv[:, 1:2] * _slab(buf_ref.at[slot, 1], s, tt)
        slabs.append(DEEPNORM_ALPHA * x_ref[:, s * LANES:(s + 1) * LANES] + y)
    mu = sum(jnp.sum(r, axis=-1, keepdims=True) for r in slabs) * (1.0 / D_MODEL)
    var = sum(jnp.sum((r - mu) * (r - mu), axis=-1, keepdims=True) for r in slabs) * (1.0 / D_MODEL)
    inv = lax.rsqrt(var + LN_EPS)
    for s in range(ROW_SLABS):
        cs = slice(s * LANES, (s + 1) * LANES)
        o_ref[:, cs] = (slabs[s] - mu) * inv * g_ref[:, cs] + b_ref[:, cs]


def combine(dest, x, pv, g, b, yb3, tt, first_block, n_blocks):
    grid_spec = pltpu.PrefetchScalarGridSpec(
        num_scalar_prefetch=1,
        grid=(n_blocks,),
        in_specs=[pl.BlockSpec((tt, D_MODEL), lambda i, dst: (i + first_block, 0)),
                  pl.BlockSpec((tt, TOP_K), lambda i, dst: (i + first_block, 0)),
                  pl.BlockSpec(g.shape, lambda i, dst: (0, 0)),
                  pl.BlockSpec(b.shape, lambda i, dst: (0, 0)),
                  pl.BlockSpec(memory_space=pl.ANY)],
        out_specs=pl.BlockSpec((tt, D_MODEL), lambda i, dst: (i, 0)),
        scratch_shapes=[pltpu.VMEM((2, TOP_K, tt * ROW_SLABS, LANES), F32), pltpu.SemaphoreType.DMA((2, TOP_K))],
    )
    return pl.pallas_call(
        functools.partial(_combine_kernel, first_block=first_block),
        grid_spec=grid_spec,
        out_shape=jax.ShapeDtypeStruct((n_blocks * tt, D_MODEL), F32),
        compiler_params=_cparams(("arbitrary",)),
        name="moe_combine",
    )(dest, x, pv, g, b, yb3)


def moe_experts(xb, topi, w1, w3, w2, tm):
    n = topi.shape[0]
    eid = topi.reshape(-1)
    onehot = (eid[:, None] == jnp.arange(N_EXPERTS, dtype=jnp.int32)[None, :]).astype(jnp.int32)
    csum = jnp.cumsum(onehot, axis=0)
    rank = jnp.sum((csum - onehot) * onehot, axis=1)
    counts = csum[-1]
    padded = (counts + tm - 1) // tm * tm
    pends = jnp.cumsum(padded)
    pstarts = pends - padded
    dest = (pstarts[eid] + rank).astype(jnp.int32)
    nb = (n * TOP_K + tm - 1) // tm + N_EXPERTS
    n_rows = nb * tm
    block_start = jnp.arange(nb, dtype=jnp.int32) * tm
    block_e = jnp.minimum(jnp.searchsorted(pends, block_start, side="right"), N_EXPERTS - 1).astype(jnp.int32)
    n_used = (pends[-1] // tm).astype(jnp.int32).reshape(1)
    n_chunks = n // TOKEN_CHUNK
    chunk_end = csum[TOKEN_CHUNK * TOP_K - 1::TOKEN_CHUNK * TOP_K]
    ends = chunk_end[:, block_e]
    first_entry = block_start - pstarts[block_e]
    last_entry = jnp.minimum(first_entry + tm, counts[block_e]) - 1
    chunk0 = jnp.minimum(jnp.sum(ends <= first_entry[None, :], axis=0), n_chunks - 1).astype(jnp.int32)
    chunk1 = jnp.minimum(jnp.sum(ends <= last_entry[None, :], axis=0), n_chunks - 1).astype(jnp.int32)
    chunk_cnt = jnp.maximum(chunk1 - chunk0 + 1, 1).astype(jnp.int32)
    dest_t = dest.reshape(n, TOP_K).T

    yb3 = swiglu_experts(block_e, n_used, chunk0, chunk_cnt, first_entry.astype(jnp.int32),
                         chunk_end.reshape(-1).astype(jnp.int32), xb, dest_t, w1, w3, w2, n_rows, tm)
    return yb3, dest


def kernel(x_prompt, x_sample, state_conv_a, state_conv_b, state_delta, meta_tokens, ln_g, ln_b, sc_w_in, sc_conv, sc_w_out, ffn_w1, ffn_w3, ffn_w2, gd_w_in, gd_conv, gd_a_log, gd_dt_bias, gd_norm_g, gd_w_out, moe_router, moe_w1, moe_w3, moe_w2):
    bsz, seq, d = x_prompt.shape
    n_s = x_sample.shape[0]
    n_small = N_META + n_s
    n_p = bsz * seq

    def row(v):
        return v.reshape(1, -1)

    sc_w_in_b = sc_w_in[0].astype(BF16)
    sc_w_out_b = sc_w_out[0].astype(BF16)
    ffn_w1_b, ffn_w3_b, ffn_w2_b = ffn_w1.astype(BF16), ffn_w3.astype(BF16), ffn_w2.astype(BF16)
    gd_w_qkv_b = gd_w_in[0][:, :GD_CONV_CH].astype(BF16)
    gd_w_z_b = gd_w_in[0][:, GD_CONV_CH:GD_CONV_CH + GD_V_DIM].astype(BF16)
    gd_w_ba_b = gd_w_in[0][:, GD_CONV_CH + GD_V_DIM:].astype(BF16)
    gd_w_out_b = gd_w_out[0].astype(BF16)
    moe_w1_b, moe_w3_b, moe_w2_b = moe_w1[0].astype(BF16), moe_w3[0].astype(BF16), moe_w2[0].astype(BF16)

    x_small = jnp.concatenate([meta_tokens.astype(F32), x_sample.reshape(n_s, d)], axis=0)

    xa_s, ch_s = l0_mix_small(x_small, state_conv_a[0, :, 0], state_conv_a[0, :, 1], sc_w_in_b, sc_w_out_b,
                              sc_conv[0], row(ln_g[0, 0]), row(ln_b[0, 0]))
    xa_p, tail_a = l0_mix_prompt(x_prompt, sc_w_in_b, sc_w_out_b, sc_conv[0], row(ln_g[0, 0]), row(ln_b[0, 0]),
                                 ch_s[N_META - HALO:N_META])
    def dense_ffn(x, tm):
        return swiglu_ln(x, ffn_w1_b[0], ffn_w3_b[0], ffn_w2_b[0], row(ln_g[0, 1]), row(ln_b[0, 1]), tm)

    xb_s = dense_ffn(xa_s, n_small)
    xb_p = dense_ffn(xa_p.reshape(n_p, d), 512)

    qkv_s, z_s, ba_s = gdn_inproj(xb_s, gd_w_qkv_b, gd_w_z_b, gd_w_ba_b, n_small)
    qkv_p, z_p, ba_p = gdn_inproj(xb_p, gd_w_qkv_b, gd_w_z_b, gd_w_ba_b, 1024)

    pad = CHUNK - N_META

    def meta_pad(a):
        return jnp.pad(a[:N_META], ((0, pad), (0, 0)))[None]

    ba_m = meta_pad(ba_s)
    o_m, _, s_meta = gdn_chunked(meta_pad(qkv_s), meta_pad(z_s), ba_m, jnp.swapaxes(ba_m, 1, 2), gd_conv[0],
                                 gd_a_log[0], gd_dt_bias[0], gd_norm_g[0],
                                 jnp.zeros((HALO, GD_CONV_CH), F32), jnp.zeros((GD_HV, GD_DH, GD_DH), F32),
                                 CHUNK, valid_rows=N_META)
    o_smp, s_smp = gdn_step(qkv_s[N_META:], state_conv_b[0], z_s[N_META:], ba_s[N_META:], gd_conv[0],
                            gd_a_log[0], gd_dt_bias[0], gd_norm_g[0], state_delta[0])
    ba_p3 = ba_p.reshape(bsz, seq, 2 * GD_HV)
    o_p, tail_b, s_p = gdn_chunked(qkv_p.reshape(bsz, seq, GD_CONV_CH), z_p.reshape(bsz, seq, GD_V_DIM), ba_p3,
                                   jnp.swapaxes(ba_p3, 1, 2), gd_conv[0], gd_a_log[0], gd_dt_bias[0],
                                   gd_norm_g[0], qkv_s[N_META - HALO:N_META], s_meta[0], 256)
    tt = TOKEN_BLOCK
    small_pad = ((0, tt - n_small), (0, 0))
    o_s = jnp.pad(jnp.concatenate([o_m[0, :N_META], o_smp], axis=0), small_pad)
    xc, xc_b, topi, topv = matmul_ln_router(o_p.reshape(n_p, GD_V_DIM), xb_p, o_s, jnp.pad(xb_s, small_pad),
                                            gd_w_out_b, row(ln_g[1, 0]), row(ln_b[1, 0]), moe_router[0])
    yb3, dest = moe_experts(xc_b, topi, moe_w1_b, moe_w3_b, moe_w2_b, EXPERT_BLOCK)
    xd_p = combine(dest, xc, topv, row(ln_g[1, 1]), row(ln_b[1, 1]), yb3, tt, 0, n_p // tt)
    xd_s = combine(dest, xc, topv, row(ln_g[1, 1]), row(ln_b[1, 1]), yb3, tt, n_p // tt, 1)

    y_prompt = xd_p.reshape(bsz, seq, d)
    y_sample = xd_s[N_META:n_small].reshape(n_s, 1, d)
    new_conv_a_prompt = tail_a[None, :, HALO - (SC_WIDTH - 1):]
    new_conv_b_prompt = tail_b[None, :, HALO - (GD_CONV - 1):]
    new_delta_prompt = s_p[None]
    new_conv_a_sample = jnp.stack([state_conv_a[0, :, 1], ch_s[N_META:]], axis=1)[None]
    new_conv_b_sample = jnp.concatenate([state_conv_b[0, :, 1:], qkv_s[N_META:, None]], axis=1)[None]
    new_delta_sample = s_smp[None]
    return (y_prompt, y_sample, new_conv_a_prompt, new_conv_b_prompt, new_delta_prompt,
            new_conv_a_sample, new_conv_b_sample, new_delta_sample)
```
